```python
import math
import jax, jax.numpy as jnp
from jax import lax
import numpy as np

D_MODEL = 2048
BATCH = 8
SEQ = 2048
DEPTH = 2

CTX_LEN = 256
GRID_W = 64
MLP_HIDDEN = 4 * D_MODEL
NORM_EPS = 1e-6
NEG_INF = -1e30
ROPE_BASE = 10000.0
N_EVEN = (DEPTH + 1) // 2
N_ODD = DEPTH // 2

NA_HEADS = 8
NA_HEAD_DIM = 128
NA_WIDTH = NA_HEADS * NA_HEAD_DIM
NA_WIN_ROWS = 8
NA_WIN_COLS = 16
NA_QCOLS = 16
NA_KCOLS = NA_QCOLS + NA_WIN_COLS

S5_WIDTH = D_MODEL // 2
S5_GROUP = 16
S5_GROUPS = S5_WIDTH // S5_GROUP
S5_STATE = 64
S5_DT_MIN = 1e-3
S5_DT_MAX = 1e-1

AB_IN = 3 * NA_WIDTH + S5_WIDTH
AB_OUT = NA_WIDTH + S5_WIDTH

GLA_HEADS = 4
GLA_DK = D_MODEL // 2 // GLA_HEADS
GLA_DV = D_MODEL // GLA_HEADS
GLA_QK = GLA_HEADS * GLA_DK
GLA_VW = GLA_HEADS * GLA_DV
GLA_RANK = 16
GLA_TAU = 16.0
GLA_CHUNK = 64
GLA_IN = 2 * GLA_QK + 2 * GLA_VW + 2 * GLA_RANK

kernel_name = 'hybrid_natten_s5_gla_dit'


def rmsnorm(x, g):
    xf = x.astype(jnp.float32)
    return xf * lax.rsqrt(jnp.mean(xf * xf, axis=-1, keepdims=True) + NORM_EPS) * g.astype(jnp.float32)


def modulate(h, shift, scale):
    return h * (1.0 + scale) + shift


def sq_relu_mlp(h, w1, w2):
    return jnp.square(jax.nn.relu(h @ w1)) @ w2


def rope_1d(x, pos):
    d = x.shape[-1]
    freqs = ROPE_BASE ** (-jnp.arange(0, d, 2, dtype=jnp.float32) / d)
    ang = pos[:, None] * freqs[None, :]
    cos = jnp.cos(ang)[None, :, None, :]
    sin = jnp.sin(ang)[None, :, None, :]
    x1, x2 = x[..., : d // 2], x[..., d // 2:]
    return jnp.concatenate([x1 * cos - x2 * sin, x1 * sin + x2 * cos], axis=-1)


def axial_rope(x, row_pos, col_pos):
    half = x.shape[-1] // 2
    return jnp.concatenate([rope_1d(x[..., :half], row_pos), rope_1d(x[..., half:], col_pos)], axis=-1)


def ctx_attention(q, k, v):
    s = jnp.einsum('bqhd,bkhd->bhqk', q, k) * q.shape[-1] ** -0.5
    return jnp.einsum('bhqk,bkhd->bqhd', jax.nn.softmax(s, axis=-1), v)


def neighbourhood_attention(q, k, v, kc, vc, rel_bias):
    B, S, H, dh = q.shape
    L = kc.shape[1]
    rows = S // GRID_W
    wh = min(NA_WIN_ROWS, rows)
    n_cb = GRID_W // NA_QCOLS
    scale = dh ** -0.5
    qg = q.reshape(B, rows, GRID_W, H, dh)
    kg = k.reshape(B, rows, GRID_W, H, dh)
    vg = v.reshape(B, rows, GRID_W, H, dh)

    def block(idx):
        r = idx // n_cb
        c0 = (idx % n_cb) * NA_QCOLS
        rs = jnp.clip(r - wh // 2, 0, rows - wh)
        ks0 = jnp.clip(c0 - NA_WIN_COLS // 2, 0, GRID_W - NA_KCOLS)
        qb = lax.dynamic_slice(qg, (0, r, c0, 0, 0), (B, 1, NA_QCOLS, H, dh))[:, 0]
        kb = lax.dynamic_slice(kg, (0, rs, ks0, 0, 0), (B, wh, NA_KCOLS, H, dh)).reshape(B, wh * NA_KCOLS, H, dh)
        vb = lax.dynamic_slice(vg, (0, rs, ks0, 0, 0), (B, wh, NA_KCOLS, H, dh)).reshape(B, wh * NA_KCOLS, H, dh)
        qcols = c0 + jnp.arange(NA_QCOLS)
        kcols = ks0 + jnp.arange(NA_KCOLS)
        krows = rs + jnp.arange(wh)
        win_start = jnp.clip(qcols - NA_WIN_COLS // 2, 0, GRID_W - NA_WIN_COLS)
        col_ok = (kcols[None, :] >= win_start[:, None]) & (kcols[None, :] < win_start[:, None] + NA_WIN_COLS)
        ri = krows - r + NA_WIN_ROWS - 1
        ci = jnp.clip(kcols[None, :] - qcols[:, None] + NA_WIN_COLS - 1, 0, 2 * NA_WIN_COLS - 2)
        bias = rel_bias[:, ri[:, None, None], ci[None, :, :]]
        bias = bias.transpose(0, 2, 1, 3).reshape(H, NA_QCOLS, wh * NA_KCOLS).astype(jnp.float32)
        mask = jnp.broadcast_to(col_ok[:, None, :], (NA_QCOLS, wh, NA_KCOLS)).reshape(NA_QCOLS, wh * NA_KCOLS)
        s_win = jnp.einsum('bqhd,bkhd->bhqk', qb, kb) * scale + bias
        s_win = jnp.where(mask, s_win, NEG_INF)
        s_ctx = jnp.einsum('bqhd,blhd->bhql', qb, kc) * scale
        p = jax.nn.softmax(jnp.concatenate([s_ctx, s_win], axis=-1), axis=-1)
        return (jnp.einsum('bhql,blhd->bqhd', p[..., :L], vc)
                + jnp.einsum('bhqk,bkhd->bqhd', p[..., L:], vb))

    out = lax.map(block, jnp.arange(rows * n_cb))
    out = out.reshape(rows, n_cb, B, NA_QCOLS, H, dh).transpose(2, 0, 1, 3, 4, 5)
    return out.reshape(B, S, H, dh)


def s5_discretize(lam_re, lam_im, log_dt, b_re, b_im):
    f32 = jnp.float32
    lam_re, lam_im = lam_re.astype(f32), lam_im.astype(f32)
    b_re, b_im = b_re.astype(f32), b_im.astype(f32)
    dt = jnp.exp(log_dt.astype(f32))[:, None]
    mag = jnp.exp(lam_re * dt)
    a_re = mag * jnp.cos(lam_im * dt)
    a_im = mag * jnp.sin(lam_im * dt)
    den = lam_re * lam_re + lam_im * lam_im
    f_re = ((a_re - 1.0) * lam_re + a_im * lam_im) / den
    f_im = (a_im * lam_re - (a_re - 1.0) * lam_im) / den
    bb_re = f_re[..., None] * b_re - f_im[..., None] * b_im
    bb_im = f_re[..., None] * b_im + f_im[..., None] * b_re
    return a_re, a_im, bb_re, bb_im


def _complex_linear_combine(e1, e2):
    a1r, a1i, b1r, b1i = e1
    a2r, a2i, b2r, b2i = e2
    return (a1r * a2r - a1i * a2i,
            a1r * a2i + a1i * a2r,
            a2r * b1r - a2i * b1i + b2r,
            a2r * b1i + a2i * b1r + b2i)


def s5_scan(u, lam_re, lam_im, log_dt, b_re, b_im, c_re, c_im, h0_re, h0_im, reverse):
    a_re, a_im, bb_re, bb_im = s5_discretize(lam_re, lam_im, log_dt, b_re, b_im)
    bu_re = jnp.einsum('btgc,gpc->btgp', u, bb_re)
    bu_im = jnp.einsum('btgc,gpc->btgp', u, bb_im)
    first = -1 if reverse else 0
    last = 0 if reverse else -1
    bu_re = bu_re.at[:, first].add(a_re * h0_re - a_im * h0_im)
    bu_im = bu_im.at[:, first].add(a_re * h0_im + a_im * h0_re)
    T = u.shape[1]
    ar = jnp.broadcast_to(a_re, (1, T) + a_re.shape)
    ai = jnp.broadcast_to(a_im, (1, T) + a_im.shape)
    _, _, h_re, h_im = lax.associative_scan(_complex_linear_combine, (ar, ai, bu_re, bu_im),
                                            reverse=reverse, axis=1)
    y = (jnp.einsum('btgp,gcp->btgc', h_re, c_re.astype(jnp.float32))
         - jnp.einsum('btgp,gcp->btgc', h_im, c_im.astype(jnp.float32)))
    return y, h_re[:, last], h_im[:, last]


def s5_bidirectional(ul, uc, lam_re, lam_im, log_dt, b_re, b_im, c_re, c_im, d_skip, glu_w, glu_b, ctx_out):
    B = ul.shape[0]
    grp = lambda u: u.reshape(u.shape[0], u.shape[1], S5_GROUPS, S5_GROUP).astype(jnp.float32)
    ul_g, uc_g = grp(ul), grp(uc)
    zero = jnp.zeros((B, S5_GROUPS, S5_STATE), jnp.float32)
    yl = 0.0
    yc = 0.0
    for dr, rev in enumerate((False, True)):
        prm = (lam_re[dr], lam_im[dr], log_dt[dr], b_re[dr], b_im[dr], c_re[dr], c_im[dr])
        ycd, hr, hi = s5_scan(uc_g, *prm, zero, zero, rev)
        yld, _, _ = s5_scan(ul_g, *prm, hr, hi, rev)
        yl = yl + yld
        yc = yc + ycd

    def finish(y, u):
        y = y.reshape(u.shape) + d_skip * u
        gl = jax.nn.gelu(y, approximate=False)
        return gl * jax.nn.sigmoid(gl @ glu_w + glu_b)

    return finish(yl, ul), (finish(yc, uc) if ctx_out else None)


def ab_mixer(hl, hc, w_in, w_out, rel_bias, lam_re, lam_im, log_dt, b_re, b_im, c_re, c_im,
             d_skip, glu_w, glu_b, ctx_out):
    B, S, _ = hl.shape
    L = hc.shape[1]
    cuts = [NA_WIDTH, 2 * NA_WIDTH, 3 * NA_WIDTH]
    ql, kl, vl, ul = jnp.split(hl @ w_in, cuts, axis=-1)
    qc, kc, vc, uc = jnp.split(hc @ w_in, cuts, axis=-1)
    heads = lambda z: z.reshape(z.shape[0], z.shape[1], NA_HEADS, NA_HEAD_DIM)
    kc_h, vc_h = heads(kc), heads(vc)
    na_l = neighbourhood_attention(heads(ql), heads(kl), heads(vl), kc_h, vc_h, rel_bias).reshape(B, S, NA_WIDTH)
    s5_l, s5_c = s5_bidirectional(ul, uc, lam_re, lam_im, log_dt, b_re, b_im, c_re, c_im,
                                  d_skip, glu_w, glu_b, ctx_out)
    yl = jnp.concatenate([na_l, s5_l], axis=-1) @ w_out
    if not ctx_out:
        return yl, None
    na_c = ctx_attention(heads(qc), kc_h, vc_h).reshape(B, L, NA_WIDTH)
    yc = jnp.concatenate([na_c, s5_c], axis=-1) @ w_out
    return yl, yc


def gla_chunked(q, k, v, log_a, s0):
    B, T, H, dk = q.shape
    dv = v.shape[-1]
    n = T // GLA_CHUNK
    q = q.reshape(B, n, GLA_CHUNK, H, dk)
    k = k.reshape(B, n, GLA_CHUNK, H, dk)
    v = v.reshape(B, n, GLA_CHUNK, H, dv)
    bc = jnp.cumsum(log_a.reshape(B, n, GLA_CHUNK, H, dk), axis=2)
    b_last = bc[:, :, -1:]
    q_in = q * jnp.exp(bc)
    k_in = k * jnp.exp(-bc)
    k_end = k * jnp.exp(b_last - bc)
    causal = jnp.tril(jnp.ones((GLA_CHUNK, GLA_CHUNK), dtype=bool))
    att = jnp.where(causal, jnp.einsum('bnihd,bnjhd->bnhij', q_in, k_in), 0.0)
    o_intra = jnp.einsum('bnhij,bnjhv->bnihv', att, v)
    decay = jnp.exp(b_last[:, :, 0])

    def step(state, xs):
        qi, ke, vv, dec = xs
        o = jnp.einsum('blhd,bhdv->blhv', qi, state)
        state = dec[..., None] * state + jnp.einsum('blhd,blhv->bhdv', ke, vv)
        return state, o

    xs = tuple(jnp.moveaxis(t, 1, 0) for t in (q_in, k_end, v, decay))
    s_fin, o_inter = lax.scan(step, s0, xs)
    o = o_intra + jnp.moveaxis(o_inter, 0, 1)
    return o.reshape(B, T, H, dv), s_fin


def gla_chunked_reverse(q, k, v, log_a, s0):
    o, s_fin = gla_chunked(jnp.flip(q, 1), jnp.flip(k, 1), jnp.flip(v, 1), jnp.flip(log_a, 1), s0)
    return jnp.flip(o, 1), s_fin


def gla_mixer(hl, hc, row_pos, col_pos, w_in, w_a2, b_a, norm_g, w_out, ctx_out):
    cuts = [GLA_QK, 2 * GLA_QK, 2 * GLA_QK + GLA_VW, 2 * GLA_QK + 2 * GLA_VW]

    def project(h):
        B, T, _ = h.shape
        q, k, v, g, a = jnp.split(h @ w_in, cuts, axis=-1)
        q = q.reshape(B, T, GLA_HEADS, GLA_DK).astype(jnp.float32) * GLA_DK ** -0.5
        k = k.reshape(B, T, GLA_HEADS, GLA_DK).astype(jnp.float32)
        v = v.reshape(B, T, GLA_HEADS, GLA_DV).astype(jnp.float32)
        g = g.reshape(B, T, GLA_HEADS, GLA_DV)
        log_a = [jax.nn.log_sigmoid((a[..., d * GLA_RANK:(d + 1) * GLA_RANK] @ w_a2[d] + b_a[d])
                                    .astype(jnp.float32)).reshape(B, T, GLA_HEADS, GLA_DK) / GLA_TAU
                 for d in range(2)]
        return q, k, v, g, log_a

    ql, kl, vl, gl, la_l = project(hl)
    qc, kc, vc, gc, la_c = project(hc)
    ql = axial_rope(ql, row_pos, col_pos)
    kl = axial_rope(kl, row_pos, col_pos)
    B = hl.shape[0]
    s0 = jnp.zeros((B, GLA_HEADS, GLA_DK, GLA_DV), jnp.float32)
    oc_f, sc_f = gla_chunked(qc, kc, vc, la_c[0], s0)
    oc_b, sc_b = gla_chunked_reverse(qc, kc, vc, la_c[1], s0)
    ol_f, _ = gla_chunked(ql, kl, vl, la_l[0], sc_f)
    ol_b, _ = gla_chunked_reverse(ql, kl, vl, la_l[1], sc_b)

    def finish(o, g):
        Bo, T = o.shape[0], o.shape[1]
        o = rmsnorm(o, norm_g) * jax.nn.silu(g)
        return o.reshape(Bo, T, GLA_VW) @ w_out

    return finish(ol_f + ol_b, gl), (finish(oc_f + oc_b, gc) if ctx_out else None)


def setup_inputs(seed: int = 0) -> dict:
    key = jax.random.key(seed)
    keys = iter(jax.random.split(key, 32))
    f32 = jnp.float32

    def nrm(shape, scale):
        return jax.random.normal(next(keys), shape, f32) * scale

    D = D_MODEL
    G, P, CG = S5_GROUPS, S5_STATE, S5_GROUP
    x = nrm((BATCH, SEQ, D), 1.0)
    c = nrm((BATCH, D), 1.0)
    ctx = nrm((BATCH, CTX_LEN, D), 1.0)
    c_ctx = nrm((D,), 1.0)
    ada_w = nrm((DEPTH, D, 6 * D), D ** -0.5)
    ada_b = nrm((DEPTH, 6 * D), 0.02)
    norm1_g = 1.0 + nrm((DEPTH, D), 0.05)
    norm2_g = 1.0 + nrm((DEPTH, D), 0.05)
    mlp_w1 = nrm((DEPTH, D, MLP_HIDDEN), D ** -0.5)
    mlp_w2 = nrm((DEPTH, MLP_HIDDEN, D), MLP_HIDDEN ** -0.5)
    final_g = 1.0 + nrm((D,), 0.05)
    ab_w_in = nrm((N_EVEN, D, AB_IN), D ** -0.5)
    ab_w_out = nrm((N_EVEN, AB_OUT, D), AB_OUT ** -0.5)
    na_rel_bias = nrm((N_EVEN, NA_HEADS, 2 * NA_WIN_ROWS - 1, 2 * NA_WIN_COLS - 1), 0.1)
    s5_lambda_re = -0.5 + nrm((N_EVEN, 2, G, P), 0.01)
    s5_lambda_im = jnp.pi * jnp.arange(P, dtype=f32) + nrm((N_EVEN, 2, G, P), 0.01)
    s5_log_dt = jax.random.uniform(next(keys), (N_EVEN, 2, G), f32,
                                   math.log(S5_DT_MIN), math.log(S5_DT_MAX))
    s5_b_re = nrm((N_EVEN, 2, G, P, CG), (2 * CG) ** -0.5)
    s5_b_im = nrm((N_EVEN, 2, G, P, CG), (2 * CG) ** -0.5)
    s5_c_re = nrm((N_EVEN, 2, G, CG, P), P ** -0.5)
    s5_c_im = nrm((N_EVEN, 2, G, CG, P), P ** -0.5)
    s5_d = nrm((N_EVEN, S5_WIDTH), 1.0)
    s5_glu_w = nrm((N_EVEN, S5_WIDTH, S5_WIDTH), S5_WIDTH ** -0.5)
    s5_glu_b = nrm((N_EVEN, S5_WIDTH), 0.02)
    gla_w_in = nrm((N_ODD, D, GLA_IN), D ** -0.5)
    gla_w_a2 = nrm((N_ODD, 2, GLA_RANK, GLA_QK), GLA_RANK ** -0.5)
    gla_b_a = nrm((N_ODD, 2, GLA_QK), 0.1)
    gla_norm_g = 1.0 + nrm((N_ODD, GLA_DV), 0.05)
    gla_w_out = nrm((N_ODD, GLA_VW, D), GLA_VW ** -0.5)
    return {'x': x, 'c': c, 'ctx': ctx, 'c_ctx': c_ctx,
            'ada_w': ada_w, 'ada_b': ada_b, 'norm1_g': norm1_g, 'norm2_g': norm2_g,
            'mlp_w1': mlp_w1, 'mlp_w2': mlp_w2, 'final_g': final_g,
            'ab_w_in': ab_w_in, 'ab_w_out': ab_w_out, 'na_rel_bias': na_rel_bias,
            's5_lambda_re': s5_lambda_re, 's5_lambda_im': s5_lambda_im, 's5_log_dt': s5_log_dt,
            's5_b_re': s5_b_re, 's5_b_im': s5_b_im, 's5_c_re': s5_c_re, 's5_c_im': s5_c_im,
            's5_d': s5_d, 's5_glu_w': s5_glu_w, 's5_glu_b': s5_glu_b,
            'gla_w_in': gla_w_in, 'gla_w_a2': gla_w_a2, 'gla_b_a': gla_b_a,
            'gla_norm_g': gla_norm_g, 'gla_w_out': gla_w_out}


def reference(x, c, ctx, c_ctx, ada_w, ada_b, norm1_g, norm2_g, mlp_w1, mlp_w2, final_g,
              ab_w_in, ab_w_out, na_rel_bias, s5_lambda_re, s5_lambda_im, s5_log_dt,
              s5_b_re, s5_b_im, s5_c_re, s5_c_im, s5_d, s5_glu_w, s5_glu_b,
              gla_w_in, gla_w_a2, gla_b_a, gla_norm_g, gla_w_out):
    f32 = jnp.float32
    S = x.shape[1]
    t = jnp.arange(S)
    row_pos = (t // GRID_W).astype(f32)
    col_pos = (t % GRID_W).astype(f32)
    xl = x.astype(f32)
    xc = ctx.astype(f32)
    for i in range(DEPTH):
        last = i == DEPTH - 1
        mod_l = (jax.nn.silu(c.astype(f32)) @ ada_w[i] + ada_b[i])[:, None, :]
        mod_c = (jax.nn.silu(c_ctx.astype(f32)) @ ada_w[i] + ada_b[i])[None, None, :]
        sh1l, sc1l, g1l, sh2l, sc2l, g2l = jnp.split(mod_l, 6, axis=-1)
        sh1c, sc1c, g1c, sh2c, sc2c, g2c = jnp.split(mod_c, 6, axis=-1)
        hl = modulate(rmsnorm(xl, norm1_g[i]), sh1l, sc1l)
        hc = modulate(rmsnorm(xc, norm1_g[i]), sh1c, sc1c)
        j = i // 2
        if i % 2 == 0:
            yl, yc = ab_mixer(hl, hc, ab_w_in[j], ab_w_out[j], na_rel_bias[j],
                              s5_lambda_re[j], s5_lambda_im[j], s5_log_dt[j],
                              s5_b_re[j], s5_b_im[j], s5_c_re[j], s5_c_im[j],
                              s5_d[j], s5_glu_w[j], s5_glu_b[j], not last)
        else:
            yl, yc = gla_mixer(hl, hc, row_pos, col_pos, gla_w_in[j], gla_w_a2[j], gla_b_a[j],
                               gla_norm_g[j], gla_w_out[j], not last)
        xl = xl + g1l * yl
        xl = xl + g2l * sq_relu_mlp(modulate(rmsnorm(xl, norm2_g[i]), sh2l, sc2l), mlp_w1[i], mlp_w2[i])
        if not last:
            xc = xc + g1c * yc
            xc = xc + g2c * sq_relu_mlp(modulate(rmsnorm(xc, norm2_g[i]), sh2c, sc2c), mlp_w1[i], mlp_w2[i])
    return rmsnorm(xl, final_g).astype(x.dtype)
```

```python
import functools
import math

import numpy as np
import jax
import jax.numpy as jnp
from jax import lax
from jax.experimental import pallas as pl
from jax.experimental.pallas import tpu as pltpu

F32 = jnp.float32
BF16 = jnp.bfloat16

D = 2048
B = 8
S = 2048
L = 256
GRID_W = 64
ROWS = S // GRID_W
NLAT = B * S
NCTX = B * L
NT = NLAT + NCTX
MLP_H = 4 * D
EPS = 1e-6
NEG_INF = -1e30

NA_H = 8
NA_DH = 128
NA_W = NA_H * NA_DH
NA_SCALE = NA_DH ** -0.5
NA_QROWS = 4
NA_KROWS = 12
NA_QT = NA_QROWS * GRID_W
NA_KT = NA_KROWS * GRID_W

S5_W = D // 2
S5_CG = 16
S5_G = S5_W // S5_CG
S5_P = 64
S5_LC = 16
S5_CW = S5_LC * S5_CG
S5_NLAT = S // S5_LC
S5_NCTX = L // S5_LC
S5_N = B * (S5_NLAT + S5_NCTX)
S5_NT = 384

GLA_H = 4
GLA_DK = 256
GLA_DV = 512
GLA_QK = GLA_H * GLA_DK
GLA_VW = GLA_H * GLA_DV
GLA_RANK = 16
GLA_TAU = 16.0
GLA_C = 64
GLA_MAIN = 2 * GLA_QK + 2 * GLA_VW
ROPE_BASE = 10000.0

VMEM_LIMIT = 56 * 1024 * 1024

_NT_DIMS = (((1,), (1,)), ((), ()))


def _params(*sem):
    return pltpu.CompilerParams(dimension_semantics=sem, vmem_limit_bytes=VMEM_LIMIT)


def _mod_row(i, tm):
    return jnp.minimum((i * tm) // S, B)


def _ada_kernel(c_ref, w_ref, b_ref, o_ref):
    c = c_ref[...]
    s = c * jax.nn.sigmoid(c)
    o_ref[...] = jnp.dot(s.astype(BF16), w_ref[...].astype(BF16),
                         preferred_element_type=F32) + b_ref[...]


def _ada_mod(cvec, ada_w, ada_b):
    depth = ada_w.shape[0]
    tn = 512
    return pl.pallas_call(
        _ada_kernel,
        grid=(depth, 6 * D // tn),
        in_specs=[pl.BlockSpec((16, D), lambda l, j: (0, 0)),
                  pl.BlockSpec((None, D, tn), lambda l, j: (l, 0, j)),
                  pl.BlockSpec((None, 1, tn), lambda l, j: (l, 0, j))],
        out_specs=pl.BlockSpec((None, 16, tn), lambda l, j: (l, 0, j)),
        out_shape=jax.ShapeDtypeStruct((depth, 16, 6 * D), F32),
        compiler_params=_params("parallel", "parallel"),
        name="ada_mod",
    )(cvec, ada_w, ada_b.reshape(depth, 1, 6 * D))


def _normmod_kernel(x_ref, sh_ref, sc_ref, g_ref, o_ref):
    x = x_ref[...]
    ms = jnp.mean(x * x, axis=-1, keepdims=True)
    h = x * lax.rsqrt(ms + EPS) * g_ref[...]
    o_ref[...] = (h * (1.0 + sc_ref[...]) + sh_ref[...]).astype(o_ref.dtype)


def _normmod(x, mod, g, shift_idx, scale_idx, rows):
    tm = 512
    return pl.pallas_call(
        _normmod_kernel,
        grid=(rows // tm,),
        in_specs=[pl.BlockSpec((tm, D), lambda i: (i, 0)),
                  pl.BlockSpec((None, 1, D), lambda i: (_mod_row(i, tm), 0, shift_idx)),
                  pl.BlockSpec((None, 1, D), lambda i: (_mod_row(i, tm), 0, scale_idx)),
                  pl.BlockSpec((1, D), lambda i: (0, 0))],
        out_specs=pl.BlockSpec((tm, D), lambda i: (i, 0)),
        out_shape=jax.ShapeDtypeStruct((rows, D), BF16),
        compiler_params=_params("parallel"),
        name="normmod",
    )(x, mod, mod, g.reshape(1, D))


def _mm_kernel(a_ref, w_ref, o_ref, *, relu2):
    acc = jnp.dot(a_ref[...], w_ref[...], preferred_element_type=F32)
    if relu2:
        acc = jnp.square(jnp.maximum(acc, 0.0))
    o_ref[...] = acc.astype(o_ref.dtype)


def _mm(a, w, rows, *, relu2=False, tm=1024, tn=1024):
    k = a.shape[1]
    n = w.shape[1]
    tn = min(tn, n)
    return pl.pallas_call(
        functools.partial(_mm_kernel, relu2=relu2),
        grid=(rows // tm, n // tn),
        in_specs=[pl.BlockSpec((tm, k), lambda i, j: (i, 0)),
                  pl.BlockSpec((k, tn), lambda i, j: (0, j))],
        out_specs=pl.BlockSpec((tm, tn), lambda i, j: (i, j)),
        out_shape=jax.ShapeDtypeStruct((rows, n), BF16),
        compiler_params=_params("parallel", "parallel"),
        name="mm_relu2" if relu2 else "mm",
    )(a, w)


def _mm_res_kernel(*refs, n_lhs, nk, final_norm):
    a_refs = refs[:n_lhs]
    w_refs = refs[n_lhs:2 * n_lhs]
    res_ref, gate_ref = refs[2 * n_lhs:2 * n_lhs + 2]
    pos = 2 * n_lhs + 2
    fg_ref = None
    if final_norm:
        fg_ref = refs[pos]
        pos += 1
    o_ref = refs[pos]
    acc_ref = refs[pos + 1] if nk > 1 else None

    part = None
    for a_ref, w_ref in zip(a_refs, w_refs):
        d = jnp.dot(a_ref[...], w_ref[...], preferred_element_type=F32)
        part = d if part is None else part + d

    def finish(acc):
        y = res_ref[...] + gate_ref[...] * acc
        if final_norm:
            ms = jnp.mean(y * y, axis=-1, keepdims=True)
            y = y * lax.rsqrt(ms + EPS) * fg_ref[...]
        o_ref[...] = y

    if nk == 1:
        finish(part)
    else:
        kk = pl.program_id(1)

        @pl.when(kk == 0)
        def _():
            acc_ref[...] = part

        @pl.when(kk > 0)
        def _():
            acc_ref[...] += part

        @pl.when(kk == nk - 1)
        def _():
            finish(acc_ref[...])


def _mm_res(a_list, w_list, resid, mod, gate_idx, rows, *, final_g=None, tm=512, tk=1024):
    n_lhs = len(a_list)
    kdim = a_list[0].shape[1]
    nk = kdim // tk
    final_norm = final_g is not None
    in_specs = ([pl.BlockSpec((tm, tk), lambda i, k: (i, k)) for _ in a_list]
                + [pl.BlockSpec((tk, D), lambda i, k: (k, 0)) for _ in w_list]
                + [pl.BlockSpec((tm, D), lambda i, k: (i, 0)),
                   pl.BlockSpec((None, 1, D), lambda i, k: (_mod_row(i, tm), 0, gate_idx))])
    args = list(a_list) + list(w_list) + [resid, mod]
    if final_norm:
        in_specs.append(pl.BlockSpec((1, D), lambda i, k: (0, 0)))
        args.append(final_g.reshape(1, D))
    return pl.pallas_call(
        functools.partial(_mm_res_kernel, n_lhs=n_lhs, nk=nk, final_norm=final_norm),
        grid=(rows // tm, nk),
        in_specs=in_specs,
        out_specs=pl.BlockSpec((tm, D), lambda i, k: (i, 0)),
        out_shape=jax.ShapeDtypeStruct((rows, D), F32),
        scratch_shapes=[pltpu.VMEM((tm, D), F32)] if nk > 1 else [],
        compiler_params=_params("parallel", "arbitrary"),
        name="mm_res",
    )(*args)


def _na_tables(rel_bias):
    ri_l, ci_l, ok_l = [], [], []
    rq = np.arange(NA_QROWS)[:, None, None, None]
    cq = np.arange(GRID_W)[None, :, None, None]
    rk = np.arange(NA_KROWS)[None, None, :, None]
    ck = np.arange(GRID_W)[None, None, None, :]
    for r0, start in ((0, 0), (NA_QROWS, 0), (ROWS - NA_QROWS, ROWS - NA_KROWS)):
        r = r0 + rq
        krow = start + rk
        rs = np.clip(r - 4, 0, ROWS - 8)
        ws = np.clip(cq - 8, 0, GRID_W - 16)
        ok = (krow >= rs) & (krow < rs + 8) & (ck >= ws) & (ck < ws + 16)
        ri = np.broadcast_to(np.clip(krow - r + 7, 0, 14), ok.shape)
        ci = np.broadcast_to(np.clip(ck - cq + 15, 0, 30), ok.shape)
        ri_l.append(ri.reshape(NA_QT, NA_KT))
        ci_l.append(ci.reshape(NA_QT, NA_KT))
        ok_l.append(ok.reshape(NA_QT, NA_KT))
    ri = np.stack(ri_l)
    ci = np.stack(ci_l)
    ok = np.stack(ok_l)
    bias = rel_bias.astype(F32)[:, ri, ci]
    return jnp.where(ok[None], bias, NEG_INF)


def _na_kernel(q_ref, k0_ref, k1_ref, k2_ref, v0_ref, v1_ref, v2_ref, kc_ref, vc_ref,
               tab_ref, o_ref):
    i = pl.program_id(1)
    q = q_ref[...]
    s_c = lax.dot_general(q, kc_ref[...], _NT_DIMS, preferred_element_type=F32) * NA_SCALE
    m_c = jnp.max(s_c, axis=-1, keepdims=True)

    @pl.when(i < ROWS // NA_QROWS)
    def _():
        s_w = []
        m = m_c
        for d, k_ref in enumerate((k0_ref, k1_ref, k2_ref)):
            s = lax.dot_general(q, k_ref[...], _NT_DIMS, preferred_element_type=F32) * NA_SCALE
            s = s + tab_ref[:, d * NA_QT:(d + 1) * NA_QT]
            s_w.append(s)
            m = jnp.maximum(m, jnp.max(s, axis=-1, keepdims=True))
        p_c = jnp.exp(s_c - m)
        l = jnp.sum(p_c, axis=-1, keepdims=True)
        o = jnp.dot(p_c.astype(BF16), vc_ref[...], preferred_element_type=F32)
        for s, v_ref in zip(s_w, (v0_ref, v1_ref, v2_ref)):
            p = jnp.exp(s - m)
            l = l + jnp.sum(p, axis=-1, keepdims=True)
            o = o + jnp.dot(p.astype(BF16), v_ref[...], preferred_element_type=F32)
        o_ref[...] = (o / l).astype(o_ref.dtype)

    @pl.when(i == ROWS // NA_QROWS)
    def _():
        p_c = jnp.exp(s_c - m_c)
        l = jnp.sum(p_c, axis=-1, keepdims=True)
        o = jnp.dot(p_c.astype(BF16), vc_ref[...], preferred_element_type=F32)
        o_ref[...] = (o / l).astype(o_ref.dtype)


def _na_attention(qkv, table):
    ng = ROWS // NA_QROWS
    blk = S // NA_QT
    ctx0 = NLAT // NA_QT

    def qrow(h, i, b):
        return jnp.where(i < ng, b * blk + i, ctx0 + b)

    def krow(d):
        return lambda h, i, b: (b * blk + jnp.clip(i - 1, 0, blk - 3) + d, NA_H + h)

    def vrow(d):
        return lambda h, i, b: (b * blk + jnp.clip(i - 1, 0, blk - 3) + d, 2 * NA_H + h)

    def pat(h, i, b):
        return (h, jnp.where(i == 0, 0, jnp.where(i >= ng - 1, 2, 1)), 0, 0)

    tile = (NA_QT, NA_DH)
    in_specs = ([pl.BlockSpec(tile, lambda h, i, b: (qrow(h, i, b), h))]
                + [pl.BlockSpec(tile, krow(d)) for d in range(3)]
                + [pl.BlockSpec(tile, vrow(d)) for d in range(3)]
                + [pl.BlockSpec(tile, lambda h, i, b: (ctx0 + b, NA_H + h)),
                   pl.BlockSpec(tile, lambda h, i, b: (ctx0 + b, 2 * NA_H + h)),
                   pl.BlockSpec((None, None, NA_QT, NA_KT), pat)])
    return pl.pallas_call(
        _na_kernel,
        grid=(NA_H, ng + 1, B),
        in_specs=in_specs,
        out_specs=pl.BlockSpec(tile, lambda h, i, b: (qrow(h, i, b), h)),
        out_shape=jax.ShapeDtypeStruct((NT, NA_W), BF16),
        compiler_params=_params("parallel", "parallel", "parallel"),
        name="na_attention",
    )(*([qkv] * 9), table)


def _s5_matrices(lam_re, lam_im, log_dt, b_re, b_im, c_re, c_im):
    hp = lax.Precision.HIGHEST
    lam_re, lam_im = lam_re.astype(F32), lam_im.astype(F32)
    b_re, b_im = b_re.astype(F32), b_im.astype(F32)
    c_re, c_im = c_re.astype(F32), c_im.astype(F32)
    dt = jnp.exp(log_dt.astype(F32))[..., None]
    mag = jnp.exp(lam_re * dt)
    a_re = mag * jnp.cos(lam_im * dt)
    a_im = mag * jnp.sin(lam_im * dt)
    den = lam_re * lam_re + lam_im * lam_im
    f_re = ((a_re - 1.0) * lam_re + a_im * lam_im) / den
    f_im = (a_im * lam_re - (a_re - 1.0) * lam_im) / den
    bb_re = f_re[..., None] * b_re - f_im[..., None] * b_im
    bb_im = f_re[..., None] * b_im + f_im[..., None] * b_re

    pr, pi = [jnp.ones_like(a_re)], [jnp.zeros_like(a_im)]
    for _ in range(S5_LC):
        pr.append(pr[-1] * a_re - pi[-1] * a_im)
        pi.append(pr[-2] * a_im + pi[-1] * a_re)
    pw_re = jnp.stack(pr)
    pw_im = jnp.stack(pi)

    ab_re = pw_re[..., None] * bb_re[None] - pw_im[..., None] * bb_im[None]
    ab_im = pw_re[..., None] * bb_im[None] + pw_im[..., None] * bb_re[None]
    kern = (jnp.einsum('dgop,tdgpc->tdgoc', c_re, ab_re, precision=hp)
            - jnp.einsum('dgop,tdgpc->tdgoc', c_im, ab_im, precision=hp))

    s_i = np.arange(S5_LC)[:, None]
    t_i = np.arange(S5_LC)[None, :]
    lag_f = np.clip(t_i - s_i, 0, S5_LC - 1)
    lag_b = np.clip(s_i - t_i, 0, S5_LC - 1)
    kf = jnp.where((s_i <= t_i)[..., None, None, None], kern[lag_f, 0], 0.0)
    kb = jnp.where((s_i >= t_i)[..., None, None, None], kern[lag_b, 1], 0.0)
    tc = (kf + kb).transpose(2, 0, 4, 1, 3).reshape(S5_G, S5_CW, S5_CW)

    idx_f = np.arange(S5_LC - 1, -1, -1)
    idx_b = np.arange(S5_LC)
    def st(arr, idx, d):
        return arr[idx, d].transpose(1, 0, 3, 2).reshape(S5_G, S5_CW, S5_P)
    et = jnp.concatenate([st(ab_re, idx_f, 0), st(ab_re, idx_b, 1),
                          st(ab_im, idx_f, 0), st(ab_im, idx_b, 1)], axis=-1)

    def rd(d, powers):
        pr_ = pw_re[powers, d]
        pi_ = pw_im[powers, d]
        cr = c_re[d][None] * pr_[:, :, None, :] - c_im[d][None] * pi_[:, :, None, :]
        ci = c_re[d][None] * pi_[:, :, None, :] + c_im[d][None] * pr_[:, :, None, :]
        to = lambda z: z.transpose(1, 3, 0, 2).reshape(S5_G, S5_P, S5_CW)
        return to(cr), to(-ci)
    fr, fi = rd(0, np.arange(1, S5_LC + 1))
    br, bi = rd(1, np.arange(S5_LC, 0, -1))
    z = jnp.zeros_like(fr)
    ft = jnp.concatenate([fr, z, fi, z, z, br, z, bi], axis=1)

    a16_re = jnp.concatenate([pw_re[S5_LC, 0], pw_re[S5_LC, 1]], axis=-1)[:, None, :]
    a16_im = jnp.concatenate([pw_im[S5_LC, 0], pw_im[S5_LC, 1]], axis=-1)[:, None, :]
    return tc.astype(BF16), et.astype(BF16), ft.astype(BF16), a16_re, a16_im


def _uproj_kernel(wt_ref, h_ref, o_ref):
    acc = lax.dot_general(wt_ref[...], h_ref[...], _NT_DIMS, preferred_element_type=F32)
    o_ref[...] = acc.reshape(S5_G, S5_CG, S5_NT).astype(o_ref.dtype)


def _s5_uproj(h, w_u_t):
    h2 = h.reshape(S5_N, S5_LC * D)
    return pl.pallas_call(
        _uproj_kernel,
        grid=(S5_LC, S5_N // S5_NT),
        in_specs=[pl.BlockSpec((S5_W, D), lambda t, n: (0, 0)),
                  pl.BlockSpec((S5_NT, D), lambda t, n: (n, t))],
        out_specs=pl.BlockSpec((S5_G, None, S5_CG, S5_NT), lambda t, n: (0, t, 0, n)),
        out_shape=jax.ShapeDtypeStruct((S5_G, S5_LC, S5_CG, S5_N), BF16),
        compiler_params=_params("parallel", "parallel"),
        name="s5_uproj",
    )(w_u_t, h2)


def _s5_chunk_rows(kind, k):
    if kind == "c":
        return pl.ds(B * S5_NLAT + k, B, stride=S5_NCTX)
    return pl.ds(k, B, stride=S5_NLAT)


def _s5_kernel(ut_ref, tc_ref, et_ref, ft_ref, ar_ref, ai_ref, d_ref, o_ref,
               he_re_ref, he_im_ref, hpf_re_ref, hpf_im_ref, hpb_re_ref, hpb_im_ref):
    sw = 2 * S5_P
    utf = ut_ref[...].reshape(S5_CW, S5_N).astype(F32)
    un = utf.T.astype(BF16)
    y = jnp.dot(un, tc_ref[...], preferred_element_type=F32)
    he = jnp.dot(un, et_ref[...], preferred_element_type=F32)
    he_re_ref[...] = he[:, :sw]
    he_im_ref[...] = he[:, sw:]

    ar = ar_ref[...]
    ai = ai_ref[...]
    is_fwd = lax.broadcasted_iota(jnp.int32, (B, 2 * S5_P), 1) < S5_P
    h_re = jnp.zeros((B, 2 * S5_P), F32)
    h_im = jnp.zeros((B, 2 * S5_P), F32)
    fwd = [("c", k) for k in range(S5_NCTX)] + [("l", k) for k in range(S5_NLAT)]
    bwd = ([("c", k) for k in range(S5_NCTX - 1, -1, -1)]
           + [("l", k) for k in range(S5_NLAT - 1, -1, -1)])
    for cf, cb in zip(fwd, bwd):
        rf = _s5_chunk_rows(*cf)
        rb = _s5_chunk_rows(*cb)
        hpf_re_ref[rf, :] = h_re
        hpf_im_ref[rf, :] = h_im
        hpb_re_ref[rb, :] = h_re
        hpb_im_ref[rb, :] = h_im
        e_re = jnp.where(is_fwd, he_re_ref[rf, :], he_re_ref[rb, :])
        e_im = jnp.where(is_fwd, he_im_ref[rf, :], he_im_ref[rb, :])
        n_re = ar * h_re - ai * h_im + e_re
        n_im = ar * h_im + ai * h_re + e_im
        h_re, h_im = n_re, n_im

    hp = jnp.concatenate([hpf_re_ref[...], hpf_im_ref[...], hpb_re_ref[...], hpb_im_ref[...]],
                         axis=1).astype(BF16)
    y = y + jnp.dot(hp, ft_ref[...], preferred_element_type=F32)
    g = y.T + d_ref[...] * utf
    gl = 0.5 * g * (1.0 + lax.erf(g * (0.5 ** 0.5)))
    o_ref[...] = gl.astype(o_ref.dtype).reshape(S5_LC, S5_CG, S5_N)


def _s5_scan(ut, mats, d_col):
    tc, et, ft, a_re, a_im = mats
    mat = pl.BlockSpec((None, S5_CW, S5_CW), lambda g: (g, 0, 0))
    vec = pl.BlockSpec((None, 1, 2 * S5_P), lambda g: (g, 0, 0))
    io = pl.BlockSpec((None, S5_LC, S5_CG, S5_N), lambda g: (g, 0, 0, 0))
    return pl.pallas_call(
        _s5_kernel,
        grid=(S5_G,),
        in_specs=[io, mat, mat,
                  pl.BlockSpec((None, 2 * S5_CW, S5_CW), lambda g: (g, 0, 0)), vec, vec,
                  pl.BlockSpec((None, S5_CW, 1), lambda g: (g, 0, 0))],
        out_specs=io,
        out_shape=jax.ShapeDtypeStruct((S5_G, S5_LC, S5_CG, S5_N), BF16),
        scratch_shapes=[pltpu.VMEM((S5_N, 2 * S5_P), F32)] * 6,
        compiler_params=_params("parallel"),
        name="s5_scan",
    )(ut, tc, et, ft, a_re, a_im, d_col)


def _glu_kernel(gl_ref, w_ref, b_ref, o_ref):
    gl = gl_ref[...].reshape(S5_W, S5_NT)
    z = jnp.dot(w_ref[...], gl, preferred_element_type=F32) + b_ref[...]
    s = gl.astype(F32) * jax.nn.sigmoid(z)
    o_ref[...] = s.T.astype(o_ref.dtype)


def _s5_glu(glt, w_t, b_col):
    out = pl.pallas_call(
        _glu_kernel,
        grid=(S5_LC, S5_N // S5_NT),
        in_specs=[pl.BlockSpec((S5_G, None, S5_CG, S5_NT), lambda t, n: (0, t, 0, n)),
                  pl.BlockSpec((S5_W, S5_W), lambda t, n: (0, 0)),
                  pl.BlockSpec((S5_W, 1), lambda t, n: (0, 0))],
        out_specs=pl.BlockSpec((S5_NT, S5_W), lambda t, n: (n, t)),
        out_shape=jax.ShapeDtypeStruct((S5_N, S5_LC * S5_W), BF16),
        compiler_params=_params("parallel", "parallel"),
        name="s5_glu",
    )(glt, w_t, b_col)
    return out.reshape(NT, S5_W)


def _rope_tables():
    t = np.arange(S)
    half = GLA_DK // 2
    freqs = ROPE_BASE ** (-np.arange(0, half, 2, dtype=np.float32) / half)
    cos_l, sin_l = [], []
    for pos in ((t // GRID_W).astype(np.float32), (t % GRID_W).astype(np.float32)):
        ang = pos[:, None] * freqs[None, :]
        c, s = np.cos(ang), np.sin(ang)
        cos_l.append(np.concatenate([c, c], axis=-1))
        sin_l.append(np.concatenate([-s, s], axis=-1))
    return (jnp.asarray(np.concatenate(cos_l, axis=-1), F32),
            jnp.asarray(np.concatenate(sin_l, axis=-1), F32))


def _rope(x, cos, sin):
    halves = [pltpu.roll(x[:, j * 128:(j + 1) * 128], 64, axis=1) for j in range(GLA_DK // 128)]
    return x * cos + jnp.concatenate(halves, axis=-1) * sin


def _split3(x):
    hi = x.astype(BF16)
    r = x - hi.astype(F32)
    mid = r.astype(BF16)
    lo = (r - mid.astype(F32)).astype(BF16)
    return hi, mid, lo


def _gla_chunk(q, k, v, a, wa, ba, st_ref, tri, keep, last_row, want_out):
    la = jax.nn.log_sigmoid(jnp.dot(a, wa, preferred_element_type=F32) + ba) / GLA_TAU
    bc = sum(jnp.dot(tri, part, preferred_element_type=F32) for part in _split3(la))
    b_last = bc[last_row:last_row + 1, :]
    q_in = q * jnp.exp(bc)
    k_end = k * jnp.exp(b_last - bc)
    st = st_ref[...]
    o = None
    if want_out:
        k_in = k * jnp.exp(-bc)
        att = lax.dot_general(q_in.astype(BF16), k_in.astype(BF16), _NT_DIMS,
                              preferred_element_type=F32)
        att = jnp.where(keep, att, 0.0)
        o = (jnp.dot(att.astype(BF16), v, preferred_element_type=F32)
             + lax.dot_general(q_in.astype(BF16), st.astype(BF16), _NT_DIMS,
                               preferred_element_type=F32))
    vt = v.astype(F32).T.astype(BF16)
    st_ref[...] = jnp.exp(b_last) * st + jnp.dot(vt, k_end.astype(BF16),
                                                  preferred_element_type=F32)
    return o


def _gla_kernel(ql_ref, kl_ref, vl_ref, gl_ref, qc_ref, kc_ref, vc_ref, al_ref, ac_ref,
                waf_ref, wab_ref, baf_ref, bab_ref, cos_ref, sin_ref, ng_ref, o_ref,
                st_ref, acc_ref):
    ii = lax.broadcasted_iota(jnp.int32, (GLA_C, GLA_C), 0)
    jj = lax.broadcasted_iota(jnp.int32, (GLA_C, GLA_C), 1)
    lower = ii >= jj
    upper = ii <= jj
    tri_f = jnp.where(lower, 1.0, 0.0).astype(BF16)
    tri_b = jnp.where(upper, 1.0, 0.0).astype(BF16)
    qscale = GLA_DK ** -0.5

    def rows(c):
        return pl.ds(pl.multiple_of(c * GLA_C, GLA_C), GLA_C)

    def ctx_step(c, wa_ref, ba_ref, tri, keep, last_row):
        r = rows(c)
        q = qc_ref[r, :].astype(F32) * qscale
        k = kc_ref[r, :].astype(F32)
        _gla_chunk(q, k, vc_ref[r, :], ac_ref[r, :], wa_ref[...], ba_ref[...], st_ref,
                   tri, keep, last_row, False)

    def lat_step(c, wa_ref, ba_ref, tri, keep, last_row, first):
        r = rows(c)
        cos = cos_ref[r, :]
        sin = sin_ref[r, :]
        q = _rope(ql_ref[r, :].astype(F32) * qscale, cos, sin)
        k = _rope(kl_ref[r, :].astype(F32), cos, sin)
        o = _gla_chunk(q, k, vl_ref[r, :], al_ref[r, :], wa_ref[...], ba_ref[...], st_ref,
                       tri, keep, last_row, True)
        if first:
            acc_ref[r, :] = o
        else:
            acc_ref[r, :] += o

    n_ctx = L // GLA_C
    n_lat = S // GLA_C

    st_ref[...] = jnp.zeros_like(st_ref)

    def f_ctx(c, carry):
        ctx_step(c, waf_ref, baf_ref, tri_f, lower, GLA_C - 1)
        return carry
    lax.fori_loop(0, n_ctx, f_ctx, 0)

    def f_lat(c, carry):
        lat_step(c, waf_ref, baf_ref, tri_f, lower, GLA_C - 1, True)
        return carry
    lax.fori_loop(0, n_lat, f_lat, 0)

    st_ref[...] = jnp.zeros_like(st_ref)

    def b_ctx(c, carry):
        ctx_step(n_ctx - 1 - c, wab_ref, bab_ref, tri_b, upper, 0)
        return carry
    lax.fori_loop(0, n_ctx, b_ctx, 0)

    def b_lat(c, carry):
        lat_step(n_lat - 1 - c, wab_ref, bab_ref, tri_b, upper, 0, False)
        return carry
    lax.fori_loop(0, n_lat, b_lat, 0)

    tr = 256

    def fin(t, carry):
        r = pl.ds(pl.multiple_of(t * tr, tr), tr)
        o = acc_ref[r, :]
        ms = jnp.mean(o * o, axis=-1, keepdims=True)
        g = gl_ref[r, :].astype(F32)
        o_ref[r, :] = (o * lax.rsqrt(ms + EPS) * ng_ref[...]
                       * (g * jax.nn.sigmoid(g))).astype(o_ref.dtype)
        return carry
    lax.fori_loop(0, S // tr, fin, 0)


def _gla(qkvg, acode, wa, ba, cos, sin, norm_g):
    ctx0 = NLAT // L
    kq = GLA_QK // GLA_DK
    in_specs = [
        pl.BlockSpec((S, GLA_DK), lambda b, h: (b, h)),
        pl.BlockSpec((S, GLA_DK), lambda b, h: (b, kq + h)),
        pl.BlockSpec((S, GLA_DV), lambda b, h: (b, kq + h)),
        pl.BlockSpec((S, GLA_DV), lambda b, h: (b, 2 * kq + h)),
        pl.BlockSpec((L, GLA_DK), lambda b, h: (ctx0 + b, h)),
        pl.BlockSpec((L, GLA_DK), lambda b, h: (ctx0 + b, kq + h)),
        pl.BlockSpec((L, GLA_DV), lambda b, h: (ctx0 + b, kq + h)),
        pl.BlockSpec((S, 128), lambda b, h: (b, 0)),
        pl.BlockSpec((L, 128), lambda b, h: (ctx0 + b, 0)),
        pl.BlockSpec((128, GLA_DK), lambda b, h: (0, h)),
        pl.BlockSpec((128, GLA_DK), lambda b, h: (0, kq + h)),
        pl.BlockSpec((1, GLA_DK), lambda b, h: (0, h)),
        pl.BlockSpec((1, GLA_DK), lambda b, h: (0, kq + h)),
        pl.BlockSpec((S, GLA_DK), lambda b, h: (0, 0)),
        pl.BlockSpec((S, GLA_DK), lambda b, h: (0, 0)),
        pl.BlockSpec((1, GLA_DV), lambda b, h: (0, 0)),
    ]
    return pl.pallas_call(
        _gla_kernel,
        grid=(B, GLA_H),
        in_specs=in_specs,
        out_specs=pl.BlockSpec((S, GLA_DV), lambda b, h: (b, h)),
        out_shape=jax.ShapeDtypeStruct((NLAT, GLA_VW), BF16),
        scratch_shapes=[pltpu.VMEM((GLA_DV, GLA_DK), F32), pltpu.VMEM((S, GLA_DV), F32)],
        compiler_params=_params("parallel", "parallel"),
        name="gla",
    )(qkvg, qkvg, qkvg, qkvg, qkvg, qkvg, qkvg, acode, acode, wa, wa, ba, ba, cos, sin,
      norm_g.reshape(1, GLA_DV))


def kernel(x, c, ctx, c_ctx, ada_w, ada_b, norm1_g, norm2_g, mlp_w1, mlp_w2, final_g, ab_w_in, ab_w_out, na_rel_bias, s5_lambda_re, s5_lambda_im, s5_log_dt, s5_b_re, s5_b_im, s5_c_re, s5_c_im, s5_d, s5_glu_w, s5_glu_b, gla_w_in, gla_w_a2, gla_b_a, gla_norm_g, gla_w_out):
    xs = jnp.concatenate([x.astype(F32).reshape(NLAT, D), ctx.astype(F32).reshape(NCTX, D)], axis=0)
    cvec = jnp.zeros((16, D), F32).at[:B].set(c.astype(F32)).at[B].set(c_ctx.astype(F32))
    mods = _ada_mod(cvec, ada_w, ada_b).reshape(2, 16, 1, 6 * D)
    bf = lambda w: w.astype(BF16)

    mod = mods[0]
    h = _normmod(xs, mod, norm1_g[0], 0, 1, NT)
    w_in = ab_w_in[0]
    qkv = _mm(h, bf(w_in[:, :3 * NA_W]), NT)
    att = _na_attention(qkv, _na_tables(na_rel_bias[0]))
    ut = _s5_uproj(h, bf(w_in[:, 3 * NA_W:].T))
    mats = _s5_matrices(s5_lambda_re[0], s5_lambda_im[0], s5_log_dt[0], s5_b_re[0], s5_b_im[0],
                        s5_c_re[0], s5_c_im[0])
    d_col = jnp.tile(s5_d[0].astype(F32).reshape(S5_G, 1, S5_CG), (1, S5_LC, 1)).reshape(S5_G, S5_CW, 1)
    glt = _s5_scan(ut, mats, d_col)
    s5 = _s5_glu(glt, bf(s5_glu_w[0].T), s5_glu_b[0].astype(F32).reshape(S5_W, 1))
    w_out = bf(ab_w_out[0])
    xs = _mm_res([att, s5], [w_out[:NA_W], w_out[NA_W:]], xs, mod, 2, NT)
    h = _normmod(xs, mod, norm2_g[0], 3, 4, NT)
    hid = _mm(h, bf(mlp_w1[0]), NT, relu2=True)
    xs = _mm_res([hid], [bf(mlp_w2[0])], xs, mod, 5, NT)

    mod = mods[1]
    h = _normmod(xs, mod, norm1_g[1], 0, 1, NT)
    w_in = gla_w_in[0]
    qkvg = _mm(h, bf(w_in[:, :GLA_MAIN]), NT)
    w_code = jnp.zeros((D, 128), F32).at[:, :2 * GLA_RANK].set(w_in[:, GLA_MAIN:])
    acode = _mm(h, bf(w_code), NT)
    wa = (jnp.zeros((128, 2 * GLA_QK), F32)
          .at[:GLA_RANK, :GLA_QK].set(gla_w_a2[0, 0])
          .at[GLA_RANK:2 * GLA_RANK, GLA_QK:].set(gla_w_a2[0, 1]))
    ba = gla_b_a[0].astype(F32).reshape(1, 2 * GLA_QK)
    cos, sin = _rope_tables()
    og = _gla(qkvg, acode, bf(wa), ba, cos, sin, gla_norm_g[0].astype(F32))
    xl = _mm_res([og], [bf(gla_w_out[0])], xs, mod, 2, NLAT)
    h = _normmod(xl, mod, norm2_g[1], 3, 4, NLAT)
    hid = _mm(h, bf(mlp_w1[1]), NLAT, relu2=True)
    out = _mm_res([hid], [bf(mlp_w2[1])], xl, mod, 5, NLAT, final_g=final_g.astype(F32))
    return out.reshape(B, S, D).astype(x.dtype)
```

```python
import functools
import math

import numpy as np
import jax
import jax.numpy as jnp
from jax import lax
from jax.experimental import pallas as pl
from jax.experimental.pallas import tpu as pltpu

F32 = jnp.float32
BF16 = jnp.bfloat16

D = 2048
B = 8
S = 2048
L = 256
GRID_W = 64
ROWS = S // GRID_W
NLAT = B * S
NCTX = B * L
NT = NLAT + NCTX
MLP_H = 4 * D
EPS = 1e-6
NEG_INF = -1e30

NA_H = 8
NA_DH = 128
NA_W = NA_H * NA_DH
NA_SCALE = NA_DH ** -0.5
NA_QROWS = 4
NA_KROWS = 12
NA_QT = NA_QROWS * GRID_W
NA_KT = NA_KROWS * GRID_W

S5_W = D // 2
S5_CG = 16
S5_G = S5_W // S5_CG
S5_P = 64
S5_LC = 16
S5_CW = S5_LC * S5_CG
S5_NLAT = S // S5_LC
S5_NCTX = L // S5_LC
S5_N = B * (S5_NLAT + S5_NCTX)
S5_NT = 384

GLA_H = 4
GLA_DK = 256
GLA_DV = 512
GLA_QK = GLA_H * GLA_DK
GLA_VW = GLA_H * GLA_DV
GLA_RANK = 16
GLA_TAU = 16.0
GLA_C = 64
GLA_MAIN = 2 * GLA_QK + 2 * GLA_VW
ROPE_BASE = 10000.0

VMEM_LIMIT = 56 * 1024 * 1024

_NT_DIMS = (((1,), (1,)), ((), ()))


def _params(*sem):
    return pltpu.CompilerParams(dimension_semantics=sem, vmem_limit_bytes=VMEM_LIMIT)


def _mod_row(i, tm):
    return jnp.minimum((i * tm) // S, B)


def _ada_kernel(c_ref, w_ref, b_ref, o_ref):
    c = c_ref[...]
    s = c * jax.nn.sigmoid(c)
    o_ref[...] = jnp.dot(s.astype(BF16), w_ref[...].astype(BF16),
                         preferred_element_type=F32) + b_ref[...]


def _ada_mod(cvec, ada_w, ada_b):
    depth = ada_w.shape[0]
    tn = 512
    return pl.pallas_call(
        _ada_kernel,
        grid=(depth, 6 * D // tn),
        in_specs=[pl.BlockSpec((16, D), lambda l, j: (0, 0)),
                  pl.BlockSpec((None, D, tn), lambda l, j: (l, 0, j)),
                  pl.BlockSpec((None, 1, tn), lambda l, j: (l, 0, j))],
        out_specs=pl.BlockSpec((None, 16, tn), lambda l, j: (l, 0, j)),
        out_shape=jax.ShapeDtypeStruct((depth, 16, 6 * D), F32),
        compiler_params=_params("parallel", "parallel"),
        name="ada_mod",
    )(cvec, ada_w, ada_b.reshape(depth, 1, 6 * D))


def _stream_specs(xs, tm, two_axes):
    if not isinstance(xs, tuple):
        imap = (lambda i, k: (i, 0)) if two_axes else (lambda i: (i, 0))
        return 0, [xs], [pl.BlockSpec((tm, D), imap)]
    nl = xs[0].shape[0] // tm
    if two_axes:
        maps = [lambda i, k: (jnp.minimum(i, nl - 1), 0), lambda i, k: (jnp.maximum(i - nl, 0), 0)]
    else:
        maps = [lambda i: (jnp.minimum(i, nl - 1), 0), lambda i: (jnp.maximum(i - nl, 0), 0)]
    return nl, list(xs), [pl.BlockSpec((tm, D), m) for m in maps]


def _stream_tile(x_refs, n_lat_tiles):
    if len(x_refs) == 1:
        return x_refs[0][...]
    return jnp.where(pl.program_id(0) < n_lat_tiles, x_refs[0][...], x_refs[1][...])


def _normmod_kernel(*refs, n_x, n_lat_tiles):
    x_refs = refs[:n_x]
    sh_ref, sc_ref, g_ref, o_ref = refs[n_x:]
    x = _stream_tile(x_refs, n_lat_tiles)
    ms = jnp.mean(x * x, axis=-1, keepdims=True)
    h = x * lax.rsqrt(ms + EPS) * g_ref[...]
    o_ref[...] = (h * (1.0 + sc_ref[...]) + sh_ref[...]).astype(o_ref.dtype)


def _normmod(xs, mod, g, shift_idx, scale_idx, rows):
    tm = 512
    nl, x_args, x_specs = _stream_specs(xs, tm, False)
    return pl.pallas_call(
        functools.partial(_normmod_kernel, n_x=len(x_args), n_lat_tiles=nl),
        grid=(rows // tm,),
        in_specs=x_specs + [
            pl.BlockSpec((None, 1, D), lambda i: (_mod_row(i, tm), 0, shift_idx)),
            pl.BlockSpec((None, 1, D), lambda i: (_mod_row(i, tm), 0, scale_idx)),
            pl.BlockSpec((1, D), lambda i: (0, 0))],
        out_specs=pl.BlockSpec((tm, D), lambda i: (i, 0)),
        out_shape=jax.ShapeDtypeStruct((rows, D), BF16),
        compiler_params=_params("parallel"),
        name="normmod",
    )(*x_args, mod, mod, g.reshape(1, D))


def _mm_kernel(a_ref, w_ref, o_ref, wb_ref, *, relu2):
    @pl.when(pl.program_id(1) == 0)
    def _():
        wb_ref[...] = w_ref[...].astype(BF16)

    acc = jnp.dot(a_ref[...], wb_ref[...], preferred_element_type=F32)
    if relu2:
        acc = jnp.square(jnp.maximum(acc, 0.0))
    o_ref[...] = acc.astype(o_ref.dtype)


def _mm(a, w, layer, n, rows, *, relu2=False, tm=1024, tn=1024):
    k = a.shape[1]
    tn = min(tn, n)
    return pl.pallas_call(
        functools.partial(_mm_kernel, relu2=relu2),
        grid=(n // tn, rows // tm),
        in_specs=[pl.BlockSpec((tm, k), lambda j, i: (i, 0)),
                  pl.BlockSpec((None, k, tn), lambda j, i: (layer, 0, j))],
        out_specs=pl.BlockSpec((tm, tn), lambda j, i: (i, j)),
        out_shape=jax.ShapeDtypeStruct((rows, n), BF16),
        scratch_shapes=[pltpu.VMEM((k, tn), BF16)],
        compiler_params=_params("parallel", "arbitrary"),
        name="mm_relu2" if relu2 else "mm",
    )(a, w)


def _mm_res_kernel(*refs, n_lhs, n_res, n_lat_tiles, nk, final_norm):
    a_refs = refs[:n_lhs]
    w_refs = refs[n_lhs:2 * n_lhs]
    res_refs = refs[2 * n_lhs:2 * n_lhs + n_res]
    gate_ref = refs[2 * n_lhs + n_res]
    pos = 2 * n_lhs + n_res + 1
    fg_ref = None
    if final_norm:
        fg_ref = refs[pos]
        pos += 1
    o_ref = refs[pos]
    acc_ref = refs[pos + 1] if nk > 1 else None

    part = None
    for a_ref, w_ref in zip(a_refs, w_refs):
        d = jnp.dot(a_ref[...], w_ref[...], preferred_element_type=F32)
        part = d if part is None else part + d

    def finish(acc):
        y = _stream_tile(res_refs, n_lat_tiles) + gate_ref[...] * acc
        if final_norm:
            ms = jnp.mean(y * y, axis=-1, keepdims=True)
            y = y * lax.rsqrt(ms + EPS) * fg_ref[...]
        o_ref[...] = y

    if nk == 1:
        finish(part)
    else:
        kk = pl.program_id(1)

        @pl.when(kk == 0)
        def _():
            acc_ref[...] = part

        @pl.when(kk > 0)
        def _():
            acc_ref[...] += part

        @pl.when(kk == nk - 1)
        def _():
            finish(acc_ref[...])


def _mm_res(a_list, w_list, resid, mod, gate_idx, rows, *, final_g=None, tm=512, tk=1024):
    n_lhs = len(a_list)
    kdim = a_list[0].shape[1]
    nk = kdim // tk
    final_norm = final_g is not None
    nl, res_args, res_specs = _stream_specs(resid, tm, True)
    in_specs = ([pl.BlockSpec((tm, tk), lambda i, k: (i, k)) for _ in a_list]
                + [pl.BlockSpec((tk, D), lambda i, k: (k, 0)) for _ in w_list]
                + res_specs
                + [pl.BlockSpec((None, 1, D), lambda i, k: (_mod_row(i, tm), 0, gate_idx))])
    args = list(a_list) + list(w_list) + res_args + [mod]
    if final_norm:
        in_specs.append(pl.BlockSpec((1, D), lambda i, k: (0, 0)))
        args.append(final_g.reshape(1, D))
    return pl.pallas_call(
        functools.partial(_mm_res_kernel, n_lhs=n_lhs, n_res=len(res_args), n_lat_tiles=nl,
                          nk=nk, final_norm=final_norm),
        grid=(rows // tm, nk),
        in_specs=in_specs,
        out_specs=pl.BlockSpec((tm, D), lambda i, k: (i, 0)),
        out_shape=jax.ShapeDtypeStruct((rows, D), F32),
        scratch_shapes=[pltpu.VMEM((tm, D), F32)] if nk > 1 else [],
        compiler_params=_params("parallel", "arbitrary"),
        name="mm_res",
    )(*args)


def _na_tables(rel_bias):
    hp = lax.Precision.HIGHEST
    cq = np.arange(GRID_W)[:, None]
    ck = np.arange(GRID_W)[None, :]
    ws = np.clip(cq - 8, 0, GRID_W - 16)
    col_ok = (ck >= ws) & (ck < ws + 16)
    col_hot = ((ck - cq + 15)[..., None] == np.arange(31)) & col_ok[..., None]
    blocks = jnp.einsum('hrj,qkj->hrqk', rel_bias.astype(F32), col_hot.astype(np.float32),
                        precision=hp)
    rq = np.arange(NA_QROWS)[:, None]
    rk = np.arange(NA_KROWS)[None, :]
    row_hot, ok_l = [], []
    for r0, start in ((0, 0), (NA_QROWS, 0), (ROWS - NA_QROWS, ROWS - NA_KROWS)):
        r = r0 + rq
        krow = start + rk
        rs = np.clip(r - 4, 0, ROWS - 8)
        row_ok = (krow >= rs) & (krow < rs + 8)
        row_hot.append(((krow - r + 7)[..., None] == np.arange(15)) & row_ok[..., None])
        ok_l.append(row_ok[:, None, :, None] & col_ok[None, :, None, :])
    row_hot = np.stack(row_hot).astype(np.float32)
    neg = np.where(np.stack(ok_l), 0.0, NEG_INF).astype(np.float32)
    tab = jnp.einsum('pabr,hrqk->hpaqbk', row_hot, blocks, precision=hp) + neg[None]
    return tab.reshape(NA_H, 3, NA_QT, NA_KT)


def _na_kernel(q_ref, k0_ref, k1_ref, k2_ref, v0_ref, v1_ref, v2_ref, kc_ref, vc_ref,
               tab_ref, o_ref):
    i = pl.program_id(1)
    q = q_ref[...]
    s_c = lax.dot_general(q, kc_ref[...], _NT_DIMS, preferred_element_type=F32) * NA_SCALE
    m_c = jnp.max(s_c, axis=-1, keepdims=True)

    @pl.when(i < ROWS // NA_QROWS)
    def _():
        s_w = []
        m = m_c
        for d, k_ref in enumerate((k0_ref, k1_ref, k2_ref)):
            s = lax.dot_general(q, k_ref[...], _NT_DIMS, preferred_element_type=F32) * NA_SCALE
            s = s + tab_ref[:, d * NA_QT:(d + 1) * NA_QT]
            s_w.append(s)
            m = jnp.maximum(m, jnp.max(s, axis=-1, keepdims=True))
        p_c = jnp.exp(s_c - m)
        l = jnp.sum(p_c, axis=-1, keepdims=True)
        o = jnp.dot(p_c.astype(BF16), vc_ref[...], preferred_element_type=F32)
        for s, v_ref in zip(s_w, (v0_ref, v1_ref, v2_ref)):
            p = jnp.exp(s - m)
            l = l + jnp.sum(p, axis=-1, keepdims=True)
            o = o + jnp.dot(p.astype(BF16), v_ref[...], preferred_element_type=F32)
        o_ref[...] = (o / l).astype(o_ref.dtype)

    @pl.when(i == ROWS // NA_QROWS)
    def _():
        p_c = jnp.exp(s_c - m_c)
        l = jnp.sum(p_c, axis=-1, keepdims=True)
        o = jnp.dot(p_c.astype(BF16), vc_ref[...], preferred_element_type=F32)
        o_ref[...] = (o / l).astype(o_ref.dtype)


def _na_attention(qkv, table):
    ng = ROWS // NA_QROWS
    blk = S // NA_QT
    ctx0 = NLAT // NA_QT

    def qrow(h, i, b):
        return jnp.where(i < ng, b * blk + i, ctx0 + b)

    def krow(d):
        return lambda h, i, b: (b * blk + jnp.clip(i - 1, 0, blk - 3) + d, NA_H + h)

    def vrow(d):
        return lambda h, i, b: (b * blk + jnp.clip(i - 1, 0, blk - 3) + d, 2 * NA_H + h)

    def pat(h, i, b):
        return (h, jnp.where(i == 0, 0, jnp.where(i >= ng - 1, 2, 1)), 0, 0)

    tile = (NA_QT, NA_DH)
    in_specs = ([pl.BlockSpec(tile, lambda h, i, b: (qrow(h, i, b), h))]
                + [pl.BlockSpec(tile, krow(d)) for d in range(3)]
                + [pl.BlockSpec(tile, vrow(d)) for d in range(3)]
                + [pl.BlockSpec(tile, lambda h, i, b: (ctx0 + b, NA_H + h)),
                   pl.BlockSpec(tile, lambda h, i, b: (ctx0 + b, 2 * NA_H + h)),
                   pl.BlockSpec((None, None, NA_QT, NA_KT), pat)])
    return pl.pallas_call(
        _na_kernel,
        grid=(NA_H, ng + 1, B),
        in_specs=in_specs,
        out_specs=pl.BlockSpec(tile, lambda h, i, b: (qrow(h, i, b), h)),
        out_shape=jax.ShapeDtypeStruct((NT, NA_W), BF16),
        compiler_params=_params("parallel", "parallel", "parallel"),
        name="na_attention",
    )(*([qkv] * 9), table)


def _s5_matrices(lam_re, lam_im, log_dt, b_re, b_im, c_re, c_im):
    hp = lax.Precision.HIGHEST
    lam_re, lam_im = lam_re.astype(F32), lam_im.astype(F32)
    b_re, b_im = b_re.astype(F32), b_im.astype(F32)
    c_re, c_im = c_re.astype(F32), c_im.astype(F32)
    dt = jnp.exp(log_dt.astype(F32))[..., None]
    mag = jnp.exp(lam_re * dt)
    a_re = mag * jnp.cos(lam_im * dt)
    a_im = mag * jnp.sin(lam_im * dt)
    den = lam_re * lam_re + lam_im * lam_im
    f_re = ((a_re - 1.0) * lam_re + a_im * lam_im) / den
    f_im = (a_im * lam_re - (a_re - 1.0) * lam_im) / den
    bb_re = f_re[..., None] * b_re - f_im[..., None] * b_im
    bb_im = f_re[..., None] * b_im + f_im[..., None] * b_re

    pr, pi = [jnp.ones_like(a_re)], [jnp.zeros_like(a_im)]
    for _ in range(S5_LC):
        pr.append(pr[-1] * a_re - pi[-1] * a_im)
        pi.append(pr[-2] * a_im + pi[-1] * a_re)
    pw_re = jnp.stack(pr)
    pw_im = jnp.stack(pi)

    ab_re = pw_re[..., None] * bb_re[None] - pw_im[..., None] * bb_im[None]
    ab_im = pw_re[..., None] * bb_im[None] + pw_im[..., None] * bb_re[None]
    kern = (jnp.einsum('dgop,tdgpc->tdgoc', c_re, ab_re, precision=hp)
            - jnp.einsum('dgop,tdgpc->tdgoc', c_im, ab_im, precision=hp))

    s_i = np.arange(S5_LC)[:, None]
    t_i = np.arange(S5_LC)[None, :]
    lag_f = np.clip(t_i - s_i, 0, S5_LC - 1)
    lag_b = np.clip(s_i - t_i, 0, S5_LC - 1)
    kf = jnp.where((s_i <= t_i)[..., None, None, None], kern[lag_f, 0], 0.0)
    kb = jnp.where((s_i >= t_i)[..., None, None, None], kern[lag_b, 1], 0.0)
    tc = (kf + kb).transpose(2, 0, 4, 1, 3).reshape(S5_G, S5_CW, S5_CW)

    idx_f = np.arange(S5_LC - 1, -1, -1)
    idx_b = np.arange(S5_LC)
    def st(arr, idx, d):
        return arr[idx, d].transpose(1, 0, 3, 2).reshape(S5_G, S5_CW, S5_P)
    et = jnp.concatenate([st(ab_re, idx_f, 0), st(ab_re, idx_b, 1),
                          st(ab_im, idx_f, 0), st(ab_im, idx_b, 1)], axis=-1)

    def rd(d, powers):
        pr_ = pw_re[powers, d]
        pi_ = pw_im[powers, d]
        cr = c_re[d][None] * pr_[:, :, None, :] - c_im[d][None] * pi_[:, :, None, :]
        ci = c_re[d][None] * pi_[:, :, None, :] + c_im[d][None] * pr_[:, :, None, :]
        to = lambda z: z.transpose(1, 3, 0, 2).reshape(S5_G, S5_P, S5_CW)
        return to(cr), to(-ci)
    fr, fi = rd(0, np.arange(1, S5_LC + 1))
    br, bi = rd(1, np.arange(S5_LC, 0, -1))
    z = jnp.zeros_like(fr)
    ft = jnp.concatenate([fr, z, fi, z, z, br, z, bi], axis=1)

    a16_re = jnp.concatenate([pw_re[S5_LC, 0], pw_re[S5_LC, 1]], axis=-1)[:, None, :]
    a16_im = jnp.concatenate([pw_im[S5_LC, 0], pw_im[S5_LC, 1]], axis=-1)[:, None, :]
    return tc.astype(BF16), et.astype(BF16), ft.astype(BF16), a16_re, a16_im


def _uproj_kernel(wt_ref, h_ref, o_ref):
    acc = lax.dot_general(wt_ref[...], h_ref[...], _NT_DIMS, preferred_element_type=F32)
    o_ref[...] = acc.reshape(S5_G, S5_CG, S5_NT).astype(o_ref.dtype)


def _s5_uproj(h, w_u_t):
    h2 = h.reshape(S5_N, S5_LC * D)
    return pl.pallas_call(
        _uproj_kernel,
        grid=(S5_LC, S5_N // S5_NT),
        in_specs=[pl.BlockSpec((S5_W, D), lambda t, n: (0, 0)),
                  pl.BlockSpec((S5_NT, D), lambda t, n: (n, t))],
        out_specs=pl.BlockSpec((S5_G, None, S5_CG, S5_NT), lambda t, n: (0, t, 0, n)),
        out_shape=jax.ShapeDtypeStruct((S5_G, S5_LC, S5_CG, S5_N), BF16),
        compiler_params=_params("parallel", "parallel"),
        name="s5_uproj",
    )(w_u_t, h2)


def _s5_chunk_rows(kind, k):
    if kind == "c":
        return pl.ds(B * S5_NLAT + k, B, stride=S5_NCTX)
    return pl.ds(k, B, stride=S5_NLAT)


def _s5_kernel(ut_ref, tc_ref, et_ref, ft_ref, ar_ref, ai_ref, d_ref, o_ref,
               he_re_ref, he_im_ref, hpf_re_ref, hpf_im_ref, hpb_re_ref, hpb_im_ref):
    sw = 2 * S5_P
    utf = ut_ref[...].reshape(S5_CW, S5_N).astype(F32)
    un = utf.T.astype(BF16)
    y = jnp.dot(un, tc_ref[...], preferred_element_type=F32)
    he = jnp.dot(un, et_ref[...], preferred_element_type=F32)
    he_re_ref[...] = he[:, :sw]
    he_im_ref[...] = he[:, sw:]

    ar = ar_ref[...]
    ai = ai_ref[...]
    is_fwd = lax.broadcasted_iota(jnp.int32, (B, 2 * S5_P), 1) < S5_P
    h_re = jnp.zeros((B, 2 * S5_P), F32)
    h_im = jnp.zeros((B, 2 * S5_P), F32)
    fwd = [("c", k) for k in range(S5_NCTX)] + [("l", k) for k in range(S5_NLAT)]
    bwd = ([("c", k) for k in range(S5_NCTX - 1, -1, -1)]
           + [("l", k) for k in range(S5_NLAT - 1, -1, -1)])
    for cf, cb in zip(fwd, bwd):
        rf = _s5_chunk_rows(*cf)
        rb = _s5_chunk_rows(*cb)
        hpf_re_ref[rf, :] = h_re
        hpf_im_ref[rf, :] = h_im
        hpb_re_ref[rb, :] = h_re
        hpb_im_ref[rb, :] = h_im
        e_re = jnp.where(is_fwd, he_re_ref[rf, :], he_re_ref[rb, :])
        e_im = jnp.where(is_fwd, he_im_ref[rf, :], he_im_ref[rb, :])
        n_re = ar * h_re - ai * h_im + e_re
        n_im = ar * h_im + ai * h_re + e_im
        h_re, h_im = n_re, n_im

    hp = jnp.concatenate([hpf_re_ref[...], hpf_im_ref[...], hpb_re_ref[...], hpb_im_ref[...]],
                         axis=1).astype(BF16)
    y = y + jnp.dot(hp, ft_ref[...], preferred_element_type=F32)
    g = y.T + d_ref[...] * utf
    gl = 0.5 * g * (1.0 + lax.erf(g * (0.5 ** 0.5)))
    o_ref[...] = gl.astype(o_ref.dtype).reshape(S5_LC, S5_CG, S5_N)


def _s5_scan(ut, mats, d_col):
    tc, et, ft, a_re, a_im = mats
    mat = pl.BlockSpec((None, S5_CW, S5_CW), lambda g: (g, 0, 0))
    vec = pl.BlockSpec((None, 1, 2 * S5_P), lambda g: (g, 0, 0))
    io = pl.BlockSpec((None, S5_LC, S5_CG, S5_N), lambda g: (g, 0, 0, 0))
    return pl.pallas_call(
        _s5_kernel,
        grid=(S5_G,),
        in_specs=[io, mat, mat,
                  pl.BlockSpec((None, 2 * S5_CW, S5_CW), lambda g: (g, 0, 0)), vec, vec,
                  pl.BlockSpec((None, S5_CW, 1), lambda g: (g, 0, 0))],
        out_specs=io,
        out_shape=jax.ShapeDtypeStruct((S5_G, S5_LC, S5_CG, S5_N), BF16),
        scratch_shapes=[pltpu.VMEM((S5_N, 2 * S5_P), F32)] * 6,
        compiler_params=_params("parallel"),
        name="s5_scan",
    )(ut, tc, et, ft, a_re, a_im, d_col)


def _glu_kernel(gl_ref, w_ref, b_ref, o_ref):
    gl = gl_ref[...].reshape(S5_W, S5_NT)
    z = jnp.dot(w_ref[...], gl, preferred_element_type=F32) + b_ref[...]
    s = gl.astype(F32) * jax.nn.sigmoid(z)
    o_ref[...] = s.T.astype(o_ref.dtype)


def _s5_glu(glt, w_t, b_col):
    out = pl.pallas_call(
        _glu_kernel,
        grid=(S5_LC, S5_N // S5_NT),
        in_specs=[pl.BlockSpec((S5_G, None, S5_CG, S5_NT), lambda t, n: (0, t, 0, n)),
                  pl.BlockSpec((S5_W, S5_W), lambda t, n: (0, 0)),
                  pl.BlockSpec((S5_W, 1), lambda t, n: (0, 0))],
        out_specs=pl.BlockSpec((S5_NT, S5_W), lambda t, n: (n, t)),
        out_shape=jax.ShapeDtypeStruct((S5_N, S5_LC * S5_W), BF16),
        compiler_params=_params("parallel", "parallel"),
        name="s5_glu",
    )(glt, w_t, b_col)
    return out.reshape(NT, S5_W)


def _rope_tables():
    t = np.arange(S)
    half = GLA_DK // 2
    freqs = ROPE_BASE ** (-np.arange(0, half, 2, dtype=np.float32) / half)
    cos_l, sin_l = [], []
    for pos in ((t // GRID_W).astype(np.float32), (t % GRID_W).astype(np.float32)):
        ang = pos[:, None] * freqs[None, :]
        c, s = np.cos(ang), np.sin(ang)
        cos_l.append(np.concatenate([c, c], axis=-1))
        sin_l.append(np.concatenate([-s, s], axis=-1))
    return (jnp.asarray(np.concatenate(cos_l, axis=-1), F32),
            jnp.asarray(np.concatenate(sin_l, axis=-1), F32))


def _rope(x, cos, sin):
    halves = [pltpu.roll(x[:, j * 128:(j + 1) * 128], 64, axis=1) for j in range(GLA_DK // 128)]
    return x * cos + jnp.concatenate(halves, axis=-1) * sin


def _split3(x):
    hi = x.astype(BF16)
    r = x - hi.astype(F32)
    mid = r.astype(BF16)
    lo = (r - mid.astype(F32)).astype(BF16)
    return hi, mid, lo


def _gla_chunk(q, k, v, a, wa, ba, st_ref, tri, keep, last_row, want_out):
    la = jax.nn.log_sigmoid(jnp.dot(a, wa, preferred_element_type=F32) + ba) / GLA_TAU
    bc = sum(jnp.dot(tri, part, preferred_element_type=F32) for part in _split3(la))
    b_last = bc[last_row:last_row + 1, :]
    q_in = q * jnp.exp(bc)
    k_end = k * jnp.exp(b_last - bc)
    st = st_ref[...]
    o = None
    if want_out:
        k_in = k * jnp.exp(-bc)
        att = lax.dot_general(q_in.astype(BF16), k_in.astype(BF16), _NT_DIMS,
                              preferred_element_type=F32)
        att = jnp.where(keep, att, 0.0)
        o = (jnp.dot(att.astype(BF16), v, preferred_element_type=F32)
             + lax.dot_general(q_in.astype(BF16), st.astype(BF16), _NT_DIMS,
                               preferred_element_type=F32))
    vt = v.astype(F32).T.astype(BF16)
    st_ref[...] = jnp.exp(b_last) * st + jnp.dot(vt, k_end.astype(BF16),
                                                  preferred_element_type=F32)
    return o


def _gla_kernel(ql_ref, kl_ref, vl_ref, gl_ref, qc_ref, kc_ref, vc_ref, al_ref, ac_ref,
                waf_ref, wab_ref, baf_ref, bab_ref, cos_ref, sin_ref, ng_ref, o_ref,
                st_ref, acc_ref):
    ii = lax.broadcasted_iota(jnp.int32, (GLA_C, GLA_C), 0)
    jj = lax.broadcasted_iota(jnp.int32, (GLA_C, GLA_C), 1)
    lower = ii >= jj
    upper = ii <= jj
    tri_f = jnp.where(lower, 1.0, 0.0).astype(BF16)
    tri_b = jnp.where(upper, 1.0, 0.0).astype(BF16)
    qscale = GLA_DK ** -0.5

    def rows(c):
        return pl.ds(pl.multiple_of(c * GLA_C, GLA_C), GLA_C)

    def ctx_step(c, wa_ref, ba_ref, tri, keep, last_row):
        r = rows(c)
        q = qc_ref[r, :].astype(F32) * qscale
        k = kc_ref[r, :].astype(F32)
        _gla_chunk(q, k, vc_ref[r, :], ac_ref[r, :], wa_ref[...], ba_ref[...], st_ref,
                   tri, keep, last_row, False)

    def lat_step(c, wa_ref, ba_ref, tri, keep, last_row, first):
        r = rows(c)
        cos = cos_ref[r, :]
        sin = sin_ref[r, :]
        q = _rope(ql_ref[r, :].astype(F32) * qscale, cos, sin)
        k = _rope(kl_ref[r, :].astype(F32), cos, sin)
        o = _gla_chunk(q, k, vl_ref[r, :], al_ref[r, :], wa_ref[...], ba_ref[...], st_ref,
                       tri, keep, last_row, True)
        if first:
            acc_ref[r, :] = o
        else:
            acc_ref[r, :] += o

    n_ctx = L // GLA_C
    n_lat = S // GLA_C

    st_ref[...] = jnp.zeros_like(st_ref)

    def f_ctx(c, carry):
        ctx_step(c, waf_ref, baf_ref, tri_f, lower, GLA_C - 1)
        return carry
    lax.fori_loop(0, n_ctx, f_ctx, 0)

    def f_lat(c, carry):
        lat_step(c, waf_ref, baf_ref, tri_f, lower, GLA_C - 1, True)
        return carry
    lax.fori_loop(0, n_lat, f_lat, 0)

    st_ref[...] = jnp.zeros_like(st_ref)

    def b_ctx(c, carry):
        ctx_step(n_ctx - 1 - c, wab_ref, bab_ref, tri_b, upper, 0)
        return carry
    lax.fori_loop(0, n_ctx, b_ctx, 0)

    def b_lat(c, carry):
        lat_step(n_lat - 1 - c, wab_ref, bab_ref, tri_b, upper, 0, False)
        return carry
    lax.fori_loop(0, n_lat, b_lat, 0)

    tr = 256

    def fin(t, carry):
        r = pl.ds(pl.multiple_of(t * tr, tr), tr)
        o = acc_ref[r, :]
        ms = jnp.mean(o * o, axis=-1, keepdims=True)
        g = gl_ref[r, :].astype(F32)
        o_ref[r, :] = (o * lax.rsqrt(ms + EPS) * ng_ref[...]
                       * (g * jax.nn.sigmoid(g))).astype(o_ref.dtype)
        return carry
    lax.fori_loop(0, S // tr, fin, 0)


def _gla(qkvg, acode, wa, ba, cos, sin, norm_g):
    ctx0 = NLAT // L
    kq = GLA_QK // GLA_DK
    in_specs = [
        pl.BlockSpec((S, GLA_DK), lambda b, h: (b, h)),
        pl.BlockSpec((S, GLA_DK), lambda b, h: (b, kq + h)),
        pl.BlockSpec((S, GLA_DV), lambda b, h: (b, kq + h)),
        pl.BlockSpec((S, GLA_DV), lambda b, h: (b, 2 * kq + h)),
        pl.BlockSpec((L, GLA_DK), lambda b, h: (ctx0 + b, h)),
        pl.BlockSpec((L, GLA_DK), lambda b, h: (ctx0 + b, kq + h)),
        pl.BlockSpec((L, GLA_DV), lambda b, h: (ctx0 + b, kq + h)),
        pl.BlockSpec((S, 128), lambda b, h: (b, 0)),
        pl.BlockSpec((L, 128), lambda b, h: (ctx0 + b, 0)),
        pl.BlockSpec((128, GLA_DK), lambda b, h: (0, h)),
        pl.BlockSpec((128, GLA_DK), lambda b, h: (0, kq + h)),
        pl.BlockSpec((1, GLA_DK), lambda b, h: (0, h)),
        pl.BlockSpec((1, GLA_DK), lambda b, h: (0, kq + h)),
        pl.BlockSpec((S, GLA_DK), lambda b, h: (0, 0)),
        pl.BlockSpec((S, GLA_DK), lambda b, h: (0, 0)),
        pl.BlockSpec((1, GLA_DV), lambda b, h: (0, 0)),
    ]
    return pl.pallas_call(
        _gla_kernel,
        grid=(B, GLA_H),
        in_specs=in_specs,
        out_specs=pl.BlockSpec((S, GLA_DV), lambda b, h: (b, h)),
        out_shape=jax.ShapeDtypeStruct((NLAT, GLA_VW), BF16),
        scratch_shapes=[pltpu.VMEM((GLA_DV, GLA_DK), F32), pltpu.VMEM((S, GLA_DV), F32)],
        compiler_params=_params("parallel", "parallel"),
        name="gla",
    )(qkvg, qkvg, qkvg, qkvg, qkvg, qkvg, qkvg, acode, acode, wa, wa, ba, ba, cos, sin,
      norm_g.reshape(1, GLA_DV))


def kernel(x, c, ctx, c_ctx, ada_w, ada_b, norm1_g, norm2_g, mlp_w1, mlp_w2, final_g, ab_w_in, ab_w_out, na_rel_bias, s5_lambda_re, s5_lambda_im, s5_log_dt, s5_b_re, s5_b_im, s5_c_re, s5_c_im, s5_d, s5_glu_w, s5_glu_b, gla_w_in, gla_w_a2, gla_b_a, gla_norm_g, gla_w_out):
    xs = (x.astype(F32).reshape(NLAT, D), ctx.astype(F32).reshape(NCTX, D))
    cvec = jnp.zeros((16, D), F32).at[:B].set(c.astype(F32)).at[B].set(c_ctx.astype(F32))
    mods = _ada_mod(cvec, ada_w, ada_b).reshape(2, 16, 1, 6 * D)
    bf = lambda w: w.astype(BF16)

    mod = mods[0]
    h = _normmod(xs, mod, norm1_g[0], 0, 1, NT)
    w_in = ab_w_in[0]
    qkv = _mm(h, ab_w_in, 0, 3 * NA_W, NT)
    att = _na_attention(qkv, _na_tables(na_rel_bias[0]))
    ut = _s5_uproj(h, bf(w_in[:, 3 * NA_W:].T))
    mats = _s5_matrices(s5_lambda_re[0], s5_lambda_im[0], s5_log_dt[0], s5_b_re[0], s5_b_im[0],
                        s5_c_re[0], s5_c_im[0])
    d_col = jnp.tile(s5_d[0].astype(F32).reshape(S5_G, 1, S5_CG), (1, S5_LC, 1)).reshape(S5_G, S5_CW, 1)
    glt = _s5_scan(ut, mats, d_col)
    s5 = _s5_glu(glt, bf(s5_glu_w[0].T), s5_glu_b[0].astype(F32).reshape(S5_W, 1))
    w_out = bf(ab_w_out[0])
    xs = _mm_res([att, s5], [w_out[:NA_W], w_out[NA_W:]], xs, mod, 2, NT)
    h = _normmod(xs, mod, norm2_g[0], 3, 4, NT)
    hid = _mm(h, mlp_w1, 0, MLP_H, NT, relu2=True)
    xs = _mm_res([hid], [bf(mlp_w2[0])], xs, mod, 5, NT)

    mod = mods[1]
    h = _normmod(xs, mod, norm1_g[1], 0, 1, NT)
    w_in = gla_w_in[0]
    qkvg = _mm(h, gla_w_in, 0, GLA_MAIN, NT)
    w_code = jnp.zeros((1, D, 128), F32).at[0, :, :2 * GLA_RANK].set(w_in[:, GLA_MAIN:])
    acode = _mm(h, w_code, 0, 128, NT)
    wa = (jnp.zeros((128, 2 * GLA_QK), F32)
          .at[:GLA_RANK, :GLA_QK].set(gla_w_a2[0, 0])
          .at[GLA_RANK:2 * GLA_RANK, GLA_QK:].set(gla_w_a2[0, 1]))
    ba = gla_b_a[0].astype(F32).reshape(1, 2 * GLA_QK)
    cos, sin = _rope_tables()
    og = _gla(qkvg, acode, bf(wa), ba, cos, sin, gla_norm_g[0].astype(F32))
    xl = _mm_res([og], [bf(gla_w_out[0])], xs, mod, 2, NLAT)
    h = _normmod(xl, mod, norm2_g[1], 3, 4, NLAT)
    hid = _mm(h, mlp_w1, 1, MLP_H, NLAT, relu2=True)
    out = _mm_res([hid], [bf(mlp_w2[1])], xl, mod, 5, NLAT, final_g=final_g.astype(F32))
    return out.reshape(B, S, D).astype(x.dtype)
```

```python
import functools
import math

import numpy as np
import jax
import jax.numpy as jnp
from jax import lax
from jax.experimental import pallas as pl
from jax.experimental.pallas import tpu as pltpu

F32 = jnp.float32
BF16 = jnp.bfloat16

D = 2048
B = 8
S = 2048
L = 256
GRID_W = 64
ROWS = S // GRID_W
NLAT = B * S
NCTX = B * L
NT = NLAT + NCTX
MLP_H = 4 * D
EPS = 1e-6
NEG_INF = -1e30

NA_H = 8
NA_DH = 128
NA_W = NA_H * NA_DH
NA_SCALE = NA_DH ** -0.5
NA_HB = 2
NA_QROWS = 4
NA_KROWS = 12
NA_QT = NA_QROWS * GRID_W
NA_KT = NA_KROWS * GRID_W

S5_W = D // 2
S5_CG = 16
S5_G = S5_W // S5_CG
S5_P = 64
S5_LC = 16
S5_CW = S5_LC * S5_CG
S5_NLAT = S // S5_LC
S5_NCTX = L // S5_LC
S5_N = B * (S5_NLAT + S5_NCTX)
S5_NT = 384

GLA_H = 4
GLA_DK = 256
GLA_DV = 512
GLA_QK = GLA_H * GLA_DK
GLA_VW = GLA_H * GLA_DV
GLA_RANK = 16
GLA_TAU = 16.0
GLA_C = 64
GLA_MAIN = 2 * GLA_QK + 2 * GLA_VW
ROPE_BASE = 10000.0

VMEM_LIMIT = 56 * 1024 * 1024

_NT_DIMS = (((1,), (1,)), ((), ()))


def _params(*sem):
    return pltpu.CompilerParams(dimension_semantics=sem, vmem_limit_bytes=VMEM_LIMIT)


def _mod_row(i, tm):
    return jnp.minimum((i * tm) // S, B)


def _ada_kernel(c_ref, w_ref, b_ref, o_ref):
    c = c_ref[...]
    s = c * jax.nn.sigmoid(c)
    o_ref[...] = jnp.dot(s.astype(BF16), w_ref[...].astype(BF16),
                         preferred_element_type=F32) + b_ref[...]


def _ada_mod(cvec, ada_w, ada_b):
    depth = ada_w.shape[0]
    tn = 512
    return pl.pallas_call(
        _ada_kernel,
        grid=(depth, 6 * D // tn),
        in_specs=[pl.BlockSpec((16, D), lambda l, j: (0, 0)),
                  pl.BlockSpec((None, D, tn), lambda l, j: (l, 0, j)),
                  pl.BlockSpec((None, 1, tn), lambda l, j: (l, 0, j))],
        out_specs=pl.BlockSpec((None, 16, tn), lambda l, j: (l, 0, j)),
        out_shape=jax.ShapeDtypeStruct((depth, 16, 6 * D), F32),
        compiler_params=_params("parallel", "parallel"),
        name="ada_mod",
    )(cvec, ada_w, ada_b.reshape(depth, 1, 6 * D))


def _stream_specs(xs, tm, two_axes):
    if not isinstance(xs, tuple):
        imap = (lambda i, k: (i, 0)) if two_axes else (lambda i: (i, 0))
        return 0, [xs], [pl.BlockSpec((tm, D), imap)]
    nl = xs[0].shape[0] // tm
    if two_axes:
        maps = [lambda i, k: (jnp.minimum(i, nl - 1), 0), lambda i, k: (jnp.maximum(i - nl, 0), 0)]
    else:
        maps = [lambda i: (jnp.minimum(i, nl - 1), 0), lambda i: (jnp.maximum(i - nl, 0), 0)]
    return nl, list(xs), [pl.BlockSpec((tm, D), m) for m in maps]


def _stream_tile(x_refs, n_lat_tiles):
    if len(x_refs) == 1:
        return x_refs[0][...]
    return jnp.where(pl.program_id(0) < n_lat_tiles, x_refs[0][...], x_refs[1][...])


def _normmod_kernel(*refs, n_x, n_lat_tiles):
    x_refs = refs[:n_x]
    sh_ref, sc_ref, g_ref, o_ref = refs[n_x:]
    x = _stream_tile(x_refs, n_lat_tiles)
    ms = jnp.mean(x * x, axis=-1, keepdims=True)
    h = x * lax.rsqrt(ms + EPS) * g_ref[...]
    o_ref[...] = (h * (1.0 + sc_ref[...]) + sh_ref[...]).astype(o_ref.dtype)


def _normmod(xs, mod, g, shift_idx, scale_idx, rows):
    tm = 512
    nl, x_args, x_specs = _stream_specs(xs, tm, False)
    return pl.pallas_call(
        functools.partial(_normmod_kernel, n_x=len(x_args), n_lat_tiles=nl),
        grid=(rows // tm,),
        in_specs=x_specs + [
            pl.BlockSpec((None, 1, D), lambda i: (_mod_row(i, tm), 0, shift_idx)),
            pl.BlockSpec((None, 1, D), lambda i: (_mod_row(i, tm), 0, scale_idx)),
            pl.BlockSpec((1, D), lambda i: (0, 0))],
        out_specs=pl.BlockSpec((tm, D), lambda i: (i, 0)),
        out_shape=jax.ShapeDtypeStruct((rows, D), BF16),
        compiler_params=_params("parallel"),
        name="normmod",
    )(*x_args, mod, mod, g.reshape(1, D))


def _mm_kernel(a_ref, w_ref, o_ref, wb_ref, *, relu2):
    @pl.when(pl.program_id(1) == 0)
    def _():
        wb_ref[...] = w_ref[...].astype(BF16)

    acc = jnp.dot(a_ref[...], wb_ref[...], preferred_element_type=F32)
    if relu2:
        acc = jnp.square(jnp.maximum(acc, 0.0))
    o_ref[...] = acc.astype(o_ref.dtype)


def _mm(a, w, layer, n, rows, *, relu2=False, tm=1024, tn=1024):
    k = a.shape[1]
    tn = min(tn, n)
    return pl.pallas_call(
        functools.partial(_mm_kernel, relu2=relu2),
        grid=(n // tn, rows // tm),
        in_specs=[pl.BlockSpec((tm, k), lambda j, i: (i, 0)),
                  pl.BlockSpec((None, k, tn), lambda j, i: (layer, 0, j))],
        out_specs=pl.BlockSpec((tm, tn), lambda j, i: (i, j)),
        out_shape=jax.ShapeDtypeStruct((rows, n), BF16),
        scratch_shapes=[pltpu.VMEM((k, tn), BF16)],
        compiler_params=_params("parallel", "arbitrary"),
        name="mm_relu2" if relu2 else "mm",
    )(a, w)


def _mm_res_kernel(*refs, n_lhs, n_res, n_lat_tiles, nk, final_norm):
    a_refs = refs[:n_lhs]
    w_refs = refs[n_lhs:2 * n_lhs]
    res_refs = refs[2 * n_lhs:2 * n_lhs + n_res]
    gate_ref = refs[2 * n_lhs + n_res]
    pos = 2 * n_lhs + n_res + 1
    fg_ref = None
    if final_norm:
        fg_ref = refs[pos]
        pos += 1
    o_ref = refs[pos]

    def normed(y):
        ms = jnp.mean(y * y, axis=-1, keepdims=True)
        return y * lax.rsqrt(ms + EPS) * fg_ref[...]

    part = None
    for a_ref, w_ref in zip(a_refs, w_refs):
        d = jnp.dot(a_ref[...], w_ref[...], preferred_element_type=F32)
        part = d if part is None else part + d
    part = gate_ref[...] * part

    if nk == 1:
        y = _stream_tile(res_refs, n_lat_tiles) + part
        o_ref[...] = normed(y) if final_norm else y
        return

    kk = pl.program_id(1)

    @pl.when(kk == 0)
    def _():
        o_ref[...] = _stream_tile(res_refs, n_lat_tiles)

    o_ref[...] += part

    if final_norm:
        @pl.when(kk == nk - 1)
        def _():
            o_ref[...] = normed(o_ref[...])


def _mm_res(a_list, w_list, resid, mod, gate_idx, rows, *, final_g=None, tm=512):
    n_lhs = len(a_list)
    kdim = a_list[0].shape[1]
    tk = min(kdim, 2048)
    nk = kdim // tk
    final_norm = final_g is not None
    nl, res_args, res_specs = _stream_specs(resid, tm, True)
    in_specs = ([pl.BlockSpec((tm, tk), lambda i, k: (i, k)) for _ in a_list]
                + [pl.BlockSpec((tk, D), lambda i, k: (k, 0)) for _ in w_list]
                + res_specs
                + [pl.BlockSpec((None, 1, D), lambda i, k: (_mod_row(i, tm), 0, gate_idx))])
    args = list(a_list) + list(w_list) + res_args + [mod]
    if final_norm:
        in_specs.append(pl.BlockSpec((1, D), lambda i, k: (0, 0)))
        args.append(final_g.reshape(1, D))
    return pl.pallas_call(
        functools.partial(_mm_res_kernel, n_lhs=n_lhs, n_res=len(res_args), n_lat_tiles=nl,
                          nk=nk, final_norm=final_norm),
        grid=(rows // tm, nk),
        in_specs=in_specs,
        out_specs=pl.BlockSpec((tm, D), lambda i, k: (i, 0)),
        out_shape=jax.ShapeDtypeStruct((rows, D), F32),
        compiler_params=_params("parallel", "arbitrary"),
        name="mm_res",
    )(*args)


def _na_tables(rel_bias):
    hp = lax.Precision.HIGHEST
    cq = np.arange(GRID_W)[:, None]
    ck = np.arange(GRID_W)[None, :]
    ws = np.clip(cq - 8, 0, GRID_W - 16)
    col_ok = (ck >= ws) & (ck < ws + 16)
    col_hot = ((ck - cq + 15)[..., None] == np.arange(31)) & col_ok[..., None]
    blocks = jnp.einsum('hrj,qkj->hrqk', rel_bias.astype(F32), col_hot.astype(np.float32),
                        precision=hp)
    rq = np.arange(NA_QROWS)[:, None]
    rk = np.arange(NA_KROWS)[None, :]
    row_hot, ok_l = [], []
    for r0, start in ((0, 0), (NA_QROWS, 0), (ROWS - NA_QROWS, ROWS - NA_KROWS)):
        r = r0 + rq
        krow = start + rk
        rs = np.clip(r - 4, 0, ROWS - 8)
        row_ok = (krow >= rs) & (krow < rs + 8)
        row_hot.append(((krow - r + 7)[..., None] == np.arange(15)) & row_ok[..., None])
        ok_l.append(row_ok[:, None, :, None] & col_ok[None, :, None, :])
    row_hot = np.stack(row_hot).astype(np.float32)
    neg = np.where(np.stack(ok_l), 0.0, NEG_INF).astype(np.float32)
    tab = jnp.einsum('pabr,hrqk->hpaqbk', row_hot, blocks, precision=hp) + neg[None]
    return tab.reshape(NA_H, 3, NA_QT, NA_KT)


def _na_kernel(q_ref, k0_ref, k1_ref, k2_ref, v0_ref, v1_ref, v2_ref, kc_ref, vc_ref,
               tab_ref, o_ref):
    i = pl.program_id(1)

    def head(ref, hh):
        return ref[:, hh * NA_DH:(hh + 1) * NA_DH]

    def ctx_scores(hh):
        return lax.dot_general(head(q_ref, hh), head(kc_ref, hh), _NT_DIMS,
                               preferred_element_type=F32) * NA_SCALE

    @pl.when(i < ROWS // NA_QROWS)
    def _():
        for hh in range(NA_HB):
            q = head(q_ref, hh)
            s_c = ctx_scores(hh)
            m = jnp.max(s_c, axis=-1, keepdims=True)
            s_w = []
            for d, k_ref in enumerate((k0_ref, k1_ref, k2_ref)):
                s = lax.dot_general(q, head(k_ref, hh), _NT_DIMS,
                                    preferred_element_type=F32) * NA_SCALE
                s = s + tab_ref[hh, :, d * NA_QT:(d + 1) * NA_QT]
                s_w.append(s)
                m = jnp.maximum(m, jnp.max(s, axis=-1, keepdims=True))
            p_c = jnp.exp(s_c - m)
            l = jnp.sum(p_c, axis=-1, keepdims=True)
            o = jnp.dot(p_c.astype(BF16), head(vc_ref, hh), preferred_element_type=F32)
            for s, v_ref in zip(s_w, (v0_ref, v1_ref, v2_ref)):
                p = jnp.exp(s - m)
                l = l + jnp.sum(p, axis=-1, keepdims=True)
                o = o + jnp.dot(p.astype(BF16), head(v_ref, hh), preferred_element_type=F32)
            o_ref[:, hh * NA_DH:(hh + 1) * NA_DH] = (o / l).astype(o_ref.dtype)

    @pl.when(i == ROWS // NA_QROWS)
    def _():
        for hh in range(NA_HB):
            s_c = ctx_scores(hh)
            p_c = jnp.exp(s_c - jnp.max(s_c, axis=-1, keepdims=True))
            l = jnp.sum(p_c, axis=-1, keepdims=True)
            o = jnp.dot(p_c.astype(BF16), head(vc_ref, hh), preferred_element_type=F32)
            o_ref[:, hh * NA_DH:(hh + 1) * NA_DH] = (o / l).astype(o_ref.dtype)


def _na_attention(qkv, table):
    ng = ROWS // NA_QROWS
    blk = S // NA_QT
    ctx0 = NLAT // NA_QT

    def qrow(h, i, b):
        return jnp.where(i < ng, b * blk + i, ctx0 + b)

    nhb = NA_H // NA_HB

    def krow(d):
        return lambda h, i, b: (b * blk + jnp.clip(i - 1, 0, blk - 3) + d, nhb + h)

    def vrow(d):
        return lambda h, i, b: (b * blk + jnp.clip(i - 1, 0, blk - 3) + d, 2 * nhb + h)

    def pat(h, i, b):
        return (h, jnp.where(i == 0, 0, jnp.where(i >= ng - 1, 2, 1)), 0, 0)

    tile = (NA_QT, NA_HB * NA_DH)
    in_specs = ([pl.BlockSpec(tile, lambda h, i, b: (qrow(h, i, b), h))]
                + [pl.BlockSpec(tile, krow(d)) for d in range(3)]
                + [pl.BlockSpec(tile, vrow(d)) for d in range(3)]
                + [pl.BlockSpec(tile, lambda h, i, b: (ctx0 + b, nhb + h)),
                   pl.BlockSpec(tile, lambda h, i, b: (ctx0 + b, 2 * nhb + h)),
                   pl.BlockSpec((NA_HB, None, NA_QT, NA_KT), pat)])
    return pl.pallas_call(
        _na_kernel,
        grid=(nhb, ng + 1, B),
        in_specs=in_specs,
        out_specs=pl.BlockSpec(tile, lambda h, i, b: (qrow(h, i, b), h)),
        out_shape=jax.ShapeDtypeStruct((NT, NA_W), BF16),
        compiler_params=_params("parallel", "parallel", "parallel"),
        name="na_attention",
    )(*([qkv] * 9), table)


def _s5_matrices(lam_re, lam_im, log_dt, b_re, b_im, c_re, c_im):
    hp = lax.Precision.HIGHEST
    lam_re, lam_im = lam_re.astype(F32), lam_im.astype(F32)
    b_re, b_im = b_re.astype(F32), b_im.astype(F32)
    c_re, c_im = c_re.astype(F32), c_im.astype(F32)
    dt = jnp.exp(log_dt.astype(F32))[..., None]
    mag = jnp.exp(lam_re * dt)
    a_re = mag * jnp.cos(lam_im * dt)
    a_im = mag * jnp.sin(lam_im * dt)
    den = lam_re * lam_re + lam_im * lam_im
    f_re = ((a_re - 1.0) * lam_re + a_im * lam_im) / den
    f_im = (a_im * lam_re - (a_re - 1.0) * lam_im) / den
    bb_re = f_re[..., None] * b_re - f_im[..., None] * b_im
    bb_im = f_re[..., None] * b_im + f_im[..., None] * b_re

    pr, pi = [jnp.ones_like(a_re)], [jnp.zeros_like(a_im)]
    for _ in range(S5_LC):
        pr.append(pr[-1] * a_re - pi[-1] * a_im)
        pi.append(pr[-2] * a_im + pi[-1] * a_re)
    pw_re = jnp.stack(pr)
    pw_im = jnp.stack(pi)

    ab_re = pw_re[..., None] * bb_re[None] - pw_im[..., None] * bb_im[None]
    ab_im = pw_re[..., None] * bb_im[None] + pw_im[..., None] * bb_re[None]
    kern = (jnp.einsum('dgop,tdgpc->tdgoc', c_re, ab_re, precision=hp)
            - jnp.einsum('dgop,tdgpc->tdgoc', c_im, ab_im, precision=hp))

    s_i = np.arange(S5_LC)[:, None]
    t_i = np.arange(S5_LC)[None, :]
    lag_f = np.clip(t_i - s_i, 0, S5_LC - 1)
    lag_b = np.clip(s_i - t_i, 0, S5_LC - 1)
    kf = jnp.where((s_i <= t_i)[..., None, None, None], kern[lag_f, 0], 0.0)
    kb = jnp.where((s_i >= t_i)[..., None, None, None], kern[lag_b, 1], 0.0)
    tc = (kf + kb).transpose(2, 0, 4, 1, 3).reshape(S5_G, S5_CW, S5_CW)

    idx_f = np.arange(S5_LC - 1, -1, -1)
    idx_b = np.arange(S5_LC)
    def st(arr, idx, d):
        return arr[idx, d].transpose(1, 0, 3, 2).reshape(S5_G, S5_CW, S5_P)
    et = jnp.concatenate([st(ab_re, idx_f, 0), st(ab_re, idx_b, 1),
                          st(ab_im, idx_f, 0), st(ab_im, idx_b, 1)], axis=-1)

    def rd(d, powers):
        pr_ = pw_re[powers, d]
        pi_ = pw_im[powers, d]
        cr = c_re[d][None] * pr_[:, :, None, :] - c_im[d][None] * pi_[:, :, None, :]
        ci = c_re[d][None] * pi_[:, :, None, :] + c_im[d][None] * pr_[:, :, None, :]
        to = lambda z: z.transpose(1, 3, 0, 2).reshape(S5_G, S5_P, S5_CW)
        return to(cr), to(-ci)
    fr, fi = rd(0, np.arange(1, S5_LC + 1))
    br, bi = rd(1, np.arange(S5_LC, 0, -1))
    z = jnp.zeros_like(fr)
    ft = jnp.concatenate([fr, z, fi, z, z, br, z, bi], axis=1)

    a16_re = jnp.concatenate([pw_re[S5_LC, 0], pw_re[S5_LC, 1]], axis=-1)[:, None, :]
    a16_im = jnp.concatenate([pw_im[S5_LC, 0], pw_im[S5_LC, 1]], axis=-1)[:, None, :]
    return tc.astype(BF16), et.astype(BF16), ft.astype(BF16), a16_re, a16_im


def _uproj_kernel(wt_ref, h_ref, o_ref):
    acc = lax.dot_general(wt_ref[...], h_ref[...], _NT_DIMS, preferred_element_type=F32)
    o_ref[...] = acc.reshape(S5_G, S5_CG, S5_NT).astype(o_ref.dtype)


def _s5_uproj(h, w_u_t):
    h2 = h.reshape(S5_N, S5_LC * D)
    return pl.pallas_call(
        _uproj_kernel,
        grid=(S5_LC, S5_N // S5_NT),
        in_specs=[pl.BlockSpec((S5_W, D), lambda t, n: (0, 0)),
                  pl.BlockSpec((S5_NT, D), lambda t, n: (n, t))],
        out_specs=pl.BlockSpec((S5_G, None, S5_CG, S5_NT), lambda t, n: (0, t, 0, n)),
        out_shape=jax.ShapeDtypeStruct((S5_G, S5_LC, S5_CG, S5_N), BF16),
        compiler_params=_params("parallel", "parallel"),
        name="s5_uproj",
    )(w_u_t, h2)


def _s5_chunk_rows(kind, k):
    if kind == "c":
        return pl.ds(B * S5_NLAT + k, B, stride=S5_NCTX)
    return pl.ds(k, B, stride=S5_NLAT)


def _s5_kernel(ut_ref, tc_ref, et_ref, ft_ref, ar_ref, ai_ref, d_ref, o_ref,
               he_re_ref, he_im_ref, hpf_re_ref, hpf_im_ref, hpb_re_ref, hpb_im_ref):
    sw = 2 * S5_P
    utf = ut_ref[...].reshape(S5_CW, S5_N).astype(F32)
    un = utf.T.astype(BF16)
    y = jnp.dot(un, tc_ref[...], preferred_element_type=F32)
    he = jnp.dot(un, et_ref[...], preferred_element_type=F32)
    he_re_ref[...] = he[:, :sw]
    he_im_ref[...] = he[:, sw:]

    ar = ar_ref[...]
    ai = ai_ref[...]
    is_fwd = lax.broadcasted_iota(jnp.int32, (B, 2 * S5_P), 1) < S5_P
    h_re = jnp.zeros((B, 2 * S5_P), F32)
    h_im = jnp.zeros((B, 2 * S5_P), F32)
    fwd = [("c", k) for k in range(S5_NCTX)] + [("l", k) for k in range(S5_NLAT)]
    bwd = ([("c", k) for k in range(S5_NCTX - 1, -1, -1)]
           + [("l", k) for k in range(S5_NLAT - 1, -1, -1)])
    for cf, cb in zip(fwd, bwd):
        rf = _s5_chunk_rows(*cf)
        rb = _s5_chunk_rows(*cb)
        hpf_re_ref[rf, :] = h_re
        hpf_im_ref[rf, :] = h_im
        hpb_re_ref[rb, :] = h_re
        hpb_im_ref[rb, :] = h_im
        e_re = jnp.where(is_fwd, he_re_ref[rf, :], he_re_ref[rb, :])
        e_im = jnp.where(is_fwd, he_im_ref[rf, :], he_im_ref[rb, :])
        n_re = ar * h_re - ai * h_im + e_re
        n_im = ar * h_im + ai * h_re + e_im
        h_re, h_im = n_re, n_im

    hp = jnp.concatenate([hpf_re_ref[...], hpf_im_ref[...], hpb_re_ref[...], hpb_im_ref[...]],
                         axis=1).astype(BF16)
    y = y + jnp.dot(hp, ft_ref[...], preferred_element_type=F32)
    g = y.T + d_ref[...] * utf
    gl = 0.5 * g * (1.0 + lax.erf(g * (0.5 ** 0.5)))
    o_ref[...] = gl.astype(o_ref.dtype).reshape(S5_LC, S5_CG, S5_N)


def _s5_scan(ut, mats, d_col):
    tc, et, ft, a_re, a_im = mats
    mat = pl.BlockSpec((None, S5_CW, S5_CW), lambda g: (g, 0, 0))
    vec = pl.BlockSpec((None, 1, 2 * S5_P), lambda g: (g, 0, 0))
    io = pl.BlockSpec((None, S5_LC, S5_CG, S5_N), lambda g: (g, 0, 0, 0))
    return pl.pallas_call(
        _s5_kernel,
        grid=(S5_G,),
        in_specs=[io, mat, mat,
                  pl.BlockSpec((None, 2 * S5_CW, S5_CW), lambda g: (g, 0, 0)), vec, vec,
                  pl.BlockSpec((None, S5_CW, 1), lambda g: (g, 0, 0))],
        out_specs=io,
        out_shape=jax.ShapeDtypeStruct((S5_G, S5_LC, S5_CG, S5_N), BF16),
        scratch_shapes=[pltpu.VMEM((S5_N, 2 * S5_P), F32)] * 6,
        compiler_params=_params("parallel"),
        name="s5_scan",
    )(ut, tc, et, ft, a_re, a_im, d_col)


def _glu_kernel(gl_ref, w_ref, b_ref, o_ref):
    gl = gl_ref[...].reshape(S5_W, S5_NT)
    z = jnp.dot(w_ref[...], gl, preferred_element_type=F32) + b_ref[...]
    s = gl.astype(F32) * jax.nn.sigmoid(z)
    o_ref[...] = s.T.astype(o_ref.dtype)


def _s5_glu(glt, w_t, b_col):
    out = pl.pallas_call(
        _glu_kernel,
        grid=(S5_LC, S5_N // S5_NT),
        in_specs=[pl.BlockSpec((S5_G, None, S5_CG, S5_NT), lambda t, n: (0, t, 0, n)),
                  pl.BlockSpec((S5_W, S5_W), lambda t, n: (0, 0)),
                  pl.BlockSpec((S5_W, 1), lambda t, n: (0, 0))],
        out_specs=pl.BlockSpec((S5_NT, S5_W), lambda t, n: (n, t)),
        out_shape=jax.ShapeDtypeStruct((S5_N, S5_LC * S5_W), BF16),
        compiler_params=_params("parallel", "parallel"),
        name="s5_glu",
    )(glt, w_t, b_col)
    return out.reshape(NT, S5_W)


def _rope_tables():
    half = GLA_DK // 2
    freqs = ROPE_BASE ** (-np.arange(0, half, 2, dtype=np.float32) / half)
    out = []
    for n in (ROWS, GRID_W):
        ang = np.arange(n, dtype=np.float32)[:, None] * freqs[None, :]
        c, s = np.cos(ang), np.sin(ang)
        out += [np.concatenate([c, c], axis=-1), np.concatenate([-s, s], axis=-1)]
    return tuple(jnp.asarray(t, F32) for t in out)


def _rope(x, rcos, rsin, ccos, csin):
    x0 = x[:, :128]
    x1 = x[:, 128:]
    return jnp.concatenate([x0 * rcos + pltpu.roll(x0, 64, axis=1) * rsin,
                            x1 * ccos + pltpu.roll(x1, 64, axis=1) * csin], axis=-1)


def _cumsum_rows(x, reverse):
    row = lax.broadcasted_iota(jnp.int32, x.shape, 0)
    s = 1
    while s < GLA_C:
        if reverse:
            x = x + jnp.where(row < GLA_C - s, pltpu.roll(x, GLA_C - s, axis=0), 0.0)
        else:
            x = x + jnp.where(row >= s, pltpu.roll(x, s, axis=0), 0.0)
        s *= 2
    return x


def _chunk_rows(c):
    return pl.ds(pl.multiple_of(c * GLA_C, GLA_C), GLA_C)


def _gla_kernel(ql_ref, kl_ref, vl_ref, gl_ref, qc_ref, kc_ref, vc_ref, al_ref, ac_ref,
                waf_ref, wab_ref, baf_ref, bab_ref, rcos_ref, rsin_ref, ccos_ref, csin_ref,
                ng_ref, o_ref,
                qi_f, ki_f, ke_f, dec_f, st_f, qi_b, ki_b, ke_b, dec_b, st_b, acc_ref):
    n_ctx = L // GLA_C
    n_lat = S // GLA_C
    qscale = GLA_DK ** -0.5
    fwd = (waf_ref, baf_ref, qi_f, ki_f, ke_f, dec_f, False)
    bwd = (wab_ref, bab_ref, qi_b, ki_b, ke_b, dec_b, True)

    def prepare(q, k, a, c):
        dst = _chunk_rows(c)
        for wa_ref, ba_ref, qi, ki, ke, dec, reverse in (fwd, bwd):
            la = jax.nn.log_sigmoid(jnp.dot(a, wa_ref[...], preferred_element_type=F32)
                                    + ba_ref[...]) / GLA_TAU
            bc = _cumsum_rows(la, reverse)
            b_last = bc[0:1, :] if reverse else bc[GLA_C - 1:GLA_C, :]
            qi[dst, :] = (q * jnp.exp(bc)).astype(BF16)
            ki[dst, :] = (k * jnp.exp(-bc)).astype(BF16)
            ke[dst, :] = (k * jnp.exp(b_last - bc)).astype(BF16)
            dec[pl.ds(c, 1), :] = jnp.exp(b_last)

    def prep_ctx(c, carry):
        r = _chunk_rows(c)
        prepare(qc_ref[r, :].astype(F32) * qscale, kc_ref[r, :].astype(F32), ac_ref[r, :], c)
        return carry
    lax.fori_loop(0, n_ctx, prep_ctx, 0)

    def prep_lat(c, carry):
        r = _chunk_rows(c)
        tabs = (rcos_ref[pl.ds(c, 1), :], rsin_ref[pl.ds(c, 1), :], ccos_ref[...], csin_ref[...])
        q = _rope(ql_ref[r, :].astype(F32) * qscale, *tabs)
        k = _rope(kl_ref[r, :].astype(F32), *tabs)
        prepare(q, k, al_ref[r, :], n_ctx + c)
        return carry
    lax.fori_loop(0, n_lat, prep_lat, 0)

    ii = lax.broadcasted_iota(jnp.int32, (GLA_C, GLA_C), 0)
    jj = lax.broadcasted_iota(jnp.int32, (GLA_C, GLA_C), 1)

    def advance(direction, st_ref, c, v, want_out):
        _, _, qi, ki, ke, dec, reverse = direction
        r = _chunk_rows(c)
        st = st_ref[...]
        o = None
        if want_out:
            q_in = qi[r, :]
            att = lax.dot_general(q_in, ki[r, :], _NT_DIMS, preferred_element_type=F32)
            att = jnp.where((ii <= jj) if reverse else (ii >= jj), att, 0.0)
            o = (jnp.dot(att.astype(BF16), v, preferred_element_type=F32)
                 + lax.dot_general(q_in, st.astype(BF16), _NT_DIMS, preferred_element_type=F32))
        vt = v.astype(F32).T.astype(BF16)
        st_ref[...] = dec[pl.ds(c, 1), :] * st + jnp.dot(vt, ke[r, :],
                                                          preferred_element_type=F32)
        return o

    st_f[...] = jnp.zeros_like(st_f)
    st_b[...] = jnp.zeros_like(st_b)

    def ctx_pair(j, carry):
        cb = n_ctx - 1 - j
        advance(fwd, st_f, j, vc_ref[_chunk_rows(j), :], False)
        advance(bwd, st_b, cb, vc_ref[_chunk_rows(cb), :], False)
        return carry
    lax.fori_loop(0, n_ctx, ctx_pair, 0)

    def lat_pair(j, accumulate):
        cb = n_lat - 1 - j
        rf = _chunk_rows(j)
        rb = _chunk_rows(cb)
        o_f = advance(fwd, st_f, n_ctx + j, vl_ref[rf, :], True)
        o_b = advance(bwd, st_b, n_ctx + cb, vl_ref[rb, :], True)
        if accumulate:
            acc_ref[rf, :] += o_f
            acc_ref[rb, :] += o_b
        else:
            acc_ref[rf, :] = o_f
            acc_ref[rb, :] = o_b

    def lat_first(j, carry):
        lat_pair(j, False)
        return carry
    lax.fori_loop(0, n_lat // 2, lat_first, 0)

    def lat_second(j, carry):
        lat_pair(j, True)
        return carry
    lax.fori_loop(n_lat // 2, n_lat, lat_second, 0)

    tr = 256

    def fin(t, carry):
        r = pl.ds(pl.multiple_of(t * tr, tr), tr)
        o = acc_ref[r, :]
        ms = jnp.mean(o * o, axis=-1, keepdims=True)
        g = gl_ref[r, :].astype(F32)
        o_ref[r, :] = (o * lax.rsqrt(ms + EPS) * ng_ref[...]
                       * (g * jax.nn.sigmoid(g))).astype(o_ref.dtype)
        return carry
    lax.fori_loop(0, S // tr, fin, 0)


def _gla(qkvg, acode, wa, ba, rope, norm_g):
    ctx0 = NLAT // L
    kq = GLA_QK // GLA_DK
    half = GLA_DK // 2
    n_chunks = (L + S) // GLA_C
    per_dir = [pltpu.VMEM((L + S, GLA_DK), BF16)] * 3 + [pltpu.VMEM((n_chunks, GLA_DK), F32),
                                                         pltpu.VMEM((GLA_DV, GLA_DK), F32)]
    in_specs = [
        pl.BlockSpec((S, GLA_DK), lambda b, h: (b, h)),
        pl.BlockSpec((S, GLA_DK), lambda b, h: (b, kq + h)),
        pl.BlockSpec((S, GLA_DV), lambda b, h: (b, kq + h)),
        pl.BlockSpec((S, GLA_DV), lambda b, h: (b, 2 * kq + h)),
        pl.BlockSpec((L, GLA_DK), lambda b, h: (ctx0 + b, h)),
        pl.BlockSpec((L, GLA_DK), lambda b, h: (ctx0 + b, kq + h)),
        pl.BlockSpec((L, GLA_DV), lambda b, h: (ctx0 + b, kq + h)),
        pl.BlockSpec((S, 128), lambda b, h: (b, 0)),
        pl.BlockSpec((L, 128), lambda b, h: (ctx0 + b, 0)),
        pl.BlockSpec((128, GLA_DK), lambda b, h: (0, h)),
        pl.BlockSpec((128, GLA_DK), lambda b, h: (0, kq + h)),
        pl.BlockSpec((1, GLA_DK), lambda b, h: (0, h)),
        pl.BlockSpec((1, GLA_DK), lambda b, h: (0, kq + h)),
        pl.BlockSpec((ROWS, half), lambda b, h: (0, 0)),
        pl.BlockSpec((ROWS, half), lambda b, h: (0, 0)),
        pl.BlockSpec((GRID_W, half), lambda b, h: (0, 0)),
        pl.BlockSpec((GRID_W, half), lambda b, h: (0, 0)),
        pl.BlockSpec((1, GLA_DV), lambda b, h: (0, 0)),
    ]
    return pl.pallas_call(
        _gla_kernel,
        grid=(B, GLA_H),
        in_specs=in_specs,
        out_specs=pl.BlockSpec((S, GLA_DV), lambda b, h: (b, h)),
        out_shape=jax.ShapeDtypeStruct((NLAT, GLA_VW), BF16),
        scratch_shapes=per_dir + per_dir + [pltpu.VMEM((S, GLA_DV), F32)],
        compiler_params=_params("parallel", "parallel"),
        name="gla",
    )(qkvg, qkvg, qkvg, qkvg, qkvg, qkvg, qkvg, acode, acode, wa, wa, ba, ba, *rope,
      norm_g.reshape(1, GLA_DV))


def kernel(x, c, ctx, c_ctx, ada_w, ada_b, norm1_g, norm2_g, mlp_w1, mlp_w2, final_g, ab_w_in, ab_w_out, na_rel_bias, s5_lambda_re, s5_lambda_im, s5_log_dt, s5_b_re, s5_b_im, s5_c_re, s5_c_im, s5_d, s5_glu_w, s5_glu_b, gla_w_in, gla_w_a2, gla_b_a, gla_norm_g, gla_w_out):
    xs = (x.astype(F32).reshape(NLAT, D), ctx.astype(F32).reshape(NCTX, D))
    cvec = jnp.zeros((16, D), F32).at[:B].set(c.astype(F32)).at[B].set(c_ctx.astype(F32))
    mods = _ada_mod(cvec, ada_w, ada_b).reshape(2, 16, 1, 6 * D)
    bf = lambda w: w.astype(BF16)

    mod = mods[0]
    h = _normmod(xs, mod, norm1_g[0], 0, 1, NT)
    w_in = ab_w_in[0]
    qkv = _mm(h, ab_w_in, 0, 3 * NA_W, NT)
    att = _na_attention(qkv, _na_tables(na_rel_bias[0]))
    ut = _s5_uproj(h, bf(w_in[:, 3 * NA_W:].T))
    mats = _s5_matrices(s5_lambda_re[0], s5_lambda_im[0], s5_log_dt[0], s5_b_re[0], s5_b_im[0],
                        s5_c_re[0], s5_c_im[0])
    d_col = jnp.tile(s5_d[0].astype(F32).reshape(S5_G, 1, S5_CG), (1, S5_LC, 1)).reshape(S5_G, S5_CW, 1)
    glt = _s5_scan(ut, mats, d_col)
    s5 = _s5_glu(glt, bf(s5_glu_w[0].T), s5_glu_b[0].astype(F32).reshape(S5_W, 1))
    w_out = bf(ab_w_out[0])
    xs = _mm_res([att, s5], [w_out[:NA_W], w_out[NA_W:]], xs, mod, 2, NT)
    h = _normmod(xs, mod, norm2_g[0], 3, 4, NT)
    hid = _mm(h, mlp_w1, 0, MLP_H, NT, relu2=True)
    xs = _mm_res([hid], [bf(mlp_w2[0])], xs, mod, 5, NT)

    mod = mods[1]
    h = _normmod(xs, mod, norm1_g[1], 0, 1, NT)
    w_in = gla_w_in[0]
    qkvg = _mm(h, gla_w_in, 0, GLA_MAIN, NT)
    w_code = jnp.zeros((1, D, 128), F32).at[0, :, :2 * GLA_RANK].set(w_in[:, GLA_MAIN:])
    acode = _mm(h, w_code, 0, 128, NT)
    wa = (jnp.zeros((128, 2 * GLA_QK), F32)
          .at[:GLA_RANK, :GLA_QK].set(gla_w_a2[0, 0])
          .at[GLA_RANK:2 * GLA_RANK, GLA_QK:].set(gla_w_a2[0, 1]))
    ba = gla_b_a[0].astype(F32).reshape(1, 2 * GLA_QK)
    og = _gla(qkvg, acode, bf(wa), ba, _rope_tables(), gla_norm_g[0].astype(F32))
    xl = _mm_res([og], [bf(gla_w_out[0])], xs, mod, 2, NLAT)
    h = _normmod(xl, mod, norm2_g[1], 3, 4, NLAT)
    hid = _mm(h, mlp_w1, 1, MLP_H, NLAT, relu2=True)
    out = _mm_res([hid], [bf(mlp_w2[1])], xl, mod, 5, NLAT, final_g=final_g.astype(F32))
    return out.reshape(B, S, D).astype(x.dtype)
```

```python
import functools
import math

import numpy as np
import jax
import jax.numpy as jnp
from jax import lax
from jax.experimental import pallas as pl
from jax.experimental.pallas import tpu as pltpu

F32 = jnp.float32
BF16 = jnp.bfloat16

D = 2048
B = 8
S = 2048
L = 256
GRID_W = 64
ROWS = S // GRID_W
NLAT = B * S
NCTX = B * L
NT = NLAT + NCTX
MLP_H = 4 * D
EPS = 1e-6
NEG_INF = -1e30

NA_H = 8
NA_DH = 128
NA_W = NA_H * NA_DH
NA_SCALE = NA_DH ** -0.5
NA_HB = 2
NA_QROWS = 4
NA_KROWS = 12
NA_QT = NA_QROWS * GRID_W
NA_KT = NA_KROWS * GRID_W

S5_W = D // 2
S5_CG = 16
S5_G = S5_W // S5_CG
S5_P = 64
S5_LC = 16
S5_CW = S5_LC * S5_CG
S5_NLAT = S // S5_LC
S5_NCTX = L // S5_LC
S5_N = B * (S5_NLAT + S5_NCTX)
S5_NT = 384

GLA_H = 4
GLA_DK = 256
GLA_DV = 512
GLA_QK = GLA_H * GLA_DK
GLA_VW = GLA_H * GLA_DV
GLA_RANK = 16
GLA_TAU = 16.0
GLA_C = 64
GLA_MAIN = 2 * GLA_QK + 2 * GLA_VW
ROPE_BASE = 10000.0

VMEM_LIMIT = 56 * 1024 * 1024

_NT_DIMS = (((1,), (1,)), ((), ()))


def _params(*sem):
    return pltpu.CompilerParams(dimension_semantics=sem, vmem_limit_bytes=VMEM_LIMIT)


def _mod_row(i, tm):
    return jnp.minimum((i * tm) // S, B)


def _ada_kernel(c_ref, w_ref, b_ref, o_ref):
    c = c_ref[...]
    s = c * jax.nn.sigmoid(c)
    o_ref[...] = jnp.dot(s.astype(BF16), w_ref[...].astype(BF16),
                         preferred_element_type=F32) + b_ref[...]


def _ada_mod(cvec, ada_w, ada_b):
    depth = ada_w.shape[0]
    tn = 512
    return pl.pallas_call(
        _ada_kernel,
        grid=(depth, 6 * D // tn),
        in_specs=[pl.BlockSpec((16, D), lambda l, j: (0, 0)),
                  pl.BlockSpec((None, D, tn), lambda l, j: (l, 0, j)),
                  pl.BlockSpec((None, 1, tn), lambda l, j: (l, 0, j))],
        out_specs=pl.BlockSpec((None, 16, tn), lambda l, j: (l, 0, j)),
        out_shape=jax.ShapeDtypeStruct((depth, 16, 6 * D), F32),
        compiler_params=_params("parallel", "parallel"),
        name="ada_mod",
    )(cvec, ada_w, ada_b.reshape(depth, 1, 6 * D))


def _stream_specs(xs, tm, two_axes):
    if not isinstance(xs, tuple):
        imap = (lambda i, k: (i, 0)) if two_axes else (lambda i: (i, 0))
        return 0, [xs], [pl.BlockSpec((tm, D), imap)]
    nl = xs[0].shape[0] // tm
    if two_axes:
        maps = [lambda i, k: (jnp.minimum(i, nl - 1), 0), lambda i, k: (jnp.maximum(i - nl, 0), 0)]
    else:
        maps = [lambda i: (jnp.minimum(i, nl - 1), 0), lambda i: (jnp.maximum(i - nl, 0), 0)]
    return nl, list(xs), [pl.BlockSpec((tm, D), m) for m in maps]


def _stream_tile(x_refs, n_lat_tiles):
    if len(x_refs) == 1:
        return x_refs[0][...]
    return jnp.where(pl.program_id(0) < n_lat_tiles, x_refs[0][...], x_refs[1][...])


def _normmod_kernel(*refs, n_x, n_lat_tiles):
    x_refs = refs[:n_x]
    sh_ref, sc_ref, g_ref, o_ref = refs[n_x:]
    x = _stream_tile(x_refs, n_lat_tiles)
    ms = jnp.mean(x * x, axis=-1, keepdims=True)
    h = x * lax.rsqrt(ms + EPS) * g_ref[...]
    o_ref[...] = (h * (1.0 + sc_ref[...]) + sh_ref[...]).astype(o_ref.dtype)


def _normmod(xs, mod, g, shift_idx, scale_idx, rows):
    tm = 512
    nl, x_args, x_specs = _stream_specs(xs, tm, False)
    return pl.pallas_call(
        functools.partial(_normmod_kernel, n_x=len(x_args), n_lat_tiles=nl),
        grid=(rows // tm,),
        in_specs=x_specs + [
            pl.BlockSpec((None, 1, D), lambda i: (_mod_row(i, tm), 0, shift_idx)),
            pl.BlockSpec((None, 1, D), lambda i: (_mod_row(i, tm), 0, scale_idx)),
            pl.BlockSpec((1, D), lambda i: (0, 0))],
        out_specs=pl.BlockSpec((tm, D), lambda i: (i, 0)),
        out_shape=jax.ShapeDtypeStruct((rows, D), BF16),
        compiler_params=_params("parallel"),
        name="normmod",
    )(*x_args, mod, mod, g.reshape(1, D))


def _mm_kernel(a_ref, w_ref, o_ref, wb_ref, *, relu2):
    @pl.when(pl.program_id(1) == 0)
    def _():
        wb_ref[...] = w_ref[...].astype(BF16)

    acc = jnp.dot(a_ref[...], wb_ref[...], preferred_element_type=F32)
    if relu2:
        acc = jnp.square(jnp.maximum(acc, 0.0))
    o_ref[...] = acc.astype(o_ref.dtype)


def _mm(a, w, layer, n, rows, *, relu2=False, tm=1024, tn=1024):
    k = a.shape[1]
    tn = min(tn, n)
    return pl.pallas_call(
        functools.partial(_mm_kernel, relu2=relu2),
        grid=(n // tn, rows // tm),
        in_specs=[pl.BlockSpec((tm, k), lambda j, i: (i, 0)),
                  pl.BlockSpec((None, k, tn), lambda j, i: (layer, 0, j))],
        out_specs=pl.BlockSpec((tm, tn), lambda j, i: (i, j)),
        out_shape=jax.ShapeDtypeStruct((rows, n), BF16),
        scratch_shapes=[pltpu.VMEM((k, tn), BF16)],
        compiler_params=_params("parallel", "arbitrary"),
        name="mm_relu2" if relu2 else "mm",
    )(a, w)


def _mm_res_kernel(*refs, n_lhs, n_res, n_lat_tiles, nk, final_norm, next_norm):
    a_refs = refs[:n_lhs]
    w_refs = refs[n_lhs:2 * n_lhs]
    res_refs = refs[2 * n_lhs:2 * n_lhs + n_res]
    gate_ref = refs[2 * n_lhs + n_res]
    pos = 2 * n_lhs + n_res + 1
    n_extra = 1 if final_norm else (3 if next_norm else 0)
    extra = refs[pos:pos + n_extra]
    o_ref = refs[pos + n_extra]
    h_ref = refs[pos + n_extra + 1] if next_norm else None

    def rms(y, g_ref):
        ms = jnp.mean(y * y, axis=-1, keepdims=True)
        return y * lax.rsqrt(ms + EPS) * g_ref[...]

    def emit_next(y):
        sh_ref, sc_ref, g_ref = extra
        h_ref[...] = (rms(y, g_ref) * (1.0 + sc_ref[...]) + sh_ref[...]).astype(h_ref.dtype)

    part = None
    for a_ref, w_ref in zip(a_refs, w_refs):
        d = jnp.dot(a_ref[...], w_ref[...], preferred_element_type=F32)
        part = d if part is None else part + d
    part = gate_ref[...] * part

    if nk == 1:
        y = _stream_tile(res_refs, n_lat_tiles) + part
        o_ref[...] = rms(y, extra[0]) if final_norm else y
        if next_norm:
            emit_next(y)
        return

    kk = pl.program_id(1)

    @pl.when(kk == 0)
    def _():
        o_ref[...] = _stream_tile(res_refs, n_lat_tiles)

    o_ref[...] += part

    if final_norm or next_norm:
        @pl.when(kk == nk - 1)
        def _():
            if final_norm:
                o_ref[...] = rms(o_ref[...], extra[0])
            else:
                emit_next(o_ref[...])


def _mm_res(a_list, w, layer, resid, mod, gate_idx, rows, *, final_g=None, nxt=None, tm=512):
    n_lhs = len(a_list)
    kdim = a_list[0].shape[1]
    tk = min(kdim, 2048)
    nk = kdim // tk
    nl, res_args, res_specs = _stream_specs(resid, tm, True)

    def mod_spec(idx):
        return pl.BlockSpec((None, 1, D), lambda i, k: (_mod_row(i, tm), 0, idx))

    def w_spec(j):
        return pl.BlockSpec((None, tk, D), lambda i, k: (layer, j * nk + k, 0))

    row_spec = pl.BlockSpec((tm, D), lambda i, k: (i, 0))
    vec_spec = pl.BlockSpec((1, D), lambda i, k: (0, 0))
    in_specs = ([pl.BlockSpec((tm, tk), lambda i, k: (i, k)) for _ in a_list]
                + [w_spec(j) for j in range(n_lhs)] + res_specs + [mod_spec(gate_idx)])
    args = list(a_list) + [w] * n_lhs + res_args + [mod]
    out_specs, out_shape = row_spec, jax.ShapeDtypeStruct((rows, D), F32)
    if final_g is not None:
        in_specs.append(vec_spec)
        args.append(final_g.reshape(1, D))
    elif nxt is not None:
        mod_n, g_n, shift_idx, scale_idx = nxt
        in_specs += [mod_spec(shift_idx), mod_spec(scale_idx), vec_spec]
        args += [mod_n, mod_n, g_n.reshape(1, D)]
        out_specs = [row_spec, row_spec]
        out_shape = [out_shape, jax.ShapeDtypeStruct((rows, D), BF16)]
    return pl.pallas_call(
        functools.partial(_mm_res_kernel, n_lhs=n_lhs, n_res=len(res_args), n_lat_tiles=nl,
                          nk=nk, final_norm=final_g is not None, next_norm=nxt is not None),
        grid=(rows // tm, nk),
        in_specs=in_specs,
        out_specs=out_specs,
        out_shape=out_shape,
        compiler_params=_params("parallel", "arbitrary"),
        name="mm_res",
    )(*args)


def _na_tables(rel_bias):
    hp = lax.Precision.HIGHEST
    cq = np.arange(GRID_W)[:, None]
    ck = np.arange(GRID_W)[None, :]
    ws = np.clip(cq - 8, 0, GRID_W - 16)
    col_ok = (ck >= ws) & (ck < ws + 16)
    col_hot = ((ck - cq + 15)[..., None] == np.arange(31)) & col_ok[..., None]
    blocks = jnp.einsum('hrj,qkj->hrqk', rel_bias.astype(F32), col_hot.astype(np.float32),
                        precision=hp)
    blocks = blocks + np.where(col_ok, 0.0, NEG_INF).astype(np.float32)
    masked = jnp.full((NA_H, GRID_W, GRID_W), NEG_INF, F32)
    pats = []
    for r0, start in ((0, 0), (NA_QROWS, 0), (ROWS - NA_QROWS, ROWS - NA_KROWS)):
        rows = []
        for a in range(NA_QROWS):
            r = r0 + a
            rs = min(max(r - 4, 0), ROWS - 8)
            rows.append(jnp.concatenate(
                [blocks[:, krow - r + 7] if rs <= krow < rs + 8 else masked
                 for krow in range(start, start + NA_KROWS)], axis=-1))
        pats.append(jnp.concatenate(rows, axis=1))
    return jnp.stack(pats, axis=1)


def _na_kernel(q_ref, k0_ref, k1_ref, k2_ref, v0_ref, v1_ref, v2_ref, kc_ref, vc_ref,
               tab_ref, o_ref):
    i = pl.program_id(1)

    def head(ref, hh):
        return ref[:, hh * NA_DH:(hh + 1) * NA_DH]

    def ctx_scores(hh):
        return lax.dot_general(head(q_ref, hh), head(kc_ref, hh), _NT_DIMS,
                               preferred_element_type=F32) * NA_SCALE

    @pl.when(i < ROWS // NA_QROWS)
    def _():
        for hh in range(NA_HB):
            q = head(q_ref, hh)
            s_c = ctx_scores(hh)
            m = jnp.max(s_c, axis=-1, keepdims=True)
            s_w = []
            for d, k_ref in enumerate((k0_ref, k1_ref, k2_ref)):
                s = lax.dot_general(q, head(k_ref, hh), _NT_DIMS,
                                    preferred_element_type=F32) * NA_SCALE
                s = s + tab_ref[hh, :, d * NA_QT:(d + 1) * NA_QT]
                s_w.append(s)
                m = jnp.maximum(m, jnp.max(s, axis=-1, keepdims=True))
            p_c = jnp.exp(s_c - m)
            l = jnp.sum(p_c, axis=-1, keepdims=True)
            o = jnp.dot(p_c.astype(BF16), head(vc_ref, hh), preferred_element_type=F32)
            for s, v_ref in zip(s_w, (v0_ref, v1_ref, v2_ref)):
                p = jnp.exp(s - m)
                l = l + jnp.sum(p, axis=-1, keepdims=True)
                o = o + jnp.dot(p.astype(BF16), head(v_ref, hh), preferred_element_type=F32)
            o_ref[:, hh * NA_DH:(hh + 1) * NA_DH] = (o / l).astype(o_ref.dtype)

    @pl.when(i == ROWS // NA_QROWS)
    def _():
        for hh in range(NA_HB):
            s_c = ctx_scores(hh)
            p_c = jnp.exp(s_c - jnp.max(s_c, axis=-1, keepdims=True))
            l = jnp.sum(p_c, axis=-1, keepdims=True)
            o = jnp.dot(p_c.astype(BF16), head(vc_ref, hh), preferred_element_type=F32)
            o_ref[:, hh * NA_DH:(hh + 1) * NA_DH] = (o / l).astype(o_ref.dtype)


def _na_attention(qkv, table):
    ng = ROWS // NA_QROWS
    blk = S // NA_QT
    ctx0 = NLAT // NA_QT

    def qrow(h, i, b):
        return jnp.where(i < ng, b * blk + i, ctx0 + b)

    nhb = NA_H // NA_HB

    def krow(d):
        return lambda h, i, b: (b * blk + jnp.clip(i - 1, 0, blk - 3) + d, nhb + h)

    def vrow(d):
        return lambda h, i, b: (b * blk + jnp.clip(i - 1, 0, blk - 3) + d, 2 * nhb + h)

    def pat(h, i, b):
        return (h, jnp.where(i == 0, 0, jnp.where(i >= ng - 1, 2, 1)), 0, 0)

    tile = (NA_QT, NA_HB * NA_DH)
    in_specs = ([pl.BlockSpec(tile, lambda h, i, b: (qrow(h, i, b), h))]
                + [pl.BlockSpec(tile, krow(d)) for d in range(3)]
                + [pl.BlockSpec(tile, vrow(d)) for d in range(3)]
                + [pl.BlockSpec(tile, lambda h, i, b: (ctx0 + b, nhb + h)),
                   pl.BlockSpec(tile, lambda h, i, b: (ctx0 + b, 2 * nhb + h)),
                   pl.BlockSpec((NA_HB, None, NA_QT, NA_KT), pat)])
    return pl.pallas_call(
        _na_kernel,
        grid=(nhb, ng + 1, B),
        in_specs=in_specs,
        out_specs=pl.BlockSpec(tile, lambda h, i, b: (qrow(h, i, b), h)),
        out_shape=jax.ShapeDtypeStruct((NT, NA_W), BF16),
        compiler_params=_params("parallel", "parallel", "parallel"),
        name="na_attention",
    )(*([qkv] * 9), table)


def _s5_matrices(lam_re, lam_im, log_dt, b_re, b_im, c_re, c_im):
    lam_re, lam_im = lam_re.astype(F32), lam_im.astype(F32)
    b_re, b_im = b_re.astype(F32), b_im.astype(F32)
    c_re, c_im = c_re.astype(F32), c_im.astype(F32)
    dt = jnp.exp(log_dt.astype(F32))[..., None]
    mag = jnp.exp(lam_re * dt)
    a_re = mag * jnp.cos(lam_im * dt)
    a_im = mag * jnp.sin(lam_im * dt)
    den = lam_re * lam_re + lam_im * lam_im
    f_re = ((a_re - 1.0) * lam_re + a_im * lam_im) / den
    f_im = (a_im * lam_re - (a_re - 1.0) * lam_im) / den
    bb_re = f_re[..., None] * b_re - f_im[..., None] * b_im
    bb_im = f_re[..., None] * b_im + f_im[..., None] * b_re

    pr, pi = [jnp.ones_like(a_re)], [jnp.zeros_like(a_im)]
    for _ in range(S5_LC):
        pr.append(pr[-1] * a_re - pi[-1] * a_im)
        pi.append(pr[-2] * a_im + pi[-1] * a_re)
    pw_re = jnp.stack(pr)
    pw_im = jnp.stack(pi)

    ab_re = pw_re[..., None] * bb_re[None] - pw_im[..., None] * bb_im[None]
    ab_im = pw_re[..., None] * bb_im[None] + pw_im[..., None] * bb_re[None]
    def lag_minor(z, d):
        return z[:S5_LC, d].transpose(1, 2, 0, 3).reshape(S5_G, 1, S5_P, S5_CW)
    def kernels(d):
        k = jnp.sum(c_re[d][..., None] * lag_minor(ab_re, d)
                    - c_im[d][..., None] * lag_minor(ab_im, d), axis=2)
        return k.reshape(S5_G, S5_CG, S5_LC, S5_CG).transpose(0, 3, 2, 1)
    zpad = jnp.zeros((S5_G, S5_CG, S5_LC, S5_CG), F32)
    kf = jnp.concatenate([zpad, kernels(0)], axis=2)
    kb = jnp.concatenate([jnp.flip(kernels(1), axis=2), zpad], axis=2)
    tc = jnp.stack([kf[:, :, S5_LC - s:2 * S5_LC - s] + kb[:, :, S5_LC - 1 - s:2 * S5_LC - 1 - s]
                    for s in range(S5_LC)], axis=1).reshape(S5_G, S5_CW, S5_CW)

    def st(arr, d, flip):
        z = arr[:S5_LC, d]
        z = jnp.flip(z, axis=0) if flip else z
        return z.transpose(1, 0, 3, 2).reshape(S5_G, S5_CW, S5_P)
    et = jnp.concatenate([st(ab_re, 0, True), st(ab_re, 1, False),
                          st(ab_im, 0, True), st(ab_im, 1, False)], axis=-1)

    def rd(d, flip):
        pr_ = pw_re[1:, d]
        pi_ = pw_im[1:, d]
        if flip:
            pr_, pi_ = jnp.flip(pr_, axis=0), jnp.flip(pi_, axis=0)
        cr = c_re[d][None] * pr_[:, :, None, :] - c_im[d][None] * pi_[:, :, None, :]
        ci = c_re[d][None] * pi_[:, :, None, :] + c_im[d][None] * pr_[:, :, None, :]
        to = lambda z: z.transpose(1, 3, 0, 2).reshape(S5_G, S5_P, S5_CW)
        return to(cr), to(-ci)
    fr, fi = rd(0, False)
    br, bi = rd(1, True)
    z = jnp.zeros_like(fr)
    ft = jnp.concatenate([fr, z, fi, z, z, br, z, bi], axis=1)

    a16_re = jnp.concatenate([pw_re[S5_LC, 0], pw_re[S5_LC, 1]], axis=-1)[:, None, :]
    a16_im = jnp.concatenate([pw_im[S5_LC, 0], pw_im[S5_LC, 1]], axis=-1)[:, None, :]
    return tc.astype(BF16), et.astype(BF16), ft.astype(BF16), a16_re, a16_im


def _uproj_kernel(wt_ref, h_ref, o_ref):
    acc = lax.dot_general(wt_ref[...], h_ref[...], _NT_DIMS, preferred_element_type=F32)
    o_ref[...] = acc.reshape(S5_G, S5_CG, S5_NT).astype(o_ref.dtype)


def _s5_uproj(h, w_u_t):
    h2 = h.reshape(S5_N, S5_LC * D)
    return pl.pallas_call(
        _uproj_kernel,
        grid=(S5_LC, S5_N // S5_NT),
        in_specs=[pl.BlockSpec((S5_W, D), lambda t, n: (0, 0)),
                  pl.BlockSpec((S5_NT, D), lambda t, n: (n, t))],
        out_specs=pl.BlockSpec((S5_G, None, S5_CG, S5_NT), lambda t, n: (0, t, 0, n)),
        out_shape=jax.ShapeDtypeStruct((S5_G, S5_LC, S5_CG, S5_N), BF16),
        compiler_params=_params("parallel", "parallel"),
        name="s5_uproj",
    )(w_u_t, h2)


def _s5_chunk_rows(kind, k):
    if kind == "c":
        return pl.ds(B * S5_NLAT + k, B, stride=S5_NCTX)
    return pl.ds(k, B, stride=S5_NLAT)


def _s5_kernel(ut_ref, tc_ref, et_ref, ft_ref, ar_ref, ai_ref, d_ref, o_ref,
               he_re_ref, he_im_ref, hpf_re_ref, hpf_im_ref, hpb_re_ref, hpb_im_ref):
    sw = 2 * S5_P
    utf = ut_ref[...].reshape(S5_CW, S5_N).astype(F32)
    un = utf.T.astype(BF16)
    y = jnp.dot(un, tc_ref[...], preferred_element_type=F32)
    he = jnp.dot(un, et_ref[...], preferred_element_type=F32)
    he_re_ref[...] = he[:, :sw]
    he_im_ref[...] = he[:, sw:]

    ar = ar_ref[...]
    ai = ai_ref[...]
    is_fwd = lax.broadcasted_iota(jnp.int32, (B, 2 * S5_P), 1) < S5_P
    h_re = jnp.zeros((B, 2 * S5_P), F32)
    h_im = jnp.zeros((B, 2 * S5_P), F32)
    fwd = [("c", k) for k in range(S5_NCTX)] + [("l", k) for k in range(S5_NLAT)]
    bwd = ([("c", k) for k in range(S5_NCTX - 1, -1, -1)]
           + [("l", k) for k in range(S5_NLAT - 1, -1, -1)])
    for cf, cb in zip(fwd, bwd):
        rf = _s5_chunk_rows(*cf)
        rb = _s5_chunk_rows(*cb)
        hpf_re_ref[rf, :] = h_re
        hpf_im_ref[rf, :] = h_im
        hpb_re_ref[rb, :] = h_re
        hpb_im_ref[rb, :] = h_im
        e_re = jnp.where(is_fwd, he_re_ref[rf, :], he_re_ref[rb, :])
        e_im = jnp.where(is_fwd, he_im_ref[rf, :], he_im_ref[rb, :])
        n_re = ar * h_re - ai * h_im + e_re
        n_im = ar * h_im + ai * h_re + e_im
        h_re, h_im = n_re, n_im

    hp = jnp.concatenate([hpf_re_ref[...], hpf_im_ref[...], hpb_re_ref[...], hpb_im_ref[...]],
                         axis=1).astype(BF16)
    y = y + jnp.dot(hp, ft_ref[...], preferred_element_type=F32)
    g = y.T + d_ref[...] * utf
    gl = 0.5 * g * (1.0 + lax.erf(g * (0.5 ** 0.5)))
    o_ref[...] = gl.astype(o_ref.dtype).reshape(S5_LC, S5_CG, S5_N)


def _s5_scan(ut, mats, d_col):
    tc, et, ft, a_re, a_im = mats
    mat = pl.BlockSpec((None, S5_CW, S5_CW), lambda g: (g, 0, 0))
    vec = pl.BlockSpec((None, 1, 2 * S5_P), lambda g: (g, 0, 0))
    io = pl.BlockSpec((None, S5_LC, S5_CG, S5_N), lambda g: (g, 0, 0, 0))
    return pl.pallas_call(
        _s5_kernel,
        grid=(S5_G,),
        in_specs=[io, mat, mat,
                  pl.BlockSpec((None, 2 * S5_CW, S5_CW), lambda g: (g, 0, 0)), vec, vec,
                  pl.BlockSpec((None, S5_CW, 1), lambda g: (g, 0, 0))],
        out_specs=io,
        out_shape=jax.ShapeDtypeStruct((S5_G, S5_LC, S5_CG, S5_N), BF16),
        scratch_shapes=[pltpu.VMEM((S5_N, 2 * S5_P), F32)] * 6,
        compiler_params=_params("parallel"),
        name="s5_scan",
    )(ut, tc, et, ft, a_re, a_im, d_col)


def _glu_kernel(gl_ref, w_ref, b_ref, o_ref):
    gl = gl_ref[...].reshape(S5_W, S5_NT)
    z = jnp.dot(w_ref[...], gl, preferred_element_type=F32) + b_ref[...]
    s = gl.astype(F32) * jax.nn.sigmoid(z)
    o_ref[...] = s.T.astype(o_ref.dtype)


def _s5_glu(glt, w_t, b_col):
    out = pl.pallas_call(
        _glu_kernel,
        grid=(S5_LC, S5_N // S5_NT),
        in_specs=[pl.BlockSpec((S5_G, None, S5_CG, S5_NT), lambda t, n: (0, t, 0, n)),
                  pl.BlockSpec((S5_W, S5_W), lambda t, n: (0, 0)),
                  pl.BlockSpec((S5_W, 1), lambda t, n: (0, 0))],
        out_specs=pl.BlockSpec((S5_NT, S5_W), lambda t, n: (n, t)),
        out_shape=jax.ShapeDtypeStruct((S5_N, S5_LC * S5_W), BF16),
        compiler_params=_params("parallel", "parallel"),
        name="s5_glu",
    )(glt, w_t, b_col)
    return out.reshape(NT, S5_W)


def _rope_tables():
    half = GLA_DK // 2
    freqs = ROPE_BASE ** (-np.arange(0, half, 2, dtype=np.float32) / half)
    out = []
    for n in (ROWS, GRID_W):
        ang = np.arange(n, dtype=np.float32)[:, None] * freqs[None, :]
        c, s = np.cos(ang), np.sin(ang)
        out += [np.concatenate([c, c], axis=-1), np.concatenate([-s, s], axis=-1)]
    return tuple(jnp.asarray(t, F32) for t in out)


def _rope(x, rcos, rsin, ccos, csin):
    x0 = x[:, :128]
    x1 = x[:, 128:]
    return jnp.concatenate([x0 * rcos + pltpu.roll(x0, 64, axis=1) * rsin,
                            x1 * ccos + pltpu.roll(x1, 64, axis=1) * csin], axis=-1)


def _cumsum_rows(x, reverse):
    row = lax.broadcasted_iota(jnp.int32, x.shape, 0)
    s = 1
    while s < GLA_C:
        if reverse:
            x = x + jnp.where(row < GLA_C - s, pltpu.roll(x, GLA_C - s, axis=0), 0.0)
        else:
            x = x + jnp.where(row >= s, pltpu.roll(x, s, axis=0), 0.0)
        s *= 2
    return x


def _chunk_rows(c):
    if isinstance(c, int):
        return pl.ds(c * GLA_C, GLA_C)
    return pl.ds(pl.multiple_of(c * GLA_C, GLA_C), GLA_C)


def _gla_kernel(ql_ref, kl_ref, vl_ref, gl_ref, qc_ref, kc_ref, vc_ref, al_ref, ac_ref,
                waf_ref, wab_ref, baf_ref, bab_ref, rcos_ref, rsin_ref, ccos_ref, csin_ref,
                ng_ref, o_ref,
                qi_f, ki_f, ke_f, dec_f, st_f, qi_b, ki_b, ke_b, dec_b, st_b, acc_ref):
    n_ctx = L // GLA_C
    n_lat = S // GLA_C
    qscale = GLA_DK ** -0.5
    fwd = (waf_ref, baf_ref, qi_f, ki_f, ke_f, dec_f, False)
    bwd = (wab_ref, bab_ref, qi_b, ki_b, ke_b, dec_b, True)

    def prepare(direction, q, k, a, c):
        wa_ref, ba_ref, qi, ki, ke, dec, reverse = direction
        dst = _chunk_rows(c)
        la = jax.nn.log_sigmoid(jnp.dot(a, wa_ref[...], preferred_element_type=F32)
                                + ba_ref[...]) / GLA_TAU
        bc = _cumsum_rows(la, reverse)
        b_last = bc[0:1, :] if reverse else bc[GLA_C - 1:GLA_C, :]
        qi[dst, :] = (q * jnp.exp(bc)).astype(BF16)
        ki[dst, :] = (k * jnp.exp(-bc)).astype(BF16)
        ke[dst, :] = (k * jnp.exp(b_last - bc)).astype(BF16)
        dec[pl.ds(c, 1), :] = jnp.exp(b_last)

    def prepare_latent(direction, c):
        r = _chunk_rows(c)
        tabs = (rcos_ref[pl.ds(c, 1), :], rsin_ref[pl.ds(c, 1), :], ccos_ref[...], csin_ref[...])
        q = _rope(ql_ref[r, :].astype(F32) * qscale, *tabs)
        k = _rope(kl_ref[r, :].astype(F32), *tabs)
        prepare(direction, q, k, al_ref[r, :], n_ctx + c)

    def prep_ctx(c, carry):
        r = _chunk_rows(c)
        q = qc_ref[r, :].astype(F32) * qscale
        k = kc_ref[r, :].astype(F32)
        prepare(fwd, q, k, ac_ref[r, :], c)
        prepare(bwd, q, k, ac_ref[r, :], c)
        return carry
    lax.fori_loop(0, n_ctx, prep_ctx, 0)
    prepare_latent(fwd, 0)
    prepare_latent(bwd, n_lat - 1)

    ii = lax.broadcasted_iota(jnp.int32, (GLA_C, GLA_C), 0)
    jj = lax.broadcasted_iota(jnp.int32, (GLA_C, GLA_C), 1)

    def advance(direction, st_ref, c, v, want_out):
        _, _, qi, ki, ke, dec, reverse = direction
        r = _chunk_rows(c)
        st = st_ref[...]
        o = None
        if want_out:
            q_in = qi[r, :]
            att = lax.dot_general(q_in, ki[r, :], _NT_DIMS, preferred_element_type=F32)
            att = jnp.where((ii <= jj) if reverse else (ii >= jj), att, 0.0)
            o = (jnp.dot(att.astype(BF16), v, preferred_element_type=F32)
                 + lax.dot_general(q_in, st.astype(BF16), _NT_DIMS, preferred_element_type=F32))
        vt = v.astype(F32).T.astype(BF16)
        st_ref[...] = dec[pl.ds(c, 1), :] * st + jnp.dot(vt, ke[r, :],
                                                          preferred_element_type=F32)
        return o

    st_f[...] = jnp.zeros_like(st_f)
    st_b[...] = jnp.zeros_like(st_b)

    def ctx_pair(j, carry):
        cb = n_ctx - 1 - j
        advance(fwd, st_f, j, vc_ref[_chunk_rows(j), :], False)
        advance(bwd, st_b, cb, vc_ref[_chunk_rows(cb), :], False)
        return carry
    lax.fori_loop(0, n_ctx, ctx_pair, 0)

    def lat_pair(j, accumulate):
        cb = n_lat - 1 - j
        rf = _chunk_rows(j)
        rb = _chunk_rows(cb)
        o_f = advance(fwd, st_f, n_ctx + j, vl_ref[rf, :], True)
        o_b = advance(bwd, st_b, n_ctx + cb, vl_ref[rb, :], True)
        if accumulate:
            acc_ref[rf, :] += o_f
            acc_ref[rb, :] += o_b
        else:
            acc_ref[rf, :] = o_f
            acc_ref[rb, :] = o_b
        prepare_latent(fwd, jnp.minimum(j + 1, n_lat - 1))
        prepare_latent(bwd, jnp.maximum(cb - 1, 0))

    def lat_first(j, carry):
        lat_pair(j, False)
        return carry
    lax.fori_loop(0, n_lat // 2, lat_first, 0)

    def lat_second(j, carry):
        lat_pair(j, True)
        return carry
    lax.fori_loop(n_lat // 2, n_lat, lat_second, 0)

    tr = 256

    def fin(t, carry):
        r = pl.ds(pl.multiple_of(t * tr, tr), tr)
        o = acc_ref[r, :]
        ms = jnp.mean(o * o, axis=-1, keepdims=True)
        g = gl_ref[r, :].astype(F32)
        o_ref[r, :] = (o * lax.rsqrt(ms + EPS) * ng_ref[...]
                       * (g * jax.nn.sigmoid(g))).astype(o_ref.dtype)
        return carry
    lax.fori_loop(0, S // tr, fin, 0)


def _gla(qkvg, acode, wa, ba, rope, norm_g):
    ctx0 = NLAT // L
    kq = GLA_QK // GLA_DK
    half = GLA_DK // 2
    n_chunks = (L + S) // GLA_C
    per_dir = [pltpu.VMEM((L + S, GLA_DK), BF16)] * 3 + [pltpu.VMEM((n_chunks, GLA_DK), F32),
                                                         pltpu.VMEM((GLA_DV, GLA_DK), F32)]
    in_specs = [
        pl.BlockSpec((S, GLA_DK), lambda b, h: (b, h)),
        pl.BlockSpec((S, GLA_DK), lambda b, h: (b, kq + h)),
        pl.BlockSpec((S, GLA_DV), lambda b, h: (b, kq + h)),
        pl.BlockSpec((S, GLA_DV), lambda b, h: (b, 2 * kq + h)),
        pl.BlockSpec((L, GLA_DK), lambda b, h: (ctx0 + b, h)),
        pl.BlockSpec((L, GLA_DK), lambda b, h: (ctx0 + b, kq + h)),
        pl.BlockSpec((L, GLA_DV), lambda b, h: (ctx0 + b, kq + h)),
        pl.BlockSpec((S, 128), lambda b, h: (b, 0)),
        pl.BlockSpec((L, 128), lambda b, h: (ctx0 + b, 0)),
        pl.BlockSpec((128, GLA_DK), lambda b, h: (0, h)),
        pl.BlockSpec((128, GLA_DK), lambda b, h: (0, kq + h)),
        pl.BlockSpec((1, GLA_DK), lambda b, h: (0, h)),
        pl.BlockSpec((1, GLA_DK), lambda b, h: (0, kq + h)),
        pl.BlockSpec((ROWS, half), lambda b, h: (0, 0)),
        pl.BlockSpec((ROWS, half), lambda b, h: (0, 0)),
        pl.BlockSpec((GRID_W, half), lambda b, h: (0, 0)),
        pl.BlockSpec((GRID_W, half), lambda b, h: (0, 0)),
        pl.BlockSpec((1, GLA_DV), lambda b, h: (0, 0)),
    ]
    return pl.pallas_call(
        _gla_kernel,
        grid=(B, GLA_H),
        in_specs=in_specs,
        out_specs=pl.BlockSpec((S, GLA_DV), lambda b, h: (b, h)),
        out_shape=jax.ShapeDtypeStruct((NLAT, GLA_VW), BF16),
        scratch_shapes=per_dir + per_dir + [pltpu.VMEM((S, GLA_DV), F32)],
        compiler_params=_params("parallel", "parallel"),
        name="gla",
    )(qkvg, qkvg, qkvg, qkvg, qkvg, qkvg, qkvg, acode, acode, wa, wa, ba, ba, *rope,
      norm_g.reshape(1, GLA_DV))


def kernel(x, c, ctx, c_ctx, ada_w, ada_b, norm1_g, norm2_g, mlp_w1, mlp_w2, final_g, ab_w_in, ab_w_out, na_rel_bias, s5_lambda_re, s5_lambda_im, s5_log_dt, s5_b_re, s5_b_im, s5_c_re, s5_c_im, s5_d, s5_glu_w, s5_glu_b, gla_w_in, gla_w_a2, gla_b_a, gla_norm_g, gla_w_out):
    xs = (x.astype(F32).reshape(NLAT, D), ctx.astype(F32).reshape(NCTX, D))
    cvec = jnp.zeros((16, D), F32).at[:B].set(c.astype(F32)).at[B].set(c_ctx.astype(F32))
    mods = _ada_mod(cvec, ada_w, ada_b).reshape(2, 16, 1, 6 * D)
    bf = lambda w: w.astype(BF16)

    mod = mods[0]
    h = _normmod(xs, mod, norm1_g[0], 0, 1, NT)
    w_in = ab_w_in[0]
    qkv = _mm(h, ab_w_in, 0, 3 * NA_W, NT)
    att = _na_attention(qkv, _na_tables(na_rel_bias[0]))
    ut = _s5_uproj(h, bf(w_in[:, 3 * NA_W:].T))
    mats = _s5_matrices(s5_lambda_re[0], s5_lambda_im[0], s5_log_dt[0], s5_b_re[0], s5_b_im[0],
                        s5_c_re[0], s5_c_im[0])
    d_col = jnp.tile(s5_d[0].astype(F32).reshape(S5_G, 1, S5_CG), (1, S5_LC, 1)).reshape(S5_G, S5_CW, 1)
    glt = _s5_scan(ut, mats, d_col)
    s5 = _s5_glu(glt, bf(s5_glu_w[0].T), s5_glu_b[0].astype(F32).reshape(S5_W, 1))
    w2 = bf(mlp_w2)
    xs, h = _mm_res([att, s5], bf(ab_w_out), 0, xs, mod, 2, NT, nxt=(mod, norm2_g[0], 3, 4))
    hid = _mm(h, mlp_w1, 0, MLP_H, NT, relu2=True)
    xs, h = _mm_res([hid], w2, 0, xs, mod, 5, NT, nxt=(mods[1], norm1_g[1], 0, 1))

    mod = mods[1]
    w_in = gla_w_in[0]
    qkvg = _mm(h, gla_w_in, 0, GLA_MAIN, NT)
    w_code = jnp.zeros((1, D, 128), F32).at[0, :, :2 * GLA_RANK].set(w_in[:, GLA_MAIN:])
    acode = _mm(h, w_code, 0, 128, NT)
    wa = (jnp.zeros((128, 2 * GLA_QK), F32)
          .at[:GLA_RANK, :GLA_QK].set(gla_w_a2[0, 0])
          .at[GLA_RANK:2 * GLA_RANK, GLA_QK:].set(gla_w_a2[0, 1]))
    ba = gla_b_a[0].astype(F32).reshape(1, 2 * GLA_QK)
    og = _gla(qkvg, acode, bf(wa), ba, _rope_tables(), gla_norm_g[0].astype(F32))
    xl, h = _mm_res([og], bf(gla_w_out), 0, xs, mod, 2, NLAT, nxt=(mod, norm2_g[1], 3, 4))
    hid = _mm(h, mlp_w1, 1, MLP_H, NLAT, relu2=True)
    out = _mm_res([hid], w2, 1, xl, mod, 5, NLAT, final_g=final_g.astype(F32))
    return out.reshape(B, S, D).astype(x.dtype)
```

```python
import functools
import math

import numpy as np
import jax
import jax.numpy as jnp
from jax import lax
from jax.experimental import pallas as pl
from jax.experimental.pallas import tpu as pltpu

F32 = jnp.float32
BF16 = jnp.bfloat16

D = 2048
B = 8
S = 2048
L = 256
GRID_W = 64
ROWS = S // GRID_W
NLAT = B * S
NCTX = B * L
NT = NLAT + NCTX
MLP_H = 4 * D
EPS = 1e-6
NEG_INF = -1e30

NA_H = 8
NA_DH = 128
NA_W = NA_H * NA_DH
NA_SCALE = NA_DH ** -0.5
NA_HB = 4
NA_QROWS = 4
NA_KROWS = 12
NA_QT = NA_QROWS * GRID_W
NA_KT = NA_KROWS * GRID_W

S5_W = D // 2
S5_CG = 16
S5_G = S5_W // S5_CG
S5_P = 64
S5_LC = 16
S5_CW = S5_LC * S5_CG
S5_NLAT = S // S5_LC
S5_NCTX = L // S5_LC
S5_N = B * (S5_NLAT + S5_NCTX)
S5_NT = 384

GLA_H = 4
GLA_DK = 256
GLA_DV = 512
GLA_QK = GLA_H * GLA_DK
GLA_VW = GLA_H * GLA_DV
GLA_RANK = 16
GLA_TAU = 16.0
GLA_C = 64
GLA_MAIN = 2 * GLA_QK + 2 * GLA_VW
ROPE_BASE = 10000.0

MM_RES_NSPLIT = 4

VMEM_LIMIT = 56 * 1024 * 1024

_NT_DIMS = (((1,), (1,)), ((), ()))


def _params(*sem):
    return pltpu.CompilerParams(dimension_semantics=sem, vmem_limit_bytes=VMEM_LIMIT)


def _mod_row(i, tm):
    return jnp.minimum((i * tm) // S, B)


def _ada_kernel(c_ref, w_ref, b_ref, o_ref):
    c = c_ref[...]
    s = c * jax.nn.sigmoid(c)
    o_ref[...] = jnp.dot(s.astype(BF16), w_ref[...].astype(BF16),
                         preferred_element_type=F32) + b_ref[...]


def _ada_mod(cvec, ada_w, ada_b):
    depth = ada_w.shape[0]
    tn = 512
    return pl.pallas_call(
        _ada_kernel,
        grid=(depth, 6 * D // tn),
        in_specs=[pl.BlockSpec((16, D), lambda l, j: (0, 0)),
                  pl.BlockSpec((None, D, tn), lambda l, j: (l, 0, j)),
                  pl.BlockSpec((None, 1, tn), lambda l, j: (l, 0, j))],
        out_specs=pl.BlockSpec((None, 16, tn), lambda l, j: (l, 0, j)),
        out_shape=jax.ShapeDtypeStruct((depth, 16, 6 * D), F32),
        compiler_params=_params("parallel", "parallel"),
        name="ada_mod",
    )(cvec, ada_w, ada_b.reshape(depth, 1, 6 * D))


def _stream_specs(xs, tm, two_axes):
    if not isinstance(xs, tuple):
        imap = (lambda i, k: (i, 0)) if two_axes else (lambda i: (i, 0))
        return 0, [xs], [pl.BlockSpec((tm, D), imap)]
    nl = xs[0].shape[0] // tm
    if two_axes:
        maps = [lambda i, k: (jnp.minimum(i, nl - 1), 0), lambda i, k: (jnp.maximum(i - nl, 0), 0)]
    else:
        maps = [lambda i: (jnp.minimum(i, nl - 1), 0), lambda i: (jnp.maximum(i - nl, 0), 0)]
    return nl, list(xs), [pl.BlockSpec((tm, D), m) for m in maps]


def _stream_tile(x_refs, n_lat_tiles, cols=slice(None)):
    if len(x_refs) == 1:
        return x_refs[0][:, cols]
    return jnp.where(pl.program_id(0) < n_lat_tiles, x_refs[0][:, cols], x_refs[1][:, cols])


def _normmod_kernel(*refs, n_x, n_lat_tiles):
    x_refs = refs[:n_x]
    sh_ref, sc_ref, g_ref, o_ref = refs[n_x:]
    x = _stream_tile(x_refs, n_lat_tiles)
    ms = jnp.mean(x * x, axis=-1, keepdims=True)
    h = x * lax.rsqrt(ms + EPS) * g_ref[...]
    o_ref[...] = (h * (1.0 + sc_ref[...]) + sh_ref[...]).astype(o_ref.dtype)


def _normmod(xs, mod, g, shift_idx, scale_idx, rows):
    tm = 512
    nl, x_args, x_specs = _stream_specs(xs, tm, False)
    return pl.pallas_call(
        functools.partial(_normmod_kernel, n_x=len(x_args), n_lat_tiles=nl),
        grid=(rows // tm,),
        in_specs=x_specs + [
            pl.BlockSpec((None, 1, D), lambda i: (_mod_row(i, tm), 0, shift_idx)),
            pl.BlockSpec((None, 1, D), lambda i: (_mod_row(i, tm), 0, scale_idx)),
            pl.BlockSpec((1, D), lambda i: (0, 0))],
        out_specs=pl.BlockSpec((tm, D), lambda i: (i, 0)),
        out_shape=jax.ShapeDtypeStruct((rows, D), BF16),
        compiler_params=_params("parallel"),
        name="normmod",
    )(*x_args, mod, mod, g.reshape(1, D))


def _mm_kernel(a_ref, w_ref, o_ref, wb_ref, *, relu2):
    @pl.when(pl.program_id(1) == 0)
    def _():
        wb_ref[...] = w_ref[...].astype(BF16)

    acc = jnp.dot(a_ref[...], wb_ref[...], preferred_element_type=F32)
    if relu2:
        acc = jnp.square(jnp.maximum(acc, 0.0))
    o_ref[...] = acc.astype(o_ref.dtype)


def _mm(a, w, layer, n, rows, *, relu2=False, tm=1024, tn=1024):
    k = a.shape[1]
    tn = min(tn, n)
    return pl.pallas_call(
        functools.partial(_mm_kernel, relu2=relu2),
        grid=(n // tn, rows // tm),
        in_specs=[pl.BlockSpec((tm, k), lambda j, i: (i, 0)),
                  pl.BlockSpec((None, k, tn), lambda j, i: (layer, 0, j))],
        out_specs=pl.BlockSpec((tm, tn), lambda j, i: (i, j)),
        out_shape=jax.ShapeDtypeStruct((rows, n), BF16),
        scratch_shapes=[pltpu.VMEM((k, tn), BF16)],
        compiler_params=_params("parallel", "arbitrary"),
        name="mm_relu2" if relu2 else "mm",
    )(a, w)


def _mm_res_kernel(*refs, n_lhs, n_res, n_lat_tiles, nk, final_norm, next_norm):
    a_refs = refs[:n_lhs]
    w_refs = refs[n_lhs:2 * n_lhs]
    res_refs = refs[2 * n_lhs:2 * n_lhs + n_res]
    gate_ref = refs[2 * n_lhs + n_res]
    pos = 2 * n_lhs + n_res + 1
    n_extra = 1 if final_norm else (3 if next_norm else 0)
    extra = refs[pos:pos + n_extra]
    o_ref = refs[pos + n_extra]
    h_ref = refs[pos + n_extra + 1] if next_norm else None

    def rms(y, g_ref):
        ms = jnp.mean(y * y, axis=-1, keepdims=True)
        return y * lax.rsqrt(ms + EPS) * g_ref[...]

    def emit_next(y):
        sh_ref, sc_ref, g_ref = extra
        h_ref[...] = (rms(y, g_ref) * (1.0 + sc_ref[...]) + sh_ref[...]).astype(h_ref.dtype)

    kk = pl.program_id(1)
    if nk > 1:
        @pl.when(kk == 0)
        def _():
            o_ref[...] = _stream_tile(res_refs, n_lat_tiles)

    wn = D // MM_RES_NSPLIT
    for cc in range(MM_RES_NSPLIT):
        cols = slice(cc * wn, (cc + 1) * wn)
        part = None
        for a_ref, w_ref in zip(a_refs, w_refs):
            d = jnp.dot(a_ref[...], w_ref[:, cols], preferred_element_type=F32)
            part = d if part is None else part + d
        part = gate_ref[:, cols] * part
        if nk == 1:
            o_ref[:, cols] = _stream_tile(res_refs, n_lat_tiles, cols) + part
        else:
            o_ref[:, cols] += part

    def epilogue():
        if final_norm:
            o_ref[...] = rms(o_ref[...], extra[0])
        elif next_norm:
            emit_next(o_ref[...])

    if nk == 1:
        epilogue()
    elif final_norm or next_norm:
        pl.when(kk == nk - 1)(epilogue)


def _mm_res(a_list, w, layer, resid, mod, gate_idx, rows, *, final_g=None, nxt=None, tm=512):
    n_lhs = len(a_list)
    kdim = a_list[0].shape[1]
    tk = min(kdim, 2048)
    nk = kdim // tk
    nl, res_args, res_specs = _stream_specs(resid, tm, True)

    def mod_spec(idx):
        return pl.BlockSpec((None, 1, D), lambda i, k: (_mod_row(i, tm), 0, idx))

    def w_spec(j):
        return pl.BlockSpec((None, tk, D), lambda i, k: (layer, j * nk + k, 0))

    row_spec = pl.BlockSpec((tm, D), lambda i, k: (i, 0))
    vec_spec = pl.BlockSpec((1, D), lambda i, k: (0, 0))
    in_specs = ([pl.BlockSpec((tm, tk), lambda i, k: (i, k)) for _ in a_list]
                + [w_spec(j) for j in range(n_lhs)] + res_specs + [mod_spec(gate_idx)])
    args = list(a_list) + [w] * n_lhs + res_args + [mod]
    out_specs, out_shape = row_spec, jax.ShapeDtypeStruct((rows, D), F32)
    if final_g is not None:
        in_specs.append(vec_spec)
        args.append(final_g.reshape(1, D))
    elif nxt is not None:
        mod_n, g_n, shift_idx, scale_idx = nxt
        in_specs += [mod_spec(shift_idx), mod_spec(scale_idx), vec_spec]
        args += [mod_n, mod_n, g_n.reshape(1, D)]
        out_specs = [row_spec, row_spec]
        out_shape = [out_shape, jax.ShapeDtypeStruct((rows, D), BF16)]
    return pl.pallas_call(
        functools.partial(_mm_res_kernel, n_lhs=n_lhs, n_res=len(res_args), n_lat_tiles=nl,
                          nk=nk, final_norm=final_g is not None, next_norm=nxt is not None),
        grid=(rows // tm, nk),
        in_specs=in_specs,
        out_specs=out_specs,
        out_shape=out_shape,
        compiler_params=_params("parallel", "arbitrary"),
        name="mm_res",
    )(*args)


def _na_tables(rel_bias):
    hp = lax.Precision.HIGHEST
    cq = np.arange(GRID_W)[:, None]
    ck = np.arange(GRID_W)[None, :]
    ws = np.clip(cq - 8, 0, GRID_W - 16)
    col_ok = (ck >= ws) & (ck < ws + 16)
    col_hot = ((ck - cq + 15)[..., None] == np.arange(31)) & col_ok[..., None]
    blocks = jnp.einsum('hrj,qkj->hrqk', rel_bias.astype(F32), col_hot.astype(np.float32),
                        precision=hp)
    blocks = blocks + np.where(col_ok, 0.0, NEG_INF).astype(np.float32)
    masked = jnp.full((NA_H, GRID_W, GRID_W), NEG_INF, F32)
    pats = []
    for r0, start in ((0, 0), (NA_QROWS, 0), (ROWS - NA_QROWS, ROWS - NA_KROWS)):
        rows = []
        for a in range(NA_QROWS):
            r = r0 + a
            rs = min(max(r - 4, 0), ROWS - 8)
            rows.append(jnp.concatenate(
                [blocks[:, krow - r + 7] if rs <= krow < rs + 8 else masked
                 for krow in range(start, start + NA_KROWS)], axis=-1))
        pats.append(jnp.concatenate(rows, axis=1))
    return jnp.stack(pats, axis=1)


def _na_kernel(q_ref, k0_ref, k1_ref, k2_ref, v0_ref, v1_ref, v2_ref, kc_ref, vc_ref,
               tab_ref, o_ref):
    i = pl.program_id(1)

    def head(ref, hh):
        return ref[:, hh * NA_DH:(hh + 1) * NA_DH]

    def ctx_scores(hh):
        return lax.dot_general(head(q_ref, hh), head(kc_ref, hh), _NT_DIMS,
                               preferred_element_type=F32) * NA_SCALE

    @pl.when(i < ROWS // NA_QROWS)
    def _():
        for hh in range(NA_HB):
            q = head(q_ref, hh)
            s_c = ctx_scores(hh)
            m = jnp.max(s_c, axis=-1, keepdims=True)
            s_w = []
            for d, k_ref in enumerate((k0_ref, k1_ref, k2_ref)):
                s = lax.dot_general(q, head(k_ref, hh), _NT_DIMS,
                                    preferred_element_type=F32) * NA_SCALE
                s = s + tab_ref[hh, :, d * NA_QT:(d + 1) * NA_QT]
                s_w.append(s)
                m = jnp.maximum(m, jnp.max(s, axis=-1, keepdims=True))
            p_c = jnp.exp(s_c - m)
            l = jnp.sum(p_c, axis=-1, keepdims=True)
            o = jnp.dot(p_c.astype(BF16), head(vc_ref, hh), preferred_element_type=F32)
            for s, v_ref in zip(s_w, (v0_ref, v1_ref, v2_ref)):
                p = jnp.exp(s - m)
                l = l + jnp.sum(p, axis=-1, keepdims=True)
                o = o + jnp.dot(p.astype(BF16), head(v_ref, hh), preferred_element_type=F32)
            o_ref[:, hh * NA_DH:(hh + 1) * NA_DH] = (o / l).astype(o_ref.dtype)

    @pl.when(i == ROWS // NA_QROWS)
    def _():
        for hh in range(NA_HB):
            s_c = ctx_scores(hh)
            p_c = jnp.exp(s_c - jnp.max(s_c, axis=-1, keepdims=True))
            l = jnp.sum(p_c, axis=-1, keepdims=True)
            o = jnp.dot(p_c.astype(BF16), head(vc_ref, hh), preferred_element_type=F32)
            o_ref[:, hh * NA_DH:(hh + 1) * NA_DH] = (o / l).astype(o_ref.dtype)


def _na_attention(qkv, table):
    ng = ROWS // NA_QROWS
    blk = S // NA_QT
    ctx0 = NLAT // NA_QT

    def qrow(h, i, b):
        return jnp.where(i < ng, b * blk + i, ctx0 + b)

    nhb = NA_H // NA_HB

    def krow(d):
        return lambda h, i, b: (b * blk + jnp.clip(i - 1, 0, blk - 3) + d, nhb + h)

    def vrow(d):
        return lambda h, i, b: (b * blk + jnp.clip(i - 1, 0, blk - 3) + d, 2 * nhb + h)

    def pat(h, i, b):
        return (h, jnp.where(i == 0, 0, jnp.where(i >= ng - 1, 2, 1)), 0, 0)

    tile = (NA_QT, NA_HB * NA_DH)
    in_specs = ([pl.BlockSpec(tile, lambda h, i, b: (qrow(h, i, b), h))]
                + [pl.BlockSpec(tile, krow(d)) for d in range(3)]
                + [pl.BlockSpec(tile, vrow(d)) for d in range(3)]
                + [pl.BlockSpec(tile, lambda h, i, b: (ctx0 + b, nhb + h)),
                   pl.BlockSpec(tile, lambda h, i, b: (ctx0 + b, 2 * nhb + h)),
                   pl.BlockSpec((NA_HB, None, NA_QT, NA_KT), pat)])
    return pl.pallas_call(
        _na_kernel,
        grid=(nhb, ng + 1, B),
        in_specs=in_specs,
        out_specs=pl.BlockSpec(tile, lambda h, i, b: (qrow(h, i, b), h)),
        out_shape=jax.ShapeDtypeStruct((NT, NA_W), BF16),
        compiler_params=_params("parallel", "parallel", "parallel"),
        name="na_attention",
    )(*([qkv] * 9), table)


def _s5_matrices(lam_re, lam_im, log_dt, b_re, b_im, c_re, c_im):
    lam_re, lam_im = lam_re.astype(F32), lam_im.astype(F32)
    b_re, b_im = b_re.astype(F32), b_im.astype(F32)
    c_re, c_im = c_re.astype(F32), c_im.astype(F32)
    dt = jnp.exp(log_dt.astype(F32))[..., None]
    mag = jnp.exp(lam_re * dt)
    a_re = mag * jnp.cos(lam_im * dt)
    a_im = mag * jnp.sin(lam_im * dt)
    den = lam_re * lam_re + lam_im * lam_im
    f_re = ((a_re - 1.0) * lam_re + a_im * lam_im) / den
    f_im = (a_im * lam_re - (a_re - 1.0) * lam_im) / den
    bb_re = f_re[..., None] * b_re - f_im[..., None] * b_im
    bb_im = f_re[..., None] * b_im + f_im[..., None] * b_re

    pr, pi = [jnp.ones_like(a_re)], [jnp.zeros_like(a_im)]
    for _ in range(S5_LC):
        pr.append(pr[-1] * a_re - pi[-1] * a_im)
        pi.append(pr[-2] * a_im + pi[-1] * a_re)
    pw_re = jnp.stack(pr)
    pw_im = jnp.stack(pi)

    ab_re = pw_re[..., None] * bb_re[None] - pw_im[..., None] * bb_im[None]
    ab_im = pw_re[..., None] * bb_im[None] + pw_im[..., None] * bb_re[None]
    def lag_minor(z, d):
        return z[:S5_LC, d].transpose(1, 2, 0, 3).reshape(S5_G, 1, S5_P, S5_CW)
    def kernels(d):
        k = jnp.sum(c_re[d][..., None] * lag_minor(ab_re, d)
                    - c_im[d][..., None] * lag_minor(ab_im, d), axis=2)
        return k.reshape(S5_G, S5_CG, S5_LC, S5_CG).transpose(0, 3, 2, 1)
    zpad = jnp.zeros((S5_G, S5_CG, S5_LC, S5_CG), F32)
    kf = jnp.concatenate([zpad, kernels(0)], axis=2)
    kb = jnp.concatenate([jnp.flip(kernels(1), axis=2), zpad], axis=2)
    tc = jnp.stack([kf[:, :, S5_LC - s:2 * S5_LC - s] + kb[:, :, S5_LC - 1 - s:2 * S5_LC - 1 - s]
                    for s in range(S5_LC)], axis=1).reshape(S5_G, S5_CW, S5_CW)

    def st(arr, d, flip):
        z = arr[:S5_LC, d]
        z = jnp.flip(z, axis=0) if flip else z
        return z.transpose(1, 0, 3, 2).reshape(S5_G, S5_CW, S5_P)
    et = jnp.concatenate([st(ab_re, 0, True), st(ab_re, 1, False),
                          st(ab_im, 0, True), st(ab_im, 1, False)], axis=-1)

    def rd(d, flip):
        pr_ = pw_re[1:, d]
        pi_ = pw_im[1:, d]
        if flip:
            pr_, pi_ = jnp.flip(pr_, axis=0), jnp.flip(pi_, axis=0)
        cr = c_re[d][None] * pr_[:, :, None, :] - c_im[d][None] * pi_[:, :, None, :]
        ci = c_re[d][None] * pi_[:, :, None, :] + c_im[d][None] * pr_[:, :, None, :]
        to = lambda z: z.transpose(1, 3, 0, 2).reshape(S5_G, S5_P, S5_CW)
        return to(cr), to(-ci)
    fr, fi = rd(0, False)
    br, bi = rd(1, True)
    z = jnp.zeros_like(fr)
    ft = jnp.concatenate([fr, z, fi, z, z, br, z, bi], axis=1)

    a16_re = jnp.concatenate([pw_re[S5_LC, 0], pw_re[S5_LC, 1]], axis=-1)[:, None, :]
    a16_im = jnp.concatenate([pw_im[S5_LC, 0], pw_im[S5_LC, 1]], axis=-1)[:, None, :]
    return tc.astype(BF16), et.astype(BF16), ft.astype(BF16), a16_re, a16_im


def _uproj_kernel(wt_ref, h_ref, o_ref):
    acc = lax.dot_general(wt_ref[...], h_ref[...], _NT_DIMS, preferred_element_type=F32)
    o_ref[...] = acc.reshape(S5_G, S5_CG, S5_NT).astype(o_ref.dtype)


def _to_chunk_rows(x):
    c = x.shape[1]
    def part(z, n):
        return z.reshape(B, n, S5_LC * c).transpose(1, 0, 2).reshape(n * B, S5_LC * c)
    return jnp.concatenate([part(x[:NLAT], S5_NLAT), part(x[NLAT:], S5_NCTX)], axis=0)


def _from_chunk_rows(y):
    c = y.shape[1] // S5_LC
    def part(z, n):
        return z.reshape(n, B, S5_LC * c).transpose(1, 0, 2).reshape(B * n * S5_LC, c)
    return jnp.concatenate([part(y[:B * S5_NLAT], S5_NLAT), part(y[B * S5_NLAT:], S5_NCTX)], axis=0)


def _s5_uproj(h, w_u_t):
    h2 = _to_chunk_rows(h)
    return pl.pallas_call(
        _uproj_kernel,
        grid=(S5_LC, S5_N // S5_NT),
        in_specs=[pl.BlockSpec((S5_W, D), lambda t, n: (0, 0)),
                  pl.BlockSpec((S5_NT, D), lambda t, n: (n, t))],
        out_specs=pl.BlockSpec((S5_G, None, S5_CG, S5_NT), lambda t, n: (0, t, 0, n)),
        out_shape=jax.ShapeDtypeStruct((S5_G, S5_LC, S5_CG, S5_N), BF16),
        compiler_params=_params("parallel", "parallel"),
        name="s5_uproj",
    )(w_u_t, h2)


def _s5_chunk_rows(kind, k):
    if kind == "c":
        return pl.ds(B * (S5_NLAT + k), B)
    return pl.ds(B * k, B)


def _s5_kernel(ut_ref, tc_ref, et_ref, ft_ref, ar_ref, ai_ref, d_ref, o_ref,
               he_re_ref, he_im_ref, hpf_re_ref, hpf_im_ref, hpb_re_ref, hpb_im_ref):
    sw = 2 * S5_P
    utf = ut_ref[...].reshape(S5_CW, S5_N).astype(F32)
    un = utf.T.astype(BF16)
    y = jnp.dot(un, tc_ref[...], preferred_element_type=F32)
    he = jnp.dot(un, et_ref[...], preferred_element_type=F32)
    he_re_ref[...] = he[:, :sw]
    he_im_ref[...] = he[:, sw:]

    ar = ar_ref[...]
    ai = ai_ref[...]
    is_fwd = lax.broadcasted_iota(jnp.int32, (B, 2 * S5_P), 1) < S5_P
    h_re = jnp.zeros((B, 2 * S5_P), F32)
    h_im = jnp.zeros((B, 2 * S5_P), F32)
    fwd = [("c", k) for k in range(S5_NCTX)] + [("l", k) for k in range(S5_NLAT)]
    bwd = ([("c", k) for k in range(S5_NCTX - 1, -1, -1)]
           + [("l", k) for k in range(S5_NLAT - 1, -1, -1)])
    for cf, cb in zip(fwd, bwd):
        rf = _s5_chunk_rows(*cf)
        rb = _s5_chunk_rows(*cb)
        hpf_re_ref[rf, :] = h_re
        hpf_im_ref[rf, :] = h_im
        hpb_re_ref[rb, :] = h_re
        hpb_im_ref[rb, :] = h_im
        e_re = jnp.where(is_fwd, he_re_ref[rf, :], he_re_ref[rb, :])
        e_im = jnp.where(is_fwd, he_im_ref[rf, :], he_im_ref[rb, :])
        n_re = ar * h_re - ai * h_im + e_re
        n_im = ar * h_im + ai * h_re + e_im
        h_re, h_im = n_re, n_im

    hp = jnp.concatenate([hpf_re_ref[...], hpf_im_ref[...], hpb_re_ref[...], hpb_im_ref[...]],
                         axis=1).astype(BF16)
    y = y + jnp.dot(hp, ft_ref[...], preferred_element_type=F32)
    g = y.T + d_ref[...] * utf
    gl = 0.5 * g * (1.0 + lax.erf(g * (0.5 ** 0.5)))
    o_ref[...] = gl.astype(o_ref.dtype).reshape(S5_LC, S5_CG, S5_N)


def _s5_scan(ut, mats, d_col):
    tc, et, ft, a_re, a_im = mats
    mat = pl.BlockSpec((None, S5_CW, S5_CW), lambda g: (g, 0, 0))
    vec = pl.BlockSpec((None, 1, 2 * S5_P), lambda g: (g, 0, 0))
    io = pl.BlockSpec((None, S5_LC, S5_CG, S5_N), lambda g: (g, 0, 0, 0))
    return pl.pallas_call(
        _s5_kernel,
        grid=(S5_G,),
        in_specs=[io, mat, mat,
                  pl.BlockSpec((None, 2 * S5_CW, S5_CW), lambda g: (g, 0, 0)), vec, vec,
                  pl.BlockSpec((None, S5_CW, 1), lambda g: (g, 0, 0))],
        out_specs=io,
        out_shape=jax.ShapeDtypeStruct((S5_G, S5_LC, S5_CG, S5_N), BF16),
        scratch_shapes=[pltpu.VMEM((S5_N, 2 * S5_P), F32)] * 6,
        compiler_params=_params("parallel"),
        name="s5_scan",
    )(ut, tc, et, ft, a_re, a_im, d_col)


def _glu_kernel(gl_ref, w_ref, b_ref, o_ref):
    gl = gl_ref[...].reshape(S5_W, S5_NT)
    z = jnp.dot(w_ref[...], gl, preferred_element_type=F32) + b_ref[...]
    s = gl.astype(F32) * jax.nn.sigmoid(z)
    o_ref[...] = s.T.astype(o_ref.dtype)


def _s5_glu(glt, w_t, b_col):
    out = pl.pallas_call(
        _glu_kernel,
        grid=(S5_LC, S5_N // S5_NT),
        in_specs=[pl.BlockSpec((S5_G, None, S5_CG, S5_NT), lambda t, n: (0, t, 0, n)),
                  pl.BlockSpec((S5_W, S5_W), lambda t, n: (0, 0)),
                  pl.BlockSpec((S5_W, 1), lambda t, n: (0, 0))],
        out_specs=pl.BlockSpec((S5_NT, S5_W), lambda t, n: (n, t)),
        out_shape=jax.ShapeDtypeStruct((S5_N, S5_LC * S5_W), BF16),
        compiler_params=_params("parallel", "parallel"),
        name="s5_glu",
    )(glt, w_t, b_col)
    return _from_chunk_rows(out)


def _rope_tables():
    half = GLA_DK // 2
    freqs = ROPE_BASE ** (-np.arange(0, half, 2, dtype=np.float32) / half)
    out = []
    for n in (ROWS, GRID_W):
        ang = np.arange(n, dtype=np.float32)[:, None] * freqs[None, :]
        c, s = np.cos(ang), np.sin(ang)
        out += [np.concatenate([c, c], axis=-1), np.concatenate([-s, s], axis=-1)]
    return tuple(jnp.asarray(t, F32) for t in out)


def _rope(x, rcos, rsin, ccos, csin):
    x0 = x[:, :128]
    x1 = x[:, 128:]
    return jnp.concatenate([x0 * rcos + pltpu.roll(x0, 64, axis=1) * rsin,
                            x1 * ccos + pltpu.roll(x1, 64, axis=1) * csin], axis=-1)


def _cumsum_rows(x, reverse):
    row = lax.broadcasted_iota(jnp.int32, x.shape, 0)
    s = 1
    while s < GLA_C:
        if reverse:
            x = x + jnp.where(row < GLA_C - s, pltpu.roll(x, GLA_C - s, axis=0), 0.0)
        else:
            x = x + jnp.where(row >= s, pltpu.roll(x, s, axis=0), 0.0)
        s *= 2
    return x


def _log_sigmoid(x):
    return jnp.minimum(x, 0.0) - jnp.log1p(jnp.exp(-jnp.abs(x)))


def _chunk_rows(c):
    if isinstance(c, int):
        return pl.ds(c * GLA_C, GLA_C)
    return pl.ds(pl.multiple_of(c * GLA_C, GLA_C), GLA_C)


def _gla_kernel(ql_ref, kl_ref, vl_ref, gl_ref, qc_ref, kc_ref, vc_ref, al_ref, ac_ref,
                waf_ref, wab_ref, baf_ref, bab_ref, rcos_ref, rsin_ref, ccos_ref, csin_ref,
                ng_ref, o_ref,
                qi_f, ki_f, ke_f, dec_f, st_f, qi_b, ki_b, ke_b, dec_b, st_b, acc_ref):
    n_ctx = L // GLA_C
    n_lat = S // GLA_C
    qscale = GLA_DK ** -0.5
    fwd = (waf_ref, baf_ref, qi_f, ki_f, ke_f, dec_f, False)
    bwd = (wab_ref, bab_ref, qi_b, ki_b, ke_b, dec_b, True)

    ii = lax.broadcasted_iota(jnp.int32, (GLA_C, GLA_C), 0)
    jj = lax.broadcasted_iota(jnp.int32, (GLA_C, GLA_C), 1)

    def prepare(direction, q, k, a, c):
        wa_ref, ba_ref, qi, ki, ke, dec, reverse = direction
        dst = _chunk_rows(c)
        la = _log_sigmoid(jnp.dot(a, wa_ref[...], preferred_element_type=F32)
                          + ba_ref[...]) / GLA_TAU
        bc = _cumsum_rows(la, reverse)
        b_last = bc[0:1, :] if reverse else bc[GLA_C - 1:GLA_C, :]
        qi[dst, :] = (q * jnp.exp(bc)).astype(BF16)
        ki[dst, :] = (k * jnp.exp(-bc)).astype(BF16)
        ke[dst, :] = (k * jnp.exp(b_last - bc)).astype(BF16)
        dec[pl.ds(c, 1), :] = jnp.exp(b_last)

    def prepare_latent(direction, c):
        r = _chunk_rows(c)
        tabs = (rcos_ref[pl.ds(c, 1), :], rsin_ref[pl.ds(c, 1), :], ccos_ref[...], csin_ref[...])
        q = _rope(ql_ref[r, :].astype(F32) * qscale, *tabs)
        k = _rope(kl_ref[r, :].astype(F32), *tabs)
        prepare(direction, q, k, al_ref[r, :], n_ctx + c)

    def prep_ctx(c, carry):
        r = _chunk_rows(c)
        q = qc_ref[r, :].astype(F32) * qscale
        k = kc_ref[r, :].astype(F32)
        prepare(fwd, q, k, ac_ref[r, :], c)
        prepare(bwd, q, k, ac_ref[r, :], c)
        return carry
    lax.fori_loop(0, n_ctx, prep_ctx, 0)
    prepare_latent(fwd, 0)
    prepare_latent(bwd, n_lat - 1)

    def advance(direction, st_ref, c, v, want_out):
        _, _, qi, ki, ke, dec, reverse = direction
        r = _chunk_rows(c)
        st = st_ref[...]
        o = None
        if want_out:
            q_in = qi[r, :]
            att = lax.dot_general(q_in, ki[r, :], _NT_DIMS, preferred_element_type=F32)
            att = jnp.where((ii <= jj) if reverse else (ii >= jj), att, 0.0)
            o = (jnp.dot(att.astype(BF16), v, preferred_element_type=F32)
                 + lax.dot_general(q_in, st.astype(BF16), _NT_DIMS, preferred_element_type=F32))
        vt = v.astype(F32).T.astype(BF16)
        st_ref[...] = dec[pl.ds(c, 1), :] * st + jnp.dot(vt, ke[r, :],
                                                          preferred_element_type=F32)
        return o

    st_f[...] = jnp.zeros_like(st_f)
    st_b[...] = jnp.zeros_like(st_b)

    def ctx_pair(j, carry):
        cb = n_ctx - 1 - j
        advance(fwd, st_f, j, vc_ref[_chunk_rows(j), :], False)
        advance(bwd, st_b, cb, vc_ref[_chunk_rows(cb), :], False)
        return carry
    lax.fori_loop(0, n_ctx, ctx_pair, 0)

    def lat_pair(j, accumulate):
        cb = n_lat - 1 - j
        rf = _chunk_rows(j)
        rb = _chunk_rows(cb)
        o_f = advance(fwd, st_f, n_ctx + j, vl_ref[rf, :], True)
        o_b = advance(bwd, st_b, n_ctx + cb, vl_ref[rb, :], True)
        if accumulate:
            acc_ref[rf, :] += o_f
            acc_ref[rb, :] += o_b
        else:
            acc_ref[rf, :] = o_f
            acc_ref[rb, :] = o_b
        prepare_latent(fwd, jnp.minimum(j + 1, n_lat - 1))
        prepare_latent(bwd, jnp.maximum(cb - 1, 0))

    def lat_first(j, carry):
        lat_pair(j, False)
        return carry
    lax.fori_loop(0, n_lat // 2, lat_first, 0, unroll=2)

    def lat_second(j, carry):
        lat_pair(j, True)
        return carry
    lax.fori_loop(n_lat // 2, n_lat, lat_second, 0, unroll=2)

    tr = 256

    def fin(t, carry):
        r = pl.ds(pl.multiple_of(t * tr, tr), tr)
        o = acc_ref[r, :]
        ms = jnp.mean(o * o, axis=-1, keepdims=True)
        g = gl_ref[r, :].astype(F32)
        o_ref[r, :] = (o * lax.rsqrt(ms + EPS) * ng_ref[...]
                       * (g * jax.nn.sigmoid(g))).astype(o_ref.dtype)
        return carry
    lax.fori_loop(0, S // tr, fin, 0)


def _gla(qkvg, acode, wa, ba, rope, norm_g):
    ctx0 = NLAT // L
    kq = GLA_QK // GLA_DK
    half = GLA_DK // 2
    n_chunks = (L + S) // GLA_C
    per_dir = [pltpu.VMEM((L + S, GLA_DK), BF16)] * 3 + [pltpu.VMEM((n_chunks, GLA_DK), F32),
                                                         pltpu.VMEM((GLA_DV, GLA_DK), F32)]
    in_specs = [
        pl.BlockSpec((S, GLA_DK), lambda b, h: (b, h)),
        pl.BlockSpec((S, GLA_DK), lambda b, h: (b, kq + h)),
        pl.BlockSpec((S, GLA_DV), lambda b, h: (b, kq + h)),
        pl.BlockSpec((S, GLA_DV), lambda b, h: (b, 2 * kq + h)),
        pl.BlockSpec((L, GLA_DK), lambda b, h: (ctx0 + b, h)),
        pl.BlockSpec((L, GLA_DK), lambda b, h: (ctx0 + b, kq + h)),
        pl.BlockSpec((L, GLA_DV), lambda b, h: (ctx0 + b, kq + h)),
        pl.BlockSpec((S, 128), lambda b, h: (b, 0)),
        pl.BlockSpec((L, 128), lambda b, h: (ctx0 + b, 0)),
        pl.BlockSpec((128, GLA_DK), lambda b, h: (0, h)),
        pl.BlockSpec((128, GLA_DK), lambda b, h: (0, kq + h)),
        pl.BlockSpec((1, GLA_DK), lambda b, h: (0, h)),
        pl.BlockSpec((1, GLA_DK), lambda b, h: (0, kq + h)),
        pl.BlockSpec((ROWS, half), lambda b, h: (0, 0)),
        pl.BlockSpec((ROWS, half), lambda b, h: (0, 0)),
        pl.BlockSpec((GRID_W, half), lambda b, h: (0, 0)),
        pl.BlockSpec((GRID_W, half), lambda b, h: (0, 0)),
        pl.BlockSpec((1, GLA_DV), lambda b, h: (0, 0)),
    ]
    return pl.pallas_call(
        _gla_kernel,
        grid=(B, GLA_H),
        in_specs=in_specs,
        out_specs=pl.BlockSpec((S, GLA_DV), lambda b, h: (b, h)),
        out_shape=jax.ShapeDtypeStruct((NLAT, GLA_VW), BF16),
        scratch_shapes=per_dir + per_dir + [pltpu.VMEM((S, GLA_DV), F32)],
        compiler_params=_params("parallel", "parallel"),
        name="gla",
    )(qkvg, qkvg, qkvg, qkvg, qkvg, qkvg, qkvg, acode, acode, wa, wa, ba, ba, *rope,
      norm_g.reshape(1, GLA_DV))


def kernel(x, c, ctx, c_ctx, ada_w, ada_b, norm1_g, norm2_g, mlp_w1, mlp_w2, final_g, ab_w_in, ab_w_out, na_rel_bias, s5_lambda_re, s5_lambda_im, s5_log_dt, s5_b_re, s5_b_im, s5_c_re, s5_c_im, s5_d, s5_glu_w, s5_glu_b, gla_w_in, gla_w_a2, gla_b_a, gla_norm_g, gla_w_out):
    xs = (x.astype(F32).reshape(NLAT, D), ctx.astype(F32).reshape(NCTX, D))
    cvec = jnp.zeros((16, D), F32).at[:B].set(c.astype(F32)).at[B].set(c_ctx.astype(F32))
    mods = _ada_mod(cvec, ada_w, ada_b).reshape(2, 16, 1, 6 * D)
    bf = lambda w: w.astype(BF16)

    mod = mods[0]
    h = _normmod(xs, mod, norm1_g[0], 0, 1, NT)
    w_in = ab_w_in[0]
    qkv = _mm(h, ab_w_in, 0, 3 * NA_W, NT)
    att = _na_attention(qkv, _na_tables(na_rel_bias[0]))
    ut = _s5_uproj(h, bf(w_in[:, 3 * NA_W:].T))
    mats = _s5_matrices(s5_lambda_re[0], s5_lambda_im[0], s5_log_dt[0], s5_b_re[0], s5_b_im[0],
                        s5_c_re[0], s5_c_im[0])
    d_col = jnp.tile(s5_d[0].astype(F32).reshape(S5_G, 1, S5_CG), (1, S5_LC, 1)).reshape(S5_G, S5_CW, 1)
    glt = _s5_scan(ut, mats, d_col)
    s5 = _s5_glu(glt, bf(s5_glu_w[0].T), s5_glu_b[0].astype(F32).reshape(S5_W, 1))
    w2 = bf(mlp_w2)
    xs, h = _mm_res([att, s5], bf(ab_w_out), 0, xs, mod, 2, NT, nxt=(mod, norm2_g[0], 3, 4))
    hid = _mm(h, mlp_w1, 0, MLP_H, NT, relu2=True)
    xs, h = _mm_res([hid], w2, 0, xs, mod, 5, NT, nxt=(mods[1], norm1_g[1], 0, 1))

    mod = mods[1]
    w_in = gla_w_in[0]
    qkvg = _mm(h, gla_w_in, 0, GLA_MAIN, NT)
    w_code = jnp.zeros((1, D, 128), F32).at[0, :, :2 * GLA_RANK].set(w_in[:, GLA_MAIN:])
    acode = _mm(h, w_code, 0, 128, NT)
    wa = (jnp.zeros((128, 2 * GLA_QK), F32)
          .at[:GLA_RANK, :GLA_QK].set(gla_w_a2[0, 0])
          .at[GLA_RANK:2 * GLA_RANK, GLA_QK:].set(gla_w_a2[0, 1]))
    ba = gla_b_a[0].astype(F32).reshape(1, 2 * GLA_QK)
    og = _gla(qkvg, acode, bf(wa), ba, _rope_tables(), gla_norm_g[0].astype(F32))
    xl, h = _mm_res([og], bf(gla_w_out), 0, xs, mod, 2, NLAT, nxt=(mod, norm2_g[1], 3, 4))
    hid = _mm(h, mlp_w1, 1, MLP_H, NLAT, relu2=True)
    out = _mm_res([hid], w2, 1, xl, mod, 5, NLAT, final_g=final_g.astype(F32))
    return out.reshape(B, S, D).astype(x.dtype)
```

```python
import functools
import math

import numpy as np
import jax
import jax.numpy as jnp
from jax import lax
from jax.experimental import pallas as pl
from jax.experimental.pallas import tpu as pltpu

F32 = jnp.float32
BF16 = jnp.bfloat16

D = 2048
B = 8
S = 2048
L = 256
GRID_W = 64
ROWS = S // GRID_W
NLAT = B * S
NCTX = B * L
NT = NLAT + NCTX
MLP_H = 4 * D
EPS = 1e-6
NEG_INF = -1e30

NA_H = 8
NA_DH = 128
NA_W = NA_H * NA_DH
NA_SCALE = NA_DH ** -0.5
NA_HB = 4
NA_QROWS = 4
NA_KROWS = 12
NA_QT = NA_QROWS * GRID_W
NA_KT = NA_KROWS * GRID_W

S5_W = D // 2
S5_CG = 16
S5_G = S5_W // S5_CG
S5_P = 64
S5_LC = 16
S5_CW = S5_LC * S5_CG
S5_NLAT = S // S5_LC
S5_NCTX = L // S5_LC
S5_N = B * (S5_NLAT + S5_NCTX)
S5_NT = 384

GLA_H = 4
GLA_DK = 256
GLA_DV = 512
GLA_QK = GLA_H * GLA_DK
GLA_VW = GLA_H * GLA_DV
GLA_RANK = 16
GLA_TAU = 16.0
GLA_C = 64
GLA_MAIN = 2 * GLA_QK + 2 * GLA_VW
ROPE_BASE = 10000.0

MM_RES_NSPLIT = 4

VMEM_LIMIT = 56 * 1024 * 1024

_NT_DIMS = (((1,), (1,)), ((), ()))


def _params(*sem):
    return pltpu.CompilerParams(dimension_semantics=sem, vmem_limit_bytes=VMEM_LIMIT)


def _mod_row(i, tm):
    return jnp.minimum((i * tm) // S, B)


def _ada_kernel(c_ref, w_ref, b_ref, o_ref):
    c = c_ref[...]
    s = c * jax.nn.sigmoid(c)
    o_ref[...] = jnp.dot(s.astype(BF16), w_ref[...].astype(BF16),
                         preferred_element_type=F32) + b_ref[...]


def _ada_mod(cvec, ada_w, ada_b):
    depth = ada_w.shape[0]
    tn = 512
    return pl.pallas_call(
        _ada_kernel,
        grid=(depth, 6 * D // tn),
        in_specs=[pl.BlockSpec((16, D), lambda l, j: (0, 0)),
                  pl.BlockSpec((None, D, tn), lambda l, j: (l, 0, j)),
                  pl.BlockSpec((None, 1, tn), lambda l, j: (l, 0, j))],
        out_specs=pl.BlockSpec((None, 16, tn), lambda l, j: (l, 0, j)),
        out_shape=jax.ShapeDtypeStruct((depth, 16, 6 * D), F32),
        compiler_params=_params("parallel", "parallel"),
        name="ada_mod",
    )(cvec, ada_w, ada_b.reshape(depth, 1, 6 * D))


def _stream_specs(xs, tm, two_axes):
    if not isinstance(xs, tuple):
        imap = (lambda i, k: (i, 0)) if two_axes else (lambda i: (i, 0))
        return 0, [xs], [pl.BlockSpec((tm, D), imap)]
    nl = xs[0].shape[0] // tm
    if two_axes:
        maps = [lambda i, k: (jnp.minimum(i, nl - 1), 0), lambda i, k: (jnp.maximum(i - nl, 0), 0)]
    else:
        maps = [lambda i: (jnp.minimum(i, nl - 1), 0), lambda i: (jnp.maximum(i - nl, 0), 0)]
    return nl, list(xs), [pl.BlockSpec((tm, D), m) for m in maps]


def _stream_tile(x_refs, n_lat_tiles, cols=slice(None)):
    if len(x_refs) == 1:
        return x_refs[0][:, cols]
    return jnp.where(pl.program_id(0) < n_lat_tiles, x_refs[0][:, cols], x_refs[1][:, cols])


def _normmod_kernel(*refs, n_x, n_lat_tiles):
    x_refs = refs[:n_x]
    sh_ref, sc_ref, g_ref, o_ref, oc_ref = refs[n_x:]
    x = _stream_tile(x_refs, n_lat_tiles)
    ms = jnp.mean(x * x, axis=-1, keepdims=True)
    h = x * lax.rsqrt(ms + EPS) * g_ref[...]
    h = h * (1.0 + sc_ref[...]) + sh_ref[...]
    o_ref[...] = h.astype(o_ref.dtype)
    h3 = h.reshape(h.shape[0] // S5_LC, S5_LC, D)
    for tl in range(S5_LC):
        oc_ref[tl] = h3[:, tl, :].astype(oc_ref.dtype)


def _normmod(xs, mod, g, shift_idx, scale_idx, rows):
    tm = 512
    nl, x_args, x_specs = _stream_specs(xs, tm, False)
    return pl.pallas_call(
        functools.partial(_normmod_kernel, n_x=len(x_args), n_lat_tiles=nl),
        grid=(rows // tm,),
        in_specs=x_specs + [
            pl.BlockSpec((None, 1, D), lambda i: (_mod_row(i, tm), 0, shift_idx)),
            pl.BlockSpec((None, 1, D), lambda i: (_mod_row(i, tm), 0, scale_idx)),
            pl.BlockSpec((1, D), lambda i: (0, 0))],
        out_specs=[pl.BlockSpec((tm, D), lambda i: (i, 0)),
                   pl.BlockSpec((S5_LC, tm // S5_LC, D), lambda i: (0, i, 0))],
        out_shape=[jax.ShapeDtypeStruct((rows, D), BF16),
                   jax.ShapeDtypeStruct((S5_LC, rows // S5_LC, D), BF16)],
        compiler_params=_params("parallel"),
        name="normmod",
    )(*x_args, mod, mod, g.reshape(1, D))


def _mm_kernel(a_ref, w_ref, o_ref, wb_ref, *, relu2):
    @pl.when(pl.program_id(1) == 0)
    def _():
        wb_ref[...] = w_ref[...].astype(BF16)

    acc = jnp.dot(a_ref[...], wb_ref[...], preferred_element_type=F32)
    if relu2:
        acc = jnp.square(jnp.maximum(acc, 0.0))
    o_ref[...] = acc.astype(o_ref.dtype)


def _mm(a, w, layer, n, rows, *, relu2=False, tm=2048, tn=1024):
    k = a.shape[1]
    tn = min(tn, n)
    return pl.pallas_call(
        functools.partial(_mm_kernel, relu2=relu2),
        grid=(n // tn, rows // tm),
        in_specs=[pl.BlockSpec((tm, k), lambda j, i: (i, 0)),
                  pl.BlockSpec((None, k, tn), lambda j, i: (layer, 0, j))],
        out_specs=pl.BlockSpec((tm, tn), lambda j, i: (i, j)),
        out_shape=jax.ShapeDtypeStruct((rows, n), BF16),
        scratch_shapes=[pltpu.VMEM((k, tn), BF16)],
        compiler_params=_params("parallel", "arbitrary"),
        name="mm_relu2" if relu2 else "mm",
    )(a, w)


def _mm_res_kernel(*refs, n_lhs, n_res, n_lat_tiles, nk, final_norm, next_norm):
    a_refs = refs[:n_lhs]
    w_refs = refs[n_lhs:2 * n_lhs]
    res_refs = refs[2 * n_lhs:2 * n_lhs + n_res]
    gate_ref = refs[2 * n_lhs + n_res]
    pos = 2 * n_lhs + n_res + 1
    n_extra = 1 if final_norm else (3 if next_norm else 0)
    extra = refs[pos:pos + n_extra]
    o_ref = refs[pos + n_extra]
    h_ref = refs[pos + n_extra + 1] if next_norm else None

    def rms(y, g_ref):
        ms = jnp.mean(y * y, axis=-1, keepdims=True)
        return y * lax.rsqrt(ms + EPS) * g_ref[...]

    def emit_next(y):
        sh_ref, sc_ref, g_ref = extra
        h_ref[...] = (rms(y, g_ref) * (1.0 + sc_ref[...]) + sh_ref[...]).astype(h_ref.dtype)

    kk = pl.program_id(1)
    if nk > 1:
        @pl.when(kk == 0)
        def _():
            o_ref[...] = _stream_tile(res_refs, n_lat_tiles)

    wn = D // MM_RES_NSPLIT
    for cc in range(MM_RES_NSPLIT):
        cols = slice(cc * wn, (cc + 1) * wn)
        part = None
        for a_ref, w_ref in zip(a_refs, w_refs):
            d = jnp.dot(a_ref[...], w_ref[:, cols], preferred_element_type=F32)
            part = d if part is None else part + d
        part = gate_ref[:, cols] * part
        if nk == 1:
            o_ref[:, cols] = _stream_tile(res_refs, n_lat_tiles, cols) + part
        else:
            o_ref[:, cols] += part

    def epilogue():
        if final_norm:
            o_ref[...] = rms(o_ref[...], extra[0])
        elif next_norm:
            emit_next(o_ref[...])

    if nk == 1:
        epilogue()
    elif final_norm or next_norm:
        pl.when(kk == nk - 1)(epilogue)


def _mm_res(a_list, w, layer, resid, mod, gate_idx, rows, *, final_g=None, nxt=None, tm=512):
    n_lhs = len(a_list)
    kdim = a_list[0].shape[1]
    tk = min(kdim, 2048)
    nk = kdim // tk
    nl, res_args, res_specs = _stream_specs(resid, tm, True)

    def mod_spec(idx):
        return pl.BlockSpec((None, 1, D), lambda i, k: (_mod_row(i, tm), 0, idx))

    def w_spec(j):
        return pl.BlockSpec((None, tk, D), lambda i, k: (layer, j * nk + k, 0))

    row_spec = pl.BlockSpec((tm, D), lambda i, k: (i, 0))
    vec_spec = pl.BlockSpec((1, D), lambda i, k: (0, 0))
    in_specs = ([pl.BlockSpec((tm, tk), lambda i, k: (i, k)) for _ in a_list]
                + [w_spec(j) for j in range(n_lhs)] + res_specs + [mod_spec(gate_idx)])
    args = list(a_list) + [w] * n_lhs + res_args + [mod]
    out_specs, out_shape = row_spec, jax.ShapeDtypeStruct((rows, D), F32)
    if final_g is not None:
        in_specs.append(vec_spec)
        args.append(final_g.reshape(1, D))
    elif nxt is not None:
        mod_n, g_n, shift_idx, scale_idx = nxt
        in_specs += [mod_spec(shift_idx), mod_spec(scale_idx), vec_spec]
        args += [mod_n, mod_n, g_n.reshape(1, D)]
        out_specs = [row_spec, row_spec]
        out_shape = [out_shape, jax.ShapeDtypeStruct((rows, D), BF16)]
    return pl.pallas_call(
        functools.partial(_mm_res_kernel, n_lhs=n_lhs, n_res=len(res_args), n_lat_tiles=nl,
                          nk=nk, final_norm=final_g is not None, next_norm=nxt is not None),
        grid=(rows // tm, nk),
        in_specs=in_specs,
        out_specs=out_specs,
        out_shape=out_shape,
        compiler_params=_params("parallel", "arbitrary"),
        name="mm_res",
    )(*args)


def _na_tables(rel_bias):
    hp = lax.Precision.HIGHEST
    cq = np.arange(GRID_W)[:, None]
    ck = np.arange(GRID_W)[None, :]
    ws = np.clip(cq - 8, 0, GRID_W - 16)
    col_ok = (ck >= ws) & (ck < ws + 16)
    col_hot = ((ck - cq + 15)[..., None] == np.arange(31)) & col_ok[..., None]
    blocks = jnp.einsum('hrj,qkj->hrqk', rel_bias.astype(F32), col_hot.astype(np.float32),
                        precision=hp)
    blocks = blocks + np.where(col_ok, 0.0, NEG_INF).astype(np.float32)
    masked = jnp.full((NA_H, GRID_W, GRID_W), NEG_INF, F32)
    pats = []
    for r0, start in ((0, 0), (NA_QROWS, 0), (ROWS - NA_QROWS, ROWS - NA_KROWS)):
        rows = []
        for a in range(NA_QROWS):
            r = r0 + a
            rs = min(max(r - 4, 0), ROWS - 8)
            rows.append(jnp.concatenate(
                [blocks[:, krow - r + 7] if rs <= krow < rs + 8 else masked
                 for krow in range(start, start + NA_KROWS)], axis=-1))
        pats.append(jnp.concatenate(rows, axis=1))
    return jnp.stack(pats, axis=1)


def _na_kernel(q_ref, k0_ref, k1_ref, k2_ref, v0_ref, v1_ref, v2_ref, kc_ref, vc_ref,
               tab_ref, o_ref):
    i = pl.program_id(1)

    def head(ref, hh):
        return ref[:, hh * NA_DH:(hh + 1) * NA_DH]

    def ctx_scores(hh):
        return lax.dot_general(head(q_ref, hh), head(kc_ref, hh), _NT_DIMS,
                               preferred_element_type=F32) * NA_SCALE

    @pl.when(i < ROWS // NA_QROWS)
    def _():
        for hh in range(NA_HB):
            q = head(q_ref, hh)
            s_c = ctx_scores(hh)
            m = jnp.max(s_c, axis=-1, keepdims=True)
            s_w = []
            for d, k_ref in enumerate((k0_ref, k1_ref, k2_ref)):
                s = lax.dot_general(q, head(k_ref, hh), _NT_DIMS,
                                    preferred_element_type=F32) * NA_SCALE
                s = s + tab_ref[hh, :, d * NA_QT:(d + 1) * NA_QT]
                s_w.append(s)
                m = jnp.maximum(m, jnp.max(s, axis=-1, keepdims=True))
            p_c = jnp.exp(s_c - m)
            l = jnp.sum(p_c, axis=-1, keepdims=True)
            o = jnp.dot(p_c.astype(BF16), head(vc_ref, hh), preferred_element_type=F32)
            for s, v_ref in zip(s_w, (v0_ref, v1_ref, v2_ref)):
                p = jnp.exp(s - m)
                l = l + jnp.sum(p, axis=-1, keepdims=True)
                o = o + jnp.dot(p.astype(BF16), head(v_ref, hh), preferred_element_type=F32)
            o_ref[:, hh * NA_DH:(hh + 1) * NA_DH] = (o / l).astype(o_ref.dtype)

    @pl.when(i == ROWS // NA_QROWS)
    def _():
        for hh in range(NA_HB):
            s_c = ctx_scores(hh)
            p_c = jnp.exp(s_c - jnp.max(s_c, axis=-1, keepdims=True))
            l = jnp.sum(p_c, axis=-1, keepdims=True)
            o = jnp.dot(p_c.astype(BF16), head(vc_ref, hh), preferred_element_type=F32)
            o_ref[:, hh * NA_DH:(hh + 1) * NA_DH] = (o / l).astype(o_ref.dtype)


def _na_attention(qkv, table):
    ng = ROWS // NA_QROWS
    blk = S // NA_QT
    ctx0 = NLAT // NA_QT

    def qrow(h, i, b):
        return jnp.where(i < ng, b * blk + i, ctx0 + b)

    nhb = NA_H // NA_HB

    def krow(d):
        return lambda h, i, b: (b * blk + jnp.clip(i - 1, 0, blk - 3) + d, nhb + h)

    def vrow(d):
        return lambda h, i, b: (b * blk + jnp.clip(i - 1, 0, blk - 3) + d, 2 * nhb + h)

    def pat(h, i, b):
        return (h, jnp.where(i == 0, 0, jnp.where(i >= ng - 1, 2, 1)), 0, 0)

    tile = (NA_QT, NA_HB * NA_DH)
    in_specs = ([pl.BlockSpec(tile, lambda h, i, b: (qrow(h, i, b), h))]
                + [pl.BlockSpec(tile, krow(d)) for d in range(3)]
                + [pl.BlockSpec(tile, vrow(d)) for d in range(3)]
                + [pl.BlockSpec(tile, lambda h, i, b: (ctx0 + b, nhb + h)),
                   pl.BlockSpec(tile, lambda h, i, b: (ctx0 + b, 2 * nhb + h)),
                   pl.BlockSpec((NA_HB, None, NA_QT, NA_KT), pat)])
    return pl.pallas_call(
        _na_kernel,
        grid=(nhb, ng + 1, B),
        in_specs=in_specs,
        out_specs=pl.BlockSpec(tile, lambda h, i, b: (qrow(h, i, b), h)),
        out_shape=jax.ShapeDtypeStruct((NT, NA_W), BF16),
        compiler_params=_params("parallel", "parallel", "parallel"),
        name="na_attention",
    )(*([qkv] * 9), table)


def _s5_matrices(lam_re, lam_im, log_dt, b_re, b_im, c_re, c_im):
    lam_re, lam_im = lam_re.astype(F32), lam_im.astype(F32)
    b_re, b_im = b_re.astype(F32), b_im.astype(F32)
    c_re, c_im = c_re.astype(F32), c_im.astype(F32)
    dt = jnp.exp(log_dt.astype(F32))[..., None]
    mag = jnp.exp(lam_re * dt)
    a_re = mag * jnp.cos(lam_im * dt)
    a_im = mag * jnp.sin(lam_im * dt)
    den = lam_re * lam_re + lam_im * lam_im
    f_re = ((a_re - 1.0) * lam_re + a_im * lam_im) / den
    f_im = (a_im * lam_re - (a_re - 1.0) * lam_im) / den
    bb_re = f_re[..., None] * b_re - f_im[..., None] * b_im
    bb_im = f_re[..., None] * b_im + f_im[..., None] * b_re

    pr, pi = [jnp.ones_like(a_re)], [jnp.zeros_like(a_im)]
    for _ in range(S5_LC):
        pr.append(pr[-1] * a_re - pi[-1] * a_im)
        pi.append(pr[-2] * a_im + pi[-1] * a_re)
    pw_re = jnp.stack(pr)
    pw_im = jnp.stack(pi)

    ab_re = pw_re[..., None] * bb_re[None] - pw_im[..., None] * bb_im[None]
    ab_im = pw_re[..., None] * bb_im[None] + pw_im[..., None] * bb_re[None]
    def lag_minor(z, d):
        return z[:S5_LC, d].transpose(1, 2, 0, 3).reshape(S5_G, 1, S5_P, S5_CW)
    def kernels(d):
        k = jnp.sum(c_re[d][..., None] * lag_minor(ab_re, d)
                    - c_im[d][..., None] * lag_minor(ab_im, d), axis=2)
        return k.reshape(S5_G, S5_CG, S5_LC, S5_CG).transpose(0, 3, 2, 1)
    zpad = jnp.zeros((S5_G, S5_CG, S5_LC, S5_CG), F32)
    kf = jnp.concatenate([zpad, kernels(0)], axis=2)
    kb = jnp.concatenate([jnp.flip(kernels(1), axis=2), zpad], axis=2)
    tc = jnp.stack([kf[:, :, S5_LC - s:2 * S5_LC - s] + kb[:, :, S5_LC - 1 - s:2 * S5_LC - 1 - s]
                    for s in range(S5_LC)], axis=1).reshape(S5_G, S5_CW, S5_CW)

    def st(arr, d, flip):
        z = arr[:S5_LC, d]
        z = jnp.flip(z, axis=0) if flip else z
        return z.transpose(1, 0, 3, 2).reshape(S5_G, S5_CW, S5_P)
    et = jnp.concatenate([st(ab_re, 0, True), st(ab_re, 1, False),
                          st(ab_im, 0, True), st(ab_im, 1, False)], axis=-1)

    def rd(d, flip):
        pr_ = pw_re[1:, d]
        pi_ = pw_im[1:, d]
        if flip:
            pr_, pi_ = jnp.flip(pr_, axis=0), jnp.flip(pi_, axis=0)
        cr = c_re[d][None] * pr_[:, :, None, :] - c_im[d][None] * pi_[:, :, None, :]
        ci = c_re[d][None] * pi_[:, :, None, :] + c_im[d][None] * pr_[:, :, None, :]
        to = lambda z: z.transpose(1, 3, 0, 2).reshape(S5_G, S5_P, S5_CW)
        return to(cr), to(-ci)
    fr, fi = rd(0, False)
    br, bi = rd(1, True)
    z = jnp.zeros_like(fr)
    ft = jnp.concatenate([fr, z, fi, z, z, br, z, bi], axis=1)

    a16_re = jnp.concatenate([pw_re[S5_LC, 0], pw_re[S5_LC, 1]], axis=-1)[:, None, :]
    a16_im = jnp.concatenate([pw_im[S5_LC, 0], pw_im[S5_LC, 1]], axis=-1)[:, None, :]
    return tc.astype(BF16), et.astype(BF16), ft.astype(BF16), a16_re, a16_im


def _uproj_kernel(wt_ref, h_ref, o_ref):
    acc = lax.dot_general(wt_ref[...], h_ref[...], _NT_DIMS, preferred_element_type=F32)
    o_ref[...] = acc.reshape(S5_G, S5_CG, S5_NT).astype(o_ref.dtype)


def _s5_uproj(h_tl, w_u_t):
    return pl.pallas_call(
        _uproj_kernel,
        grid=(S5_LC, S5_N // S5_NT),
        in_specs=[pl.BlockSpec((S5_W, D), lambda t, n: (0, 0)),
                  pl.BlockSpec((None, S5_NT, D), lambda t, n: (t, n, 0))],
        out_specs=pl.BlockSpec((S5_G, None, S5_CG, S5_NT), lambda t, n: (0, t, 0, n)),
        out_shape=jax.ShapeDtypeStruct((S5_G, S5_LC, S5_CG, S5_N), BF16),
        compiler_params=_params("parallel", "parallel"),
        name="s5_uproj",
    )(w_u_t, h_tl)


def _s5_chunk_rows(kind, k):
    if kind == "c":
        return pl.ds(B * S5_NLAT + k, B, stride=S5_NCTX)
    return pl.ds(k, B, stride=S5_NLAT)


def _s5_kernel(ut_ref, tc_ref, et_ref, ft_ref, ar_ref, ai_ref, d_ref, o_ref,
               he_re_ref, he_im_ref, hpf_re_ref, hpf_im_ref, hpb_re_ref, hpb_im_ref):
    sw = 2 * S5_P
    utf = ut_ref[...].reshape(S5_CW, S5_N).astype(F32)
    un = utf.T.astype(BF16)
    y = jnp.dot(un, tc_ref[...], preferred_element_type=F32)
    he = jnp.dot(un, et_ref[...], preferred_element_type=F32)
    he_re_ref[...] = he[:, :sw]
    he_im_ref[...] = he[:, sw:]

    ar = ar_ref[...]
    ai = ai_ref[...]
    is_fwd = lax.broadcasted_iota(jnp.int32, (B, 2 * S5_P), 1) < S5_P
    h_re = jnp.zeros((B, 2 * S5_P), F32)
    h_im = jnp.zeros((B, 2 * S5_P), F32)
    fwd = [("c", k) for k in range(S5_NCTX)] + [("l", k) for k in range(S5_NLAT)]
    bwd = ([("c", k) for k in range(S5_NCTX - 1, -1, -1)]
           + [("l", k) for k in range(S5_NLAT - 1, -1, -1)])
    for cf, cb in zip(fwd, bwd):
        rf = _s5_chunk_rows(*cf)
        rb = _s5_chunk_rows(*cb)
        hpf_re_ref[rf, :] = h_re
        hpf_im_ref[rf, :] = h_im
        hpb_re_ref[rb, :] = h_re
        hpb_im_ref[rb, :] = h_im
        e_re = jnp.where(is_fwd, he_re_ref[rf, :], he_re_ref[rb, :])
        e_im = jnp.where(is_fwd, he_im_ref[rf, :], he_im_ref[rb, :])
        n_re = ar * h_re - ai * h_im + e_re
        n_im = ar * h_im + ai * h_re + e_im
        h_re, h_im = n_re, n_im

    hp = jnp.concatenate([hpf_re_ref[...], hpf_im_ref[...], hpb_re_ref[...], hpb_im_ref[...]],
                         axis=1).astype(BF16)
    y = y + jnp.dot(hp, ft_ref[...], preferred_element_type=F32)
    g = y.T + d_ref[...] * utf
    gl = 0.5 * g * (1.0 + lax.erf(g * (0.5 ** 0.5)))
    o_ref[...] = gl.astype(o_ref.dtype).reshape(S5_LC, S5_CG, S5_N)


def _s5_scan(ut, mats, d_col):
    tc, et, ft, a_re, a_im = mats
    mat = pl.BlockSpec((None, S5_CW, S5_CW), lambda g: (g, 0, 0))
    vec = pl.BlockSpec((None, 1, 2 * S5_P), lambda g: (g, 0, 0))
    io = pl.BlockSpec((None, S5_LC, S5_CG, S5_N), lambda g: (g, 0, 0, 0))
    return pl.pallas_call(
        _s5_kernel,
        grid=(S5_G,),
        in_specs=[io, mat, mat,
                  pl.BlockSpec((None, 2 * S5_CW, S5_CW), lambda g: (g, 0, 0)), vec, vec,
                  pl.BlockSpec((None, S5_CW, 1), lambda g: (g, 0, 0))],
        out_specs=io,
        out_shape=jax.ShapeDtypeStruct((S5_G, S5_LC, S5_CG, S5_N), BF16),
        scratch_shapes=[pltpu.VMEM((S5_N, 2 * S5_P), F32)] * 6,
        compiler_params=_params("parallel"),
        name="s5_scan",
    )(ut, tc, et, ft, a_re, a_im, d_col)


def _glu_kernel(gl_ref, w_ref, b_ref, o_ref):
    gl = gl_ref[...].reshape(S5_W, S5_NT)
    z = jnp.dot(w_ref[...], gl, preferred_element_type=F32) + b_ref[...]
    s = gl.astype(F32) * jax.nn.sigmoid(z)
    o_ref[...] = s.T.astype(o_ref.dtype)


def _s5_glu(glt, w_t, b_col):
    out = pl.pallas_call(
        _glu_kernel,
        grid=(S5_LC, S5_N // S5_NT),
        in_specs=[pl.BlockSpec((S5_G, None, S5_CG, S5_NT), lambda t, n: (0, t, 0, n)),
                  pl.BlockSpec((S5_W, S5_W), lambda t, n: (0, 0)),
                  pl.BlockSpec((S5_W, 1), lambda t, n: (0, 0))],
        out_specs=pl.BlockSpec((S5_NT, S5_W), lambda t, n: (n, t)),
        out_shape=jax.ShapeDtypeStruct((S5_N, S5_LC * S5_W), BF16),
        compiler_params=_params("parallel", "parallel"),
        name="s5_glu",
    )(glt, w_t, b_col)
    return out.reshape(NT, S5_W)


def _rope_tables():
    half = GLA_DK // 2
    freqs = ROPE_BASE ** (-np.arange(0, half, 2, dtype=np.float32) / half)
    out = []
    for n in (ROWS, GRID_W):
        ang = np.arange(n, dtype=np.float32)[:, None] * freqs[None, :]
        c, s = np.cos(ang), np.sin(ang)
        out += [np.concatenate([c, c], axis=-1), np.concatenate([-s, s], axis=-1)]
    return tuple(jnp.asarray(t, F32) for t in out)


def _rope(x, rcos, rsin, ccos, csin):
    x0 = x[:, :128]
    x1 = x[:, 128:]
    return jnp.concatenate([x0 * rcos + pltpu.roll(x0, 64, axis=1) * rsin,
                            x1 * ccos + pltpu.roll(x1, 64, axis=1) * csin], axis=-1)


def _cumsum_rows(x, reverse):
    row = lax.broadcasted_iota(jnp.int32, x.shape, 0)
    s = 1
    while s < GLA_C:
        if reverse:
            x = x + jnp.where(row < GLA_C - s, pltpu.roll(x, GLA_C - s, axis=0), 0.0)
        else:
            x = x + jnp.where(row >= s, pltpu.roll(x, s, axis=0), 0.0)
        s *= 2
    return x


def _log_sigmoid(x):
    return jnp.minimum(x, 0.0) - jnp.log1p(jnp.exp(-jnp.abs(x)))


def _chunk_rows(c):
    if isinstance(c, int):
        return pl.ds(c * GLA_C, GLA_C)
    return pl.ds(pl.multiple_of(c * GLA_C, GLA_C), GLA_C)


def _gla_kernel(ql_ref, kl_ref, vl_ref, gl_ref, qc_ref, kc_ref, vc_ref, al_ref, ac_ref,
                waf_ref, wab_ref, baf_ref, bab_ref, rcos_ref, rsin_ref, ccos_ref, csin_ref,
                ng_ref, o_ref,
                qi_f, ki_f, ke_f, dec_f, st_f, qi_b, ki_b, ke_b, dec_b, st_b, acc_ref):
    n_ctx = L // GLA_C
    n_lat = S // GLA_C
    qscale = GLA_DK ** -0.5
    fwd = (waf_ref, baf_ref, qi_f, ki_f, ke_f, dec_f, False)
    bwd = (wab_ref, bab_ref, qi_b, ki_b, ke_b, dec_b, True)

    ii = lax.broadcasted_iota(jnp.int32, (GLA_C, GLA_C), 0)
    jj = lax.broadcasted_iota(jnp.int32, (GLA_C, GLA_C), 1)

    def prepare(direction, q, k, a, c):
        wa_ref, ba_ref, qi, ki, ke, dec, reverse = direction
        dst = _chunk_rows(c)
        la = _log_sigmoid(jnp.dot(a, wa_ref[...], preferred_element_type=F32)
                          + ba_ref[...]) / GLA_TAU
        bc = _cumsum_rows(la, reverse)
        b_last = bc[0:1, :] if reverse else bc[GLA_C - 1:GLA_C, :]
        qi[dst, :] = (q * jnp.exp(bc)).astype(BF16)
        ki[dst, :] = (k * jnp.exp(-bc)).astype(BF16)
        ke[dst, :] = (k * jnp.exp(b_last - bc)).astype(BF16)
        dec[pl.ds(c, 1), :] = jnp.exp(b_last)

    def prepare_latent(direction, c):
        r = _chunk_rows(c)
        tabs = (rcos_ref[pl.ds(c, 1), :], rsin_ref[pl.ds(c, 1), :], ccos_ref[...], csin_ref[...])
        q = _rope(ql_ref[r, :].astype(F32) * qscale, *tabs)
        k = _rope(kl_ref[r, :].astype(F32), *tabs)
        prepare(direction, q, k, al_ref[r, :], n_ctx + c)

    def prep_ctx(c, carry):
        r = _chunk_rows(c)
        q = qc_ref[r, :].astype(F32) * qscale
        k = kc_ref[r, :].astype(F32)
        prepare(fwd, q, k, ac_ref[r, :], c)
        prepare(bwd, q, k, ac_ref[r, :], c)
        return carry
    lax.fori_loop(0, n_ctx, prep_ctx, 0)
    prepare_latent(fwd, 0)
    prepare_latent(bwd, n_lat - 1)

    def advance(direction, st_ref, c, v, want_out):
        _, _, qi, ki, ke, dec, reverse = direction
        r = _chunk_rows(c)
        st = st_ref[...]
        o = None
        if want_out:
            q_in = qi[r, :]
            att = lax.dot_general(q_in, ki[r, :], _NT_DIMS, preferred_element_type=F32)
            att = jnp.where((ii <= jj) if reverse else (ii >= jj), att, 0.0)
            o = (jnp.dot(att.astype(BF16), v, preferred_element_type=F32)
                 + lax.dot_general(q_in, st.astype(BF16), _NT_DIMS, preferred_element_type=F32))
        vt = v.astype(F32).T.astype(BF16)
        st_ref[...] = dec[pl.ds(c, 1), :] * st + jnp.dot(vt, ke[r, :],
                                                          preferred_element_type=F32)
        return o

    st_f[...] = jnp.zeros_like(st_f)
    st_b[...] = jnp.zeros_like(st_b)

    def ctx_pair(j, carry):
        cb = n_ctx - 1 - j
        advance(fwd, st_f, j, vc_ref[_chunk_rows(j), :], False)
        advance(bwd, st_b, cb, vc_ref[_chunk_rows(cb), :], False)
        return carry
    lax.fori_loop(0, n_ctx, ctx_pair, 0)

    def lat_pair(j, accumulate):
        cb = n_lat - 1 - j
        rf = _chunk_rows(j)
        rb = _chunk_rows(cb)
        o_f = advance(fwd, st_f, n_ctx + j, vl_ref[rf, :], True)
        o_b = advance(bwd, st_b, n_ctx + cb, vl_ref[rb, :], True)
        if accumulate:
            acc_ref[rf, :] += o_f
            acc_ref[rb, :] += o_b
        else:
            acc_ref[rf, :] = o_f
            acc_ref[rb, :] = o_b
        prepare_latent(fwd, jnp.minimum(j + 1, n_lat - 1))
        prepare_latent(bwd, jnp.maximum(cb - 1, 0))

    def lat_first(j, carry):
        lat_pair(j, False)
        return carry
    lax.fori_loop(0, n_lat // 2, lat_first, 0, unroll=2)

    def lat_second(j, carry):
        lat_pair(j, True)
        return carry
    lax.fori_loop(n_lat // 2, n_lat, lat_second, 0, unroll=2)

    tr = 256

    def fin(t, carry):
        r = pl.ds(pl.multiple_of(t * tr, tr), tr)
        o = acc_ref[r, :]
        ms = jnp.mean(o * o, axis=-1, keepdims=True)
        g = gl_ref[r, :].astype(F32)
        o_ref[r, :] = (o * lax.rsqrt(ms + EPS) * ng_ref[...]
                       * (g * jax.nn.sigmoid(g))).astype(o_ref.dtype)
        return carry
    lax.fori_loop(0, S // tr, fin, 0)


def _gla(qkvg, acode, wa, ba, rope, norm_g):
    ctx0 = NLAT // L
    kq = GLA_QK // GLA_DK
    half = GLA_DK // 2
    n_chunks = (L + S) // GLA_C
    per_dir = [pltpu.VMEM((L + S, GLA_DK), BF16)] * 3 + [pltpu.VMEM((n_chunks, GLA_DK), F32),
                                                         pltpu.VMEM((GLA_DV, GLA_DK), F32)]
    in_specs = [
        pl.BlockSpec((S, GLA_DK), lambda b, h: (b, h)),
        pl.BlockSpec((S, GLA_DK), lambda b, h: (b, kq + h)),
        pl.BlockSpec((S, GLA_DV), lambda b, h: (b, kq + h)),
        pl.BlockSpec((S, GLA_DV), lambda b, h: (b, 2 * kq + h)),
        pl.BlockSpec((L, GLA_DK), lambda b, h: (ctx0 + b, h)),
        pl.BlockSpec((L, GLA_DK), lambda b, h: (ctx0 + b, kq + h)),
        pl.BlockSpec((L, GLA_DV), lambda b, h: (ctx0 + b, kq + h)),
        pl.BlockSpec((S, 128), lambda b, h: (b, 0)),
        pl.BlockSpec((L, 128), lambda b, h: (ctx0 + b, 0)),
        pl.BlockSpec((128, GLA_DK), lambda b, h: (0, h)),
        pl.BlockSpec((128, GLA_DK), lambda b, h: (0, kq + h)),
        pl.BlockSpec((1, GLA_DK), lambda b, h: (0, h)),
        pl.BlockSpec((1, GLA_DK), lambda b, h: (0, kq + h)),
        pl.BlockSpec((ROWS, half), lambda b, h: (0, 0)),
        pl.BlockSpec((ROWS, half), lambda b, h: (0, 0)),
        pl.BlockSpec((GRID_W, half), lambda b, h: (0, 0)),
        pl.BlockSpec((GRID_W, half), lambda b, h: (0, 0)),
        pl.BlockSpec((1, GLA_DV), lambda b, h: (0, 0)),
    ]
    return pl.pallas_call(
        _gla_kernel,
        grid=(B, GLA_H),
        in_specs=in_specs,
        out_specs=pl.BlockSpec((S, GLA_DV), lambda b, h: (b, h)),
        out_shape=jax.ShapeDtypeStruct((NLAT, GLA_VW), BF16),
        scratch_shapes=per_dir + per_dir + [pltpu.VMEM((S, GLA_DV), F32)],
        compiler_params=_params("parallel", "parallel"),
        name="gla",
    )(qkvg, qkvg, qkvg, qkvg, qkvg, qkvg, qkvg, acode, acode, wa, wa, ba, ba, *rope,
      norm_g.reshape(1, GLA_DV))


def kernel(x, c, ctx, c_ctx, ada_w, ada_b, norm1_g, norm2_g, mlp_w1, mlp_w2, final_g, ab_w_in, ab_w_out, na_rel_bias, s5_lambda_re, s5_lambda_im, s5_log_dt, s5_b_re, s5_b_im, s5_c_re, s5_c_im, s5_d, s5_glu_w, s5_glu_b, gla_w_in, gla_w_a2, gla_b_a, gla_norm_g, gla_w_out):
    xs = (x.astype(F32).reshape(NLAT, D), ctx.astype(F32).reshape(NCTX, D))
    cvec = jnp.zeros((16, D), F32).at[:B].set(c.astype(F32)).at[B].set(c_ctx.astype(F32))
    mods = _ada_mod(cvec, ada_w, ada_b).reshape(2, 16, 1, 6 * D)
    bf = lambda w: w.astype(BF16)

    mod = mods[0]
    h, h_tl = _normmod(xs, mod, norm1_g[0], 0, 1, NT)
    w_in = ab_w_in[0]
    qkv = _mm(h, ab_w_in, 0, 3 * NA_W, NT)
    att = _na_attention(qkv, _na_tables(na_rel_bias[0]))
    ut = _s5_uproj(h_tl, bf(w_in[:, 3 * NA_W:].T))
    mats = _s5_matrices(s5_lambda_re[0], s5_lambda_im[0], s5_log_dt[0], s5_b_re[0], s5_b_im[0],
                        s5_c_re[0], s5_c_im[0])
    d_col = jnp.tile(s5_d[0].astype(F32).reshape(S5_G, 1, S5_CG), (1, S5_LC, 1)).reshape(S5_G, S5_CW, 1)
    glt = _s5_scan(ut, mats, d_col)
    s5 = _s5_glu(glt, bf(s5_glu_w[0].T), s5_glu_b[0].astype(F32).reshape(S5_W, 1))
    w2 = bf(mlp_w2)
    xs, h = _mm_res([att, s5], bf(ab_w_out), 0, xs, mod, 2, NT, nxt=(mod, norm2_g[0], 3, 4))
    hid = _mm(h, mlp_w1, 0, MLP_H, NT, relu2=True)
    xs, h = _mm_res([hid], w2, 0, xs, mod, 5, NT, nxt=(mods[1], norm1_g[1], 0, 1))

    mod = mods[1]
    w_in = gla_w_in[0]
    qkvg = _mm(h, gla_w_in, 0, GLA_MAIN, NT)
    w_code = jnp.zeros((1, D, 128), F32).at[0, :, :2 * GLA_RANK].set(w_in[:, GLA_MAIN:])
    acode = _mm(h, w_code, 0, 128, NT)
    wa = (jnp.zeros((128, 2 * GLA_QK), F32)
          .at[:GLA_RANK, :GLA_QK].set(gla_w_a2[0, 0])
          .at[GLA_RANK:2 * GLA_RANK, GLA_QK:].set(gla_w_a2[0, 1]))
    ba = gla_b_a[0].astype(F32).reshape(1, 2 * GLA_QK)
    og = _gla(qkvg, acode, bf(wa), ba, _rope_tables(), gla_norm_g[0].astype(F32))
    xl, h = _mm_res([og], bf(gla_w_out), 0, xs, mod, 2, NLAT, nxt=(mod, norm2_g[1], 3, 4))
    hid = _mm(h, mlp_w1, 1, MLP_H, NLAT, relu2=True)
    out = _mm_res([hid], w2, 1, xl, mod, 5, NLAT, final_g=final_g.astype(F32))
    return out.reshape(B, S, D).astype(x.dtype)
```

```python
import functools
import math

import numpy as np
import jax
import jax.numpy as jnp
from jax import lax
from jax.experimental import pallas as pl
from jax.experimental.pallas import tpu as pltpu

F32 = jnp.float32
BF16 = jnp.bfloat16

D = 2048
B = 8
S = 2048
L = 256
GRID_W = 64
ROWS = S // GRID_W
NLAT = B * S
NCTX = B * L
NT = NLAT + NCTX
MLP_H = 4 * D
EPS = 1e-6
NEG_INF = -1e30

NA_H = 8
NA_DH = 128
NA_W = NA_H * NA_DH
NA_SCALE = NA_DH ** -0.5
NA_HB = 4
NA_QROWS = 4
NA_KROWS = 12
NA_QT = NA_QROWS * GRID_W
NA_KT = NA_KROWS * GRID_W

S5_W = D // 2
S5_CG = 16
S5_G = S5_W // S5_CG
S5_P = 64
S5_LC = 16
S5_CW = S5_LC * S5_CG
S5_NLAT = S // S5_LC
S5_NCTX = L // S5_LC
S5_N = B * (S5_NLAT + S5_NCTX)
S5_NT = 384

GLA_H = 4
GLA_DK = 256
GLA_DV = 512
GLA_QK = GLA_H * GLA_DK
GLA_VW = GLA_H * GLA_DV
GLA_RANK = 16
GLA_TAU = 16.0
GLA_C = 64
GLA_MAIN = 2 * GLA_QK + 2 * GLA_VW
ROPE_BASE = 10000.0

MM_RES_NSPLIT = 4

VMEM_LIMIT = 56 * 1024 * 1024

_NT_DIMS = (((1,), (1,)), ((), ()))


def _params(*sem):
    return pltpu.CompilerParams(dimension_semantics=sem, vmem_limit_bytes=VMEM_LIMIT)


def _mod_row(i, tm):
    return jnp.minimum((i * tm) // S, B)


def _ada_kernel(c_ref, w_ref, b_ref, o_ref):
    c = c_ref[...]
    s = c * jax.nn.sigmoid(c)
    o_ref[...] = jnp.dot(s.astype(BF16), w_ref[...].astype(BF16),
                         preferred_element_type=F32) + b_ref[...]


def _ada_mod(cvec, ada_w, ada_b):
    depth = ada_w.shape[0]
    tn = 512
    return pl.pallas_call(
        _ada_kernel,
        grid=(depth, 6 * D // tn),
        in_specs=[pl.BlockSpec((16, D), lambda l, j: (0, 0)),
                  pl.BlockSpec((None, D, tn), lambda l, j: (l, 0, j)),
                  pl.BlockSpec((None, 1, tn), lambda l, j: (l, 0, j))],
        out_specs=pl.BlockSpec((None, 16, tn), lambda l, j: (l, 0, j)),
        out_shape=jax.ShapeDtypeStruct((depth, 16, 6 * D), F32),
        compiler_params=_params("parallel", "parallel"),
        name="ada_mod",
    )(cvec, ada_w, ada_b.reshape(depth, 1, 6 * D))


def _stream_specs(xs, tm, two_axes):
    if not isinstance(xs, tuple):
        imap = (lambda i, k: (i, 0)) if two_axes else (lambda i: (i, 0))
        return 0, [xs], [pl.BlockSpec((tm, D), imap)]
    nl = xs[0].shape[0] // tm
    if two_axes:
        maps = [lambda i, k: (jnp.minimum(i, nl - 1), 0), lambda i, k: (jnp.maximum(i - nl, 0), 0)]
    else:
        maps = [lambda i: (jnp.minimum(i, nl - 1), 0), lambda i: (jnp.maximum(i - nl, 0), 0)]
    return nl, list(xs), [pl.BlockSpec((tm, D), m) for m in maps]


def _stream_tile(x_refs, n_lat_tiles, cols=slice(None)):
    if len(x_refs) == 1:
        return x_refs[0][:, cols]
    return jnp.where(pl.program_id(0) < n_lat_tiles, x_refs[0][:, cols], x_refs[1][:, cols])


def _normmod_kernel(*refs, n_x, n_lat_tiles):
    x_refs = refs[:n_x]
    sh_ref, sc_ref, g_ref, o_ref, oc_ref = refs[n_x:]
    x = _stream_tile(x_refs, n_lat_tiles)
    ms = jnp.mean(x * x, axis=-1, keepdims=True)
    h = x * lax.rsqrt(ms + EPS) * g_ref[...]
    h = h * (1.0 + sc_ref[...]) + sh_ref[...]
    o_ref[...] = h.astype(o_ref.dtype)
    h3 = h.reshape(h.shape[0] // S5_LC, S5_LC, D)
    for tl in range(S5_LC):
        oc_ref[tl] = h3[:, tl, :].astype(oc_ref.dtype)


def _normmod(xs, mod, g, shift_idx, scale_idx, rows):
    tm = 512
    nl, x_args, x_specs = _stream_specs(xs, tm, False)
    return pl.pallas_call(
        functools.partial(_normmod_kernel, n_x=len(x_args), n_lat_tiles=nl),
        grid=(rows // tm,),
        in_specs=x_specs + [
            pl.BlockSpec((None, 1, D), lambda i: (_mod_row(i, tm), 0, shift_idx)),
            pl.BlockSpec((None, 1, D), lambda i: (_mod_row(i, tm), 0, scale_idx)),
            pl.BlockSpec((1, D), lambda i: (0, 0))],
        out_specs=[pl.BlockSpec((tm, D), lambda i: (i, 0)),
                   pl.BlockSpec((S5_LC, tm // S5_LC, D), lambda i: (0, i, 0))],
        out_shape=[jax.ShapeDtypeStruct((rows, D), BF16),
                   jax.ShapeDtypeStruct((S5_LC, rows // S5_LC, D), BF16)],
        compiler_params=_params("parallel"),
        name="normmod",
    )(*x_args, mod, mod, g.reshape(1, D))


def _mm_kernel(a_ref, w_ref, o_ref, wb_ref, *, relu2):
    @pl.when(pl.program_id(1) == 0)
    def _():
        wb_ref[...] = w_ref[...].astype(BF16)

    acc = jnp.dot(a_ref[...], wb_ref[...], preferred_element_type=F32)
    if relu2:
        acc = jnp.square(jnp.maximum(acc, 0.0))
    o_ref[...] = acc.astype(o_ref.dtype)


def _mm(a, w, layer, n, rows, *, relu2=False, tm=2048, tn=1024):
    k = a.shape[1]
    tn = min(tn, n)
    return pl.pallas_call(
        functools.partial(_mm_kernel, relu2=relu2),
        grid=(n // tn, rows // tm),
        in_specs=[pl.BlockSpec((tm, k), lambda j, i: (i, 0)),
                  pl.BlockSpec((None, k, tn), lambda j, i: (layer, 0, j))],
        out_specs=pl.BlockSpec((tm, tn), lambda j, i: (i, j)),
        out_shape=jax.ShapeDtypeStruct((rows, n), BF16),
        scratch_shapes=[pltpu.VMEM((k, tn), BF16)],
        compiler_params=_params("parallel", "arbitrary"),
        name="mm_relu2" if relu2 else "mm",
    )(a, w)


def _mm_res_kernel(*refs, n_lhs, n_res, n_lat_tiles, nk, final_norm, next_norm):
    a_refs = refs[:n_lhs]
    w_refs = refs[n_lhs:2 * n_lhs]
    res_refs = refs[2 * n_lhs:2 * n_lhs + n_res]
    gate_ref = refs[2 * n_lhs + n_res]
    pos = 2 * n_lhs + n_res + 1
    n_extra = 1 if final_norm else (3 if next_norm else 0)
    extra = refs[pos:pos + n_extra]
    o_ref = refs[pos + n_extra]
    h_ref = refs[pos + n_extra + 1] if next_norm else None

    def rms(y, g_ref):
        ms = jnp.mean(y * y, axis=-1, keepdims=True)
        return y * lax.rsqrt(ms + EPS) * g_ref[...]

    def emit_next(y):
        sh_ref, sc_ref, g_ref = extra
        h_ref[...] = (rms(y, g_ref) * (1.0 + sc_ref[...]) + sh_ref[...]).astype(h_ref.dtype)

    kk = pl.program_id(1)
    if nk > 1:
        @pl.when(kk == 0)
        def _():
            o_ref[...] = _stream_tile(res_refs, n_lat_tiles)

    wn = D // MM_RES_NSPLIT
    for cc in range(MM_RES_NSPLIT):
        cols = slice(cc * wn, (cc + 1) * wn)
        part = None
        for a_ref, w_ref in zip(a_refs, w_refs):
            d = jnp.dot(a_ref[...], w_ref[:, cols], preferred_element_type=F32)
            part = d if part is None else part + d
        part = gate_ref[:, cols] * part
        if nk == 1:
            o_ref[:, cols] = _stream_tile(res_refs, n_lat_tiles, cols) + part
        else:
            o_ref[:, cols] += part

    def epilogue():
        if final_norm:
            o_ref[...] = rms(o_ref[...], extra[0])
        elif next_norm:
            emit_next(o_ref[...])

    if nk == 1:
        epilogue()
    elif final_norm or next_norm:
        pl.when(kk == nk - 1)(epilogue)


def _mm_res(a_list, w, layer, resid, mod, gate_idx, rows, *, final_g=None, nxt=None, tm=512):
    n_lhs = len(a_list)
    kdim = a_list[0].shape[1]
    tk = min(kdim, 2048)
    nk = kdim // tk
    nl, res_args, res_specs = _stream_specs(resid, tm, True)

    def mod_spec(idx):
        return pl.BlockSpec((None, 1, D), lambda i, k: (_mod_row(i, tm), 0, idx))

    def w_spec(j):
        return pl.BlockSpec((None, tk, D), lambda i, k: (layer, j * nk + k, 0))

    row_spec = pl.BlockSpec((tm, D), lambda i, k: (i, 0))
    vec_spec = pl.BlockSpec((1, D), lambda i, k: (0, 0))
    in_specs = ([pl.BlockSpec((tm, tk), lambda i, k: (i, k)) for _ in a_list]
                + [w_spec(j) for j in range(n_lhs)] + res_specs + [mod_spec(gate_idx)])
    args = list(a_list) + [w] * n_lhs + res_args + [mod]
    out_specs, out_shape = row_spec, jax.ShapeDtypeStruct((rows, D), F32)
    if final_g is not None:
        in_specs.append(vec_spec)
        args.append(final_g.reshape(1, D))
    elif nxt is not None:
        mod_n, g_n, shift_idx, scale_idx = nxt
        in_specs += [mod_spec(shift_idx), mod_spec(scale_idx), vec_spec]
        args += [mod_n, mod_n, g_n.reshape(1, D)]
        out_specs = [row_spec, row_spec]
        out_shape = [out_shape, jax.ShapeDtypeStruct((rows, D), BF16)]
    return pl.pallas_call(
        functools.partial(_mm_res_kernel, n_lhs=n_lhs, n_res=len(res_args), n_lat_tiles=nl,
                          nk=nk, final_norm=final_g is not None, next_norm=nxt is not None),
        grid=(rows // tm, nk),
        in_specs=in_specs,
        out_specs=out_specs,
        out_shape=out_shape,
        compiler_params=_params("parallel", "arbitrary"),
        name="mm_res",
    )(*args)


def _na_tables(rel_bias):
    hp = lax.Precision.HIGHEST
    cq = np.arange(GRID_W)[:, None]
    ck = np.arange(GRID_W)[None, :]
    ws = np.clip(cq - 8, 0, GRID_W - 16)
    col_ok = (ck >= ws) & (ck < ws + 16)
    col_hot = ((ck - cq + 15)[..., None] == np.arange(31)) & col_ok[..., None]
    blocks = jnp.einsum('hrj,qkj->hrqk', rel_bias.astype(F32), col_hot.astype(np.float32),
                        precision=hp)
    blocks = blocks + np.where(col_ok, 0.0, NEG_INF).astype(np.float32)
    masked = jnp.full((NA_H, GRID_W, GRID_W), NEG_INF, F32)
    pats = []
    for r0, start in ((0, 0), (NA_QROWS, 0), (ROWS - NA_QROWS, ROWS - NA_KROWS)):
        rows = []
        for a in range(NA_QROWS):
            r = r0 + a
            rs = min(max(r - 4, 0), ROWS - 8)
            rows.append(jnp.concatenate(
                [blocks[:, krow - r + 7] if rs <= krow < rs + 8 else masked
                 for krow in range(start, start + NA_KROWS)], axis=-1))
        pats.append(jnp.concatenate(rows, axis=1))
    return jnp.stack(pats, axis=1)


def _na_kernel(q_ref, k0_ref, k1_ref, k2_ref, v0_ref, v1_ref, v2_ref, kc_ref, vc_ref,
               tab_ref, o_ref):
    i = pl.program_id(1)

    def head(ref, hh):
        return ref[:, hh * NA_DH:(hh + 1) * NA_DH]

    def ctx_scores(hh):
        return lax.dot_general(head(q_ref, hh), head(kc_ref, hh), _NT_DIMS,
                               preferred_element_type=F32) * NA_SCALE

    @pl.when(i < ROWS // NA_QROWS)
    def _():
        for hh in range(NA_HB):
            q = head(q_ref, hh)
            s_c = ctx_scores(hh)
            m = jnp.max(s_c, axis=-1, keepdims=True)
            s_w = []
            for d, k_ref in enumerate((k0_ref, k1_ref, k2_ref)):
                s = lax.dot_general(q, head(k_ref, hh), _NT_DIMS,
                                    preferred_element_type=F32) * NA_SCALE
                s = s + tab_ref[hh, :, d * NA_QT:(d + 1) * NA_QT]
                s_w.append(s)
                m = jnp.maximum(m, jnp.max(s, axis=-1, keepdims=True))
            p_c = jnp.exp(s_c - m)
            l = jnp.sum(p_c, axis=-1, keepdims=True)
            o = jnp.dot(p_c.astype(BF16), head(vc_ref, hh), preferred_element_type=F32)
            for s, v_ref in zip(s_w, (v0_ref, v1_ref, v2_ref)):
                p = jnp.exp(s - m)
                l = l + jnp.sum(p, axis=-1, keepdims=True)
                o = o + jnp.dot(p.astype(BF16), head(v_ref, hh), preferred_element_type=F32)
            o_ref[:, hh * NA_DH:(hh + 1) * NA_DH] = (o / l).astype(o_ref.dtype)

    @pl.when(i == ROWS // NA_QROWS)
    def _():
        for hh in range(NA_HB):
            s_c = ctx_scores(hh)
            p_c = jnp.exp(s_c - jnp.max(s_c, axis=-1, keepdims=True))
            l = jnp.sum(p_c, axis=-1, keepdims=True)
            o = jnp.dot(p_c.astype(BF16), head(vc_ref, hh), preferred_element_type=F32)
            o_ref[:, hh * NA_DH:(hh + 1) * NA_DH] = (o / l).astype(o_ref.dtype)


def _na_attention(qkv, table):
    ng = ROWS // NA_QROWS
    blk = S // NA_QT
    ctx0 = NLAT // NA_QT

    def qrow(h, i, b):
        return jnp.where(i < ng, b * blk + i, ctx0 + b)

    nhb = NA_H // NA_HB

    def krow(d):
        return lambda h, i, b: (b * blk + jnp.clip(i - 1, 0, blk - 3) + d, nhb + h)

    def vrow(d):
        return lambda h, i, b: (b * blk + jnp.clip(i - 1, 0, blk - 3) + d, 2 * nhb + h)

    def pat(h, i, b):
        return (h, jnp.where(i == 0, 0, jnp.where(i >= ng - 1, 2, 1)), 0, 0)

    tile = (NA_QT, NA_HB * NA_DH)
    in_specs = ([pl.BlockSpec(tile, lambda h, i, b: (qrow(h, i, b), h))]
                + [pl.BlockSpec(tile, krow(d)) for d in range(3)]
                + [pl.BlockSpec(tile, vrow(d)) for d in range(3)]
                + [pl.BlockSpec(tile, lambda h, i, b: (ctx0 + b, nhb + h)),
                   pl.BlockSpec(tile, lambda h, i, b: (ctx0 + b, 2 * nhb + h)),
                   pl.BlockSpec((NA_HB, None, NA_QT, NA_KT), pat)])
    return pl.pallas_call(
        _na_kernel,
        grid=(nhb, ng + 1, B),
        in_specs=in_specs,
        out_specs=pl.BlockSpec(tile, lambda h, i, b: (qrow(h, i, b), h)),
        out_shape=jax.ShapeDtypeStruct((NT, NA_W), BF16),
        compiler_params=_params("parallel", "parallel", "parallel"),
        name="na_attention",
    )(*([qkv] * 9), table)


def _s5_matrices(lam_re, lam_im, log_dt, b_re, b_im, c_re, c_im):
    lam_re, lam_im = lam_re.astype(F32), lam_im.astype(F32)
    b_re, b_im = b_re.astype(F32), b_im.astype(F32)
    c_re, c_im = c_re.astype(F32), c_im.astype(F32)
    dt = jnp.exp(log_dt.astype(F32))[..., None]
    mag = jnp.exp(lam_re * dt)
    a_re = mag * jnp.cos(lam_im * dt)
    a_im = mag * jnp.sin(lam_im * dt)
    den = lam_re * lam_re + lam_im * lam_im
    f_re = ((a_re - 1.0) * lam_re + a_im * lam_im) / den
    f_im = (a_im * lam_re - (a_re - 1.0) * lam_im) / den
    bb_re = f_re[..., None] * b_re - f_im[..., None] * b_im
    bb_im = f_re[..., None] * b_im + f_im[..., None] * b_re

    pr, pi = [jnp.ones_like(a_re)], [jnp.zeros_like(a_im)]
    for _ in range(S5_LC):
        pr.append(pr[-1] * a_re - pi[-1] * a_im)
        pi.append(pr[-2] * a_im + pi[-1] * a_re)
    pw_re = jnp.stack(pr)
    pw_im = jnp.stack(pi)

    ab_re = pw_re[..., None] * bb_re[None] - pw_im[..., None] * bb_im[None]
    ab_im = pw_re[..., None] * bb_im[None] + pw_im[..., None] * bb_re[None]
    def lag_minor(z, d):
        return z[:S5_LC, d].transpose(1, 2, 0, 3).reshape(S5_G, 1, S5_P, S5_CW)
    def kernels(d):
        k = jnp.sum(c_re[d][..., None] * lag_minor(ab_re, d)
                    - c_im[d][..., None] * lag_minor(ab_im, d), axis=2)
        return k.reshape(S5_G, S5_CG, S5_LC, S5_CG).transpose(0, 3, 2, 1)
    zpad = jnp.zeros((S5_G, S5_CG, S5_LC, S5_CG), F32)
    kf = jnp.concatenate([zpad, kernels(0)], axis=2)
    kb = jnp.concatenate([jnp.flip(kernels(1), axis=2), zpad], axis=2)
    tc = jnp.stack([kf[:, :, S5_LC - s:2 * S5_LC - s] + kb[:, :, S5_LC - 1 - s:2 * S5_LC - 1 - s]
                    for s in range(S5_LC)], axis=1).reshape(S5_G, S5_CW, S5_CW)

    def st(arr, d, flip):
        z = arr[:S5_LC, d]
        z = jnp.flip(z, axis=0) if flip else z
        return z.transpose(1, 0, 3, 2).reshape(S5_G, S5_CW, S5_P)
    et = jnp.concatenate([st(ab_re, 0, True), st(ab_re, 1, False),
                          st(ab_im, 0, True), st(ab_im, 1, False)], axis=-1)

    def rd(d, flip):
        pr_ = pw_re[1:, d]
        pi_ = pw_im[1:, d]
        if flip:
            pr_, pi_ = jnp.flip(pr_, axis=0), jnp.flip(pi_, axis=0)
        cr = c_re[d][None] * pr_[:, :, None, :] - c_im[d][None] * pi_[:, :, None, :]
        ci = c_re[d][None] * pi_[:, :, None, :] + c_im[d][None] * pr_[:, :, None, :]
        to = lambda z: z.transpose(1, 3, 0, 2).reshape(S5_G, S5_P, S5_CW)
        return to(cr), to(-ci)
    fr, fi = rd(0, False)
    br, bi = rd(1, True)
    z = jnp.zeros_like(fr)
    ft = jnp.concatenate([fr, z, fi, z, z, br, z, bi], axis=1)

    a16_re = jnp.concatenate([pw_re[S5_LC, 0], pw_re[S5_LC, 1]], axis=-1)[:, None, :]
    a16_im = jnp.concatenate([pw_im[S5_LC, 0], pw_im[S5_LC, 1]], axis=-1)[:, None, :]
    return tc.astype(BF16), et.astype(BF16), ft.astype(BF16), a16_re, a16_im


def _uproj_kernel(wt_ref, h_ref, o_ref):
    acc = lax.dot_general(wt_ref[...], h_ref[...], _NT_DIMS, preferred_element_type=F32)
    o_ref[...] = acc.reshape(S5_G, S5_CG, S5_NT).astype(o_ref.dtype)


def _s5_uproj(h_tl, w_u_t):
    return pl.pallas_call(
        _uproj_kernel,
        grid=(S5_LC, S5_N // S5_NT),
        in_specs=[pl.BlockSpec((S5_W, D), lambda t, n: (0, 0)),
                  pl.BlockSpec((None, S5_NT, D), lambda t, n: (t, n, 0))],
        out_specs=pl.BlockSpec((S5_G, None, S5_CG, S5_NT), lambda t, n: (0, t, 0, n)),
        out_shape=jax.ShapeDtypeStruct((S5_G, S5_LC, S5_CG, S5_N), BF16),
        compiler_params=_params("parallel", "parallel"),
        name="s5_uproj",
    )(w_u_t, h_tl)


S5_PL = S5_NLAT + 4
S5_PC = S5_NCTX + 4
S5_CB = B * S5_PL
S5_ROWS = S5_CB + B * S5_PC


def _s5_chunk_rows(kind, k):
    if kind == "c":
        return pl.ds(S5_CB + k, B, stride=S5_PC)
    return pl.ds(k, B, stride=S5_PL)


def _s5_batch_rows():
    runs = [(b * S5_NLAT, b * S5_PL, S5_NLAT) for b in range(B)]
    runs += [(B * S5_NLAT + b * S5_NCTX, S5_CB + b * S5_PC, S5_NCTX) for b in range(B)]
    return runs


def _s5_kernel(ut_ref, tc_ref, et_ref, ft_ref, ar_ref, ai_ref, d_ref, o_ref,
               he_re_ref, he_im_ref, hpf_re_ref, hpf_im_ref, hpb_re_ref, hpb_im_ref):
    sw = 2 * S5_P
    utf = ut_ref[...].reshape(S5_CW, S5_N).astype(F32)
    un = utf.T.astype(BF16)
    y = jnp.dot(un, tc_ref[...], preferred_element_type=F32)
    he = jnp.dot(un, et_ref[...], preferred_element_type=F32)
    for src, dst, n in _s5_batch_rows():
        he_re_ref[pl.ds(dst, n), :] = he[src:src + n, :sw]
        he_im_ref[pl.ds(dst, n), :] = he[src:src + n, sw:]

    ar = ar_ref[...]
    ai = ai_ref[...]
    is_fwd = lax.broadcasted_iota(jnp.int32, (B, 2 * S5_P), 1) < S5_P
    h_re = jnp.zeros((B, 2 * S5_P), F32)
    h_im = jnp.zeros((B, 2 * S5_P), F32)
    fwd = [("c", k) for k in range(S5_NCTX)] + [("l", k) for k in range(S5_NLAT)]
    bwd = ([("c", k) for k in range(S5_NCTX - 1, -1, -1)]
           + [("l", k) for k in range(S5_NLAT - 1, -1, -1)])
    for cf, cb in zip(fwd, bwd):
        rf = _s5_chunk_rows(*cf)
        rb = _s5_chunk_rows(*cb)
        hpf_re_ref[rf, :] = h_re
        hpf_im_ref[rf, :] = h_im
        hpb_re_ref[rb, :] = h_re
        hpb_im_ref[rb, :] = h_im
        e_re = jnp.where(is_fwd, he_re_ref[rf, :], he_re_ref[rb, :])
        e_im = jnp.where(is_fwd, he_im_ref[rf, :], he_im_ref[rb, :])
        n_re = ar * h_re - ai * h_im + e_re
        n_im = ar * h_im + ai * h_re + e_im
        h_re, h_im = n_re, n_im

    def chunk_order(ref):
        return jnp.concatenate([ref[pl.ds(dst, n), :] for _, dst, n in _s5_batch_rows()], axis=0)
    hp = jnp.concatenate([chunk_order(r) for r in (hpf_re_ref, hpf_im_ref, hpb_re_ref, hpb_im_ref)],
                         axis=1).astype(BF16)
    y = y + jnp.dot(hp, ft_ref[...], preferred_element_type=F32)
    g = y.T + d_ref[...] * utf
    gl = 0.5 * g * (1.0 + lax.erf(g * (0.5 ** 0.5)))
    o_ref[...] = gl.astype(o_ref.dtype).reshape(S5_LC, S5_CG, S5_N)


def _s5_scan(ut, mats, d_col):
    tc, et, ft, a_re, a_im = mats
    mat = pl.BlockSpec((None, S5_CW, S5_CW), lambda g: (g, 0, 0))
    vec = pl.BlockSpec((None, 1, 2 * S5_P), lambda g: (g, 0, 0))
    io = pl.BlockSpec((None, S5_LC, S5_CG, S5_N), lambda g: (g, 0, 0, 0))
    return pl.pallas_call(
        _s5_kernel,
        grid=(S5_G,),
        in_specs=[io, mat, mat,
                  pl.BlockSpec((None, 2 * S5_CW, S5_CW), lambda g: (g, 0, 0)), vec, vec,
                  pl.BlockSpec((None, S5_CW, 1), lambda g: (g, 0, 0))],
        out_specs=io,
        out_shape=jax.ShapeDtypeStruct((S5_G, S5_LC, S5_CG, S5_N), BF16),
        scratch_shapes=[pltpu.VMEM((S5_ROWS, 2 * S5_P), F32)] * 6,
        compiler_params=_params("parallel"),
        name="s5_scan",
    )(ut, tc, et, ft, a_re, a_im, d_col)


def _glu_kernel(gl_ref, w_ref, b_ref, o_ref):
    gl = gl_ref[...].reshape(S5_W, S5_NT)
    z = jnp.dot(w_ref[...], gl, preferred_element_type=F32) + b_ref[...]
    s = gl.astype(F32) * jax.nn.sigmoid(z)
    o_ref[...] = s.T.astype(o_ref.dtype)


def _s5_glu(glt, w_t, b_col):
    out = pl.pallas_call(
        _glu_kernel,
        grid=(S5_LC, S5_N // S5_NT),
        in_specs=[pl.BlockSpec((S5_G, None, S5_CG, S5_NT), lambda t, n: (0, t, 0, n)),
                  pl.BlockSpec((S5_W, S5_W), lambda t, n: (0, 0)),
                  pl.BlockSpec((S5_W, 1), lambda t, n: (0, 0))],
        out_specs=pl.BlockSpec((S5_NT, S5_W), lambda t, n: (n, t)),
        out_shape=jax.ShapeDtypeStruct((S5_N, S5_LC * S5_W), BF16),
        compiler_params=_params("parallel", "parallel"),
        name="s5_glu",
    )(glt, w_t, b_col)
    return out.reshape(NT, S5_W)


def _rope_tables():
    half = GLA_DK // 2
    freqs = ROPE_BASE ** (-np.arange(0, half, 2, dtype=np.float32) / half)
    out = []
    for n in (ROWS, GRID_W):
        ang = np.arange(n, dtype=np.float32)[:, None] * freqs[None, :]
        c, s = np.cos(ang), np.sin(ang)
        out += [np.concatenate([c, c], axis=-1), np.concatenate([-s, s], axis=-1)]
    return tuple(jnp.asarray(t, F32) for t in out)


def _rope(x, rcos, rsin, ccos, csin):
    x0 = x[:, :128]
    x1 = x[:, 128:]
    return jnp.concatenate([x0 * rcos + pltpu.roll(x0, 64, axis=1) * rsin,
                            x1 * ccos + pltpu.roll(x1, 64, axis=1) * csin], axis=-1)


def _cumsum_rows(x, reverse):
    row = lax.broadcasted_iota(jnp.int32, x.shape, 0)
    s = 1
    while s < GLA_C:
        if s >= 8:
            if reverse:
                x = jnp.concatenate([x[:GLA_C - s] + x[s:], x[GLA_C - s:]], axis=0)
            else:
                x = jnp.concatenate([x[:s], x[s:] + x[:GLA_C - s]], axis=0)
        elif reverse:
            x = x + jnp.where(row < GLA_C - s, pltpu.roll(x, GLA_C - s, axis=0), 0.0)
        else:
            x = x + jnp.where(row >= s, pltpu.roll(x, s, axis=0), 0.0)
        s *= 2
    return x


def _log_sigmoid(x):
    return jnp.minimum(x, 0.0) - jnp.log(1.0 + jnp.exp(-jnp.abs(x)))


def _chunk_rows(c):
    if isinstance(c, int):
        return pl.ds(c * GLA_C, GLA_C)
    return pl.ds(pl.multiple_of(c * GLA_C, GLA_C), GLA_C)


def _gla_kernel(ql_ref, kl_ref, vl_ref, gl_ref, qc_ref, kc_ref, vc_ref, al_ref, ac_ref,
                waf_ref, wab_ref, baf_ref, bab_ref, rcos_ref, rsin_ref, ccos_ref, csin_ref,
                ng_ref, o_ref,
                qi_f, ki_f, ke_f, dec_f, st_f, qi_b, ki_b, ke_b, dec_b, st_b, acc_ref):
    n_ctx = L // GLA_C
    n_lat = S // GLA_C
    qscale = GLA_DK ** -0.5
    fwd = (waf_ref, baf_ref, qi_f, ki_f, ke_f, dec_f, False)
    bwd = (wab_ref, bab_ref, qi_b, ki_b, ke_b, dec_b, True)

    ii = lax.broadcasted_iota(jnp.int32, (GLA_C, GLA_C), 0)
    jj = lax.broadcasted_iota(jnp.int32, (GLA_C, GLA_C), 1)

    def prepare(direction, q, k, a, c):
        wa_ref, ba_ref, qi, ki, ke, dec, reverse = direction
        dst = _chunk_rows(c)
        la = _log_sigmoid(jnp.dot(a, wa_ref[...], preferred_element_type=F32)
                          + ba_ref[...]) / GLA_TAU
        bc = _cumsum_rows(la, reverse)
        b_last = bc[0:1, :] if reverse else bc[GLA_C - 1:GLA_C, :]
        qi[dst, :] = (q * jnp.exp(bc)).astype(BF16)
        ki[dst, :] = (k * jnp.exp(-bc)).astype(BF16)
        ke[dst, :] = (k * jnp.exp(b_last - bc)).astype(BF16)
        dec[pl.ds(c, 1), :] = jnp.exp(b_last)

    def prepare_latent(direction, c):
        r = _chunk_rows(c)
        tabs = (rcos_ref[pl.ds(c, 1), :], rsin_ref[pl.ds(c, 1), :], ccos_ref[...], csin_ref[...])
        q = _rope(ql_ref[r, :].astype(F32) * qscale, *tabs)
        k = _rope(kl_ref[r, :].astype(F32), *tabs)
        prepare(direction, q, k, al_ref[r, :], n_ctx + c)

    def prep_ctx(c, carry):
        r = _chunk_rows(c)
        q = qc_ref[r, :].astype(F32) * qscale
        k = kc_ref[r, :].astype(F32)
        prepare(fwd, q, k, ac_ref[r, :], c)
        prepare(bwd, q, k, ac_ref[r, :], c)
        return carry
    lax.fori_loop(0, n_ctx, prep_ctx, 0)
    prepare_latent(fwd, 0)
    prepare_latent(bwd, n_lat - 1)

    def advance(direction, st_ref, c, v, want_out):
        _, _, qi, ki, ke, dec, reverse = direction
        r = _chunk_rows(c)
        st = st_ref[...]
        o = None
        if want_out:
            q_in = qi[r, :]
            att = lax.dot_general(q_in, ki[r, :], _NT_DIMS, preferred_element_type=F32)
            att = jnp.where((ii <= jj) if reverse else (ii >= jj), att, 0.0)
            o = (jnp.dot(att.astype(BF16), v, preferred_element_type=F32)
                 + lax.dot_general(q_in, st.astype(BF16), _NT_DIMS, preferred_element_type=F32))
        st_ref[...] = dec[pl.ds(c, 1), :] * st + lax.dot_general(
            v, ke[r, :], (((0,), (0,)), ((), ())), preferred_element_type=F32)
        return o

    st_f[...] = jnp.zeros_like(st_f)
    st_b[...] = jnp.zeros_like(st_b)

    def ctx_pair(j, carry):
        cb = n_ctx - 1 - j
        advance(fwd, st_f, j, vc_ref[_chunk_rows(j), :], False)
        advance(bwd, st_b, cb, vc_ref[_chunk_rows(cb), :], False)
        return carry
    lax.fori_loop(0, n_ctx, ctx_pair, 0)

    def lat_pair(j, accumulate):
        cb = n_lat - 1 - j
        rf = _chunk_rows(j)
        rb = _chunk_rows(cb)
        o_f = advance(fwd, st_f, n_ctx + j, vl_ref[rf, :], True)
        o_b = advance(bwd, st_b, n_ctx + cb, vl_ref[rb, :], True)
        if accumulate:
            acc_ref[rf, :] += o_f
            acc_ref[rb, :] += o_b
        else:
            acc_ref[rf, :] = o_f
            acc_ref[rb, :] = o_b
        prepare_latent(fwd, jnp.minimum(j + 1, n_lat - 1))
        prepare_latent(bwd, jnp.maximum(cb - 1, 0))

    def lat_first(j, carry):
        lat_pair(j, False)
        return carry
    lax.fori_loop(0, n_lat // 2, lat_first, 0, unroll=2)

    def lat_second(j, carry):
        lat_pair(j, True)
        return carry
    lax.fori_loop(n_lat // 2, n_lat, lat_second, 0, unroll=2)

    tr = 256

    def fin(t, carry):
        r = pl.ds(pl.multiple_of(t * tr, tr), tr)
        o = acc_ref[r, :]
        ms = jnp.mean(o * o, axis=-1, keepdims=True)
        g = gl_ref[r, :].astype(F32)
        o_ref[r, :] = (o * lax.rsqrt(ms + EPS) * ng_ref[...]
                       * (g * jax.nn.sigmoid(g))).astype(o_ref.dtype)
        return carry
    lax.fori_loop(0, S // tr, fin, 0)


def _gla(qkvg, acode, wa, ba, rope, norm_g):
    ctx0 = NLAT // L
    kq = GLA_QK // GLA_DK
    half = GLA_DK // 2
    n_chunks = (L + S) // GLA_C
    per_dir = [pltpu.VMEM((L + S, GLA_DK), BF16)] * 3 + [pltpu.VMEM((n_chunks, GLA_DK), F32),
                                                         pltpu.VMEM((GLA_DV, GLA_DK), F32)]
    in_specs = [
        pl.BlockSpec((S, GLA_DK), lambda b, h: (b, h)),
        pl.BlockSpec((S, GLA_DK), lambda b, h: (b, kq + h)),
        pl.BlockSpec((S, GLA_DV), lambda b, h: (b, kq + h)),
        pl.BlockSpec((S, GLA_DV), lambda b, h: (b, 2 * kq + h)),
        pl.BlockSpec((L, GLA_DK), lambda b, h: (ctx0 + b, h)),
        pl.BlockSpec((L, GLA_DK), lambda b, h: (ctx0 + b, kq + h)),
        pl.BlockSpec((L, GLA_DV), lambda b, h: (ctx0 + b, kq + h)),
        pl.BlockSpec((S, 128), lambda b, h: (b, 0)),
        pl.BlockSpec((L, 128), lambda b, h: (ctx0 + b, 0)),
        pl.BlockSpec((128, GLA_DK), lambda b, h: (0, h)),
        pl.BlockSpec((128, GLA_DK), lambda b, h: (0, kq + h)),
        pl.BlockSpec((1, GLA_DK), lambda b, h: (0, h)),
        pl.BlockSpec((1, GLA_DK), lambda b, h: (0, kq + h)),
        pl.BlockSpec((ROWS, half), lambda b, h: (0, 0)),
        pl.BlockSpec((ROWS, half), lambda b, h: (0, 0)),
        pl.BlockSpec((GRID_W, half), lambda b, h: (0, 0)),
        pl.BlockSpec((GRID_W, half), lambda b, h: (0, 0)),
        pl.BlockSpec((1, GLA_DV), lambda b, h: (0, 0)),
    ]
    return pl.pallas_call(
        _gla_kernel,
        grid=(B, GLA_H),
        in_specs=in_specs,
        out_specs=pl.BlockSpec((S, GLA_DV), lambda b, h: (b, h)),
        out_shape=jax.ShapeDtypeStruct((NLAT, GLA_VW), BF16),
        scratch_shapes=per_dir + per_dir + [pltpu.VMEM((S, GLA_DV), F32)],
        compiler_params=_params("parallel", "parallel"),
        name="gla",
    )(qkvg, qkvg, qkvg, qkvg, qkvg, qkvg, qkvg, acode, acode, wa, wa, ba, ba, *rope,
      norm_g.reshape(1, GLA_DV))


def kernel(x, c, ctx, c_ctx, ada_w, ada_b, norm1_g, norm2_g, mlp_w1, mlp_w2, final_g, ab_w_in, ab_w_out, na_rel_bias, s5_lambda_re, s5_lambda_im, s5_log_dt, s5_b_re, s5_b_im, s5_c_re, s5_c_im, s5_d, s5_glu_w, s5_glu_b, gla_w_in, gla_w_a2, gla_b_a, gla_norm_g, gla_w_out):
    xs = (x.astype(F32).reshape(NLAT, D), ctx.astype(F32).reshape(NCTX, D))
    cvec = jnp.zeros((16, D), F32).at[:B].set(c.astype(F32)).at[B].set(c_ctx.astype(F32))
    mods = _ada_mod(cvec, ada_w, ada_b).reshape(2, 16, 1, 6 * D)
    bf = lambda w: w.astype(BF16)

    mod = mods[0]
    h, h_tl = _normmod(xs, mod, norm1_g[0], 0, 1, NT)
    w_in = ab_w_in[0]
    qkv = _mm(h, ab_w_in, 0, 3 * NA_W, NT)
    att = _na_attention(qkv, _na_tables(na_rel_bias[0]))
    ut = _s5_uproj(h_tl, bf(w_in[:, 3 * NA_W:].T))
    mats = _s5_matrices(s5_lambda_re[0], s5_lambda_im[0], s5_log_dt[0], s5_b_re[0], s5_b_im[0],
                        s5_c_re[0], s5_c_im[0])
    d_col = jnp.tile(s5_d[0].astype(F32).reshape(S5_G, 1, S5_CG), (1, S5_LC, 1)).reshape(S5_G, S5_CW, 1)
    glt = _s5_scan(ut, mats, d_col)
    s5 = _s5_glu(glt, bf(s5_glu_w[0].T), s5_glu_b[0].astype(F32).reshape(S5_W, 1))
    w2 = bf(mlp_w2)
    xs, h = _mm_res([att, s5], bf(ab_w_out), 0, xs, mod, 2, NT, nxt=(mod, norm2_g[0], 3, 4))
    hid = _mm(h, mlp_w1, 0, MLP_H, NT, relu2=True)
    xs, h = _mm_res([hid], w2, 0, xs, mod, 5, NT, nxt=(mods[1], norm1_g[1], 0, 1))

    mod = mods[1]
    w_in = gla_w_in[0]
    qkvg = _mm(h, gla_w_in, 0, GLA_MAIN, NT)
    w_code = jnp.zeros((1, D, 128), F32).at[0, :, :2 * GLA_RANK].set(w_in[:, GLA_MAIN:])
    acode = _mm(h, w_code, 0, 128, NT)
    wa = (jnp.zeros((128, 2 * GLA_QK), F32)
          .at[:GLA_RANK, :GLA_QK].set(gla_w_a2[0, 0])
          .at[GLA_RANK:2 * GLA_RANK, GLA_QK:].set(gla_w_a2[0, 1]))
    ba = gla_b_a[0].astype(F32).reshape(1, 2 * GLA_QK)
    og = _gla(qkvg, acode, bf(wa), ba, _rope_tables(), gla_norm_g[0].astype(F32))
    xl, h = _mm_res([og], bf(gla_w_out), 0, xs, mod, 2, NLAT, nxt=(mod, norm2_g[1], 3, 4))
    hid = _mm(h, mlp_w1, 1, MLP_H, NLAT, relu2=True)
    out = _mm_res([hid], w2, 1, xl, mod, 5, NLAT, final_g=final_g.astype(F32))
    return out.reshape(B, S, D).astype(x.dtype)
```

```python
import functools
import math

import numpy as np
import jax
import jax.numpy as jnp
from jax import lax
from jax.experimental import pallas as pl
from jax.experimental.pallas import tpu as pltpu

F32 = jnp.float32
BF16 = jnp.bfloat16

D = 2048
B = 8
S = 2048
L = 256
GRID_W = 64
ROWS = S // GRID_W
NLAT = B * S
NCTX = B * L
NT = NLAT + NCTX
MLP_H = 4 * D
EPS = 1e-6
NEG_INF = -1e30

NA_H = 8
NA_DH = 128
NA_W = NA_H * NA_DH
NA_SCALE = NA_DH ** -0.5
NA_HB = 4
NA_QROWS = 4
NA_KROWS = 12
NA_QT = NA_QROWS * GRID_W
NA_KT = NA_KROWS * GRID_W

S5_W = D // 2
S5_CG = 16
S5_G = S5_W // S5_CG
S5_P = 64
S5_LC = 16
S5_CW = S5_LC * S5_CG
S5_NLAT = S // S5_LC
S5_NCTX = L // S5_LC
S5_N = B * (S5_NLAT + S5_NCTX)
S5_NT = S5_N

GLA_H = 4
GLA_DK = 256
GLA_DV = 512
GLA_QK = GLA_H * GLA_DK
GLA_VW = GLA_H * GLA_DV
GLA_RANK = 16
GLA_TAU = 16.0
GLA_C = 64
GLA_MAIN = 2 * GLA_QK + 2 * GLA_VW
ROPE_BASE = 10000.0

MM_RES_NSPLIT = 4

VMEM_LIMIT = 56 * 1024 * 1024

_NT_DIMS = (((1,), (1,)), ((), ()))


def _params(*sem):
    return pltpu.CompilerParams(dimension_semantics=sem, vmem_limit_bytes=VMEM_LIMIT)


def _mod_row(i, tm):
    return jnp.minimum((i * tm) // S, B)


def _ada_kernel(c_ref, w_ref, b_ref, o_ref):
    c = c_ref[...]
    s = c * jax.nn.sigmoid(c)
    o_ref[...] = jnp.dot(s.astype(BF16), w_ref[...].astype(BF16),
                         preferred_element_type=F32) + b_ref[...]


def _ada_mod(cvec, ada_w, ada_b):
    depth = ada_w.shape[0]
    tn = 512
    return pl.pallas_call(
        _ada_kernel,
        grid=(depth, 6 * D // tn),
        in_specs=[pl.BlockSpec((16, D), lambda l, j: (0, 0)),
                  pl.BlockSpec((None, D, tn), lambda l, j: (l, 0, j)),
                  pl.BlockSpec((None, 1, tn), lambda l, j: (l, 0, j))],
        out_specs=pl.BlockSpec((None, 16, tn), lambda l, j: (l, 0, j)),
        out_shape=jax.ShapeDtypeStruct((depth, 16, 6 * D), F32),
        compiler_params=_params("parallel", "parallel"),
        name="ada_mod",
    )(cvec, ada_w, ada_b.reshape(depth, 1, 6 * D))


def _stream_specs(xs, tm, two_axes):
    if not isinstance(xs, tuple):
        imap = (lambda i, k: (i, 0)) if two_axes else (lambda i: (i, 0))
        return 0, [xs], [pl.BlockSpec((tm, D), imap)]
    nl = xs[0].shape[0] // tm
    if two_axes:
        maps = [lambda i, k: (jnp.minimum(i, nl - 1), 0), lambda i, k: (jnp.maximum(i - nl, 0), 0)]
    else:
        maps = [lambda i: (jnp.minimum(i, nl - 1), 0), lambda i: (jnp.maximum(i - nl, 0), 0)]
    return nl, list(xs), [pl.BlockSpec((tm, D), m) for m in maps]


def _stream_tile(x_refs, n_lat_tiles, cols=slice(None)):
    if len(x_refs) == 1:
        return x_refs[0][:, cols]
    return jnp.where(pl.program_id(0) < n_lat_tiles, x_refs[0][:, cols], x_refs[1][:, cols])


def _normmod_kernel(*refs, n_x, n_lat_tiles):
    x_refs = refs[:n_x]
    sh_ref, sc_ref, g_ref, o_ref, oc_ref = refs[n_x:]
    x = _stream_tile(x_refs, n_lat_tiles)
    ms = jnp.mean(x * x, axis=-1, keepdims=True)
    h = x * lax.rsqrt(ms + EPS) * g_ref[...]
    h = h * (1.0 + sc_ref[...]) + sh_ref[...]
    o_ref[...] = h.astype(o_ref.dtype)
    h3 = h.reshape(h.shape[0] // S5_LC, S5_LC, D)
    for tl in range(S5_LC):
        oc_ref[tl] = h3[:, tl, :].astype(oc_ref.dtype)


def _normmod(xs, mod, g, shift_idx, scale_idx, rows):
    tm = 512
    nl, x_args, x_specs = _stream_specs(xs, tm, False)
    return pl.pallas_call(
        functools.partial(_normmod_kernel, n_x=len(x_args), n_lat_tiles=nl),
        grid=(rows // tm,),
        in_specs=x_specs + [
            pl.BlockSpec((None, 1, D), lambda i: (_mod_row(i, tm), 0, shift_idx)),
            pl.BlockSpec((None, 1, D), lambda i: (_mod_row(i, tm), 0, scale_idx)),
            pl.BlockSpec((1, D), lambda i: (0, 0))],
        out_specs=[pl.BlockSpec((tm, D), lambda i: (i, 0)),
                   pl.BlockSpec((S5_LC, tm // S5_LC, D), lambda i: (0, i, 0))],
        out_shape=[jax.ShapeDtypeStruct((rows, D), BF16),
                   jax.ShapeDtypeStruct((S5_LC, rows // S5_LC, D), BF16)],
        compiler_params=_params("parallel"),
        name="normmod",
    )(*x_args, mod, mod, g.reshape(1, D))


def _mm_kernel(a_ref, w_ref, o_ref, wb_ref, *, relu2):
    @pl.when(pl.program_id(1) == 0)
    def _():
        wb_ref[...] = w_ref[...].astype(BF16)

    acc = jnp.dot(a_ref[...], wb_ref[...], preferred_element_type=F32)
    if relu2:
        acc = jnp.square(jnp.maximum(acc, 0.0))
    o_ref[...] = acc.astype(o_ref.dtype)


def _mm(a, w, layer, n, rows, *, relu2=False, tm=2048, tn=1024):
    k = a.shape[1]
    tn = min(tn, n)
    return pl.pallas_call(
        functools.partial(_mm_kernel, relu2=relu2),
        grid=(n // tn, rows // tm),
        in_specs=[pl.BlockSpec((tm, k), lambda j, i: (i, 0)),
                  pl.BlockSpec((None, k, tn), lambda j, i: (layer, 0, j))],
        out_specs=pl.BlockSpec((tm, tn), lambda j, i: (i, j)),
        out_shape=jax.ShapeDtypeStruct((rows, n), BF16),
        scratch_shapes=[pltpu.VMEM((k, tn), BF16)],
        compiler_params=_params("parallel", "arbitrary"),
        name="mm_relu2" if relu2 else "mm",
    )(a, w)


def _mm_res_kernel(*refs, n_lhs, n_res, n_lat_tiles, nk, final_norm, next_norm):
    a_refs = refs[:n_lhs]
    w_refs = refs[n_lhs:2 * n_lhs]
    res_refs = refs[2 * n_lhs:2 * n_lhs + n_res]
    gate_ref = refs[2 * n_lhs + n_res]
    pos = 2 * n_lhs + n_res + 1
    n_extra = 1 if final_norm else (3 if next_norm else 0)
    extra = refs[pos:pos + n_extra]
    o_ref = refs[pos + n_extra]
    h_ref = refs[pos + n_extra + 1] if next_norm else None

    def rms(y, g_ref):
        ms = jnp.mean(y * y, axis=-1, keepdims=True)
        return y * lax.rsqrt(ms + EPS) * g_ref[...]

    def emit_next(y):
        sh_ref, sc_ref, g_ref = extra
        h_ref[...] = (rms(y, g_ref) * (1.0 + sc_ref[...]) + sh_ref[...]).astype(h_ref.dtype)

    kk = pl.program_id(1)
    if nk > 1:
        @pl.when(kk == 0)
        def _():
            o_ref[...] = _stream_tile(res_refs, n_lat_tiles)

    wn = D // MM_RES_NSPLIT
    for cc in range(MM_RES_NSPLIT):
        cols = slice(cc * wn, (cc + 1) * wn)
        part = None
        for a_ref, w_ref in zip(a_refs, w_refs):
            d = jnp.dot(a_ref[...], w_ref[:, cols], preferred_element_type=F32)
            part = d if part is None else part + d
        part = gate_ref[:, cols] * part
        if nk == 1:
            o_ref[:, cols] = _stream_tile(res_refs, n_lat_tiles, cols) + part
        else:
            o_ref[:, cols] += part

    def epilogue():
        if final_norm:
            o_ref[...] = rms(o_ref[...], extra[0])
        elif next_norm:
            emit_next(o_ref[...])

    if nk == 1:
        epilogue()
    elif final_norm or next_norm:
        pl.when(kk == nk - 1)(epilogue)


def _mm_res(a_list, w, layer, resid, mod, gate_idx, rows, *, final_g=None, nxt=None,
            tm=512, tk=2048):
    n_lhs = len(a_list)
    kdim = a_list[0].shape[1]
    tk = min(kdim, tk)
    nk = kdim // tk
    nl, res_args, res_specs = _stream_specs(resid, tm, True)

    def mod_spec(idx):
        return pl.BlockSpec((None, 1, D), lambda i, k: (_mod_row(i, tm), 0, idx))

    def w_spec(j):
        return pl.BlockSpec((None, tk, D), lambda i, k: (layer, j * nk + k, 0))

    row_spec = pl.BlockSpec((tm, D), lambda i, k: (i, 0))
    vec_spec = pl.BlockSpec((1, D), lambda i, k: (0, 0))
    in_specs = ([pl.BlockSpec((tm, tk), lambda i, k: (i, k)) for _ in a_list]
                + [w_spec(j) for j in range(n_lhs)] + res_specs + [mod_spec(gate_idx)])
    args = list(a_list) + [w] * n_lhs + res_args + [mod]
    out_specs, out_shape = row_spec, jax.ShapeDtypeStruct((rows, D), F32)
    if final_g is not None:
        in_specs.append(vec_spec)
        args.append(final_g.reshape(1, D))
    elif nxt is not None:
        mod_n, g_n, shift_idx, scale_idx = nxt
        in_specs += [mod_spec(shift_idx), mod_spec(scale_idx), vec_spec]
        args += [mod_n, mod_n, g_n.reshape(1, D)]
        out_specs = [row_spec, row_spec]
        out_shape = [out_shape, jax.ShapeDtypeStruct((rows, D), BF16)]
    return pl.pallas_call(
        functools.partial(_mm_res_kernel, n_lhs=n_lhs, n_res=len(res_args), n_lat_tiles=nl,
                          nk=nk, final_norm=final_g is not None, next_norm=nxt is not None),
        grid=(rows // tm, nk),
        in_specs=in_specs,
        out_specs=out_specs,
        out_shape=out_shape,
        compiler_params=_params("parallel", "arbitrary"),
        name="mm_res",
    )(*args)


def _na_tables(rel_bias):
    hp = lax.Precision.HIGHEST
    cq = np.arange(GRID_W)[:, None]
    ck = np.arange(GRID_W)[None, :]
    ws = np.clip(cq - 8, 0, GRID_W - 16)
    col_ok = (ck >= ws) & (ck < ws + 16)
    col_hot = ((ck - cq + 15)[..., None] == np.arange(31)) & col_ok[..., None]
    blocks = jnp.einsum('hrj,qkj->hrqk', rel_bias.astype(F32), col_hot.astype(np.float32),
                        precision=hp)
    blocks = blocks + np.where(col_ok, 0.0, NEG_INF).astype(np.float32)
    masked = jnp.full((NA_H, GRID_W, GRID_W), NEG_INF, F32)
    pats = []
    for r0, start in ((0, 0), (NA_QROWS, 0), (ROWS - NA_QROWS, ROWS - NA_KROWS)):
        rows = []
        for a in range(NA_QROWS):
            r = r0 + a
            rs = min(max(r - 4, 0), ROWS - 8)
            rows.append(jnp.concatenate(
                [blocks[:, krow - r + 7] if rs <= krow < rs + 8 else masked
                 for krow in range(start, start + NA_KROWS)], axis=-1))
        pats.append(jnp.concatenate(rows, axis=1))
    return jnp.stack(pats, axis=1)


def _na_kernel(q_ref, k0_ref, k1_ref, k2_ref, v0_ref, v1_ref, v2_ref, kc_ref, vc_ref,
               tab_ref, o_ref):
    i = pl.program_id(1)

    def head(ref, hh):
        return ref[:, hh * NA_DH:(hh + 1) * NA_DH]

    def ctx_scores(hh):
        return lax.dot_general(head(q_ref, hh), head(kc_ref, hh), _NT_DIMS,
                               preferred_element_type=F32) * NA_SCALE

    @pl.when(i < ROWS // NA_QROWS)
    def _():
        for hh in range(NA_HB):
            q = head(q_ref, hh)
            s_c = ctx_scores(hh)
            m = jnp.max(s_c, axis=-1, keepdims=True)
            s_w = []
            for d, k_ref in enumerate((k0_ref, k1_ref, k2_ref)):
                s = lax.dot_general(q, head(k_ref, hh), _NT_DIMS,
                                    preferred_element_type=F32) * NA_SCALE
                s = s + tab_ref[hh, :, d * NA_QT:(d + 1) * NA_QT]
                s_w.append(s)
                m = jnp.maximum(m, jnp.max(s, axis=-1, keepdims=True))
            p_c = jnp.exp(s_c - m)
            l = jnp.sum(p_c, axis=-1, keepdims=True)
            o = jnp.dot(p_c.astype(BF16), head(vc_ref, hh), preferred_element_type=F32)
            for s, v_ref in zip(s_w, (v0_ref, v1_ref, v2_ref)):
                p = jnp.exp(s - m)
                l = l + jnp.sum(p, axis=-1, keepdims=True)
                o = o + jnp.dot(p.astype(BF16), head(v_ref, hh), preferred_element_type=F32)
            o_ref[:, hh * NA_DH:(hh + 1) * NA_DH] = (o / l).astype(o_ref.dtype)

    @pl.when(i == ROWS // NA_QROWS)
    def _():
        for hh in range(NA_HB):
            s_c = ctx_scores(hh)
            p_c = jnp.exp(s_c - jnp.max(s_c, axis=-1, keepdims=True))
            l = jnp.sum(p_c, axis=-1, keepdims=True)
            o = jnp.dot(p_c.astype(BF16), head(vc_ref, hh), preferred_element_type=F32)
            o_ref[:, hh * NA_DH:(hh + 1) * NA_DH] = (o / l).astype(o_ref.dtype)


def _na_attention(qkv, table):
    ng = ROWS // NA_QROWS
    blk = S // NA_QT
    ctx0 = NLAT // NA_QT

    def qrow(h, i, b):
        return jnp.where(i < ng, b * blk + i, ctx0 + b)

    nhb = NA_H // NA_HB

    def krow(d):
        return lambda h, i, b: (b * blk + jnp.clip(i - 1, 0, blk - 3) + d, nhb + h)

    def vrow(d):
        return lambda h, i, b: (b * blk + jnp.clip(i - 1, 0, blk - 3) + d, 2 * nhb + h)

    def pat(h, i, b):
        return (h, jnp.where(i == 0, 0, jnp.where(i >= ng - 1, 2, 1)), 0, 0)

    tile = (NA_QT, NA_HB * NA_DH)
    in_specs = ([pl.BlockSpec(tile, lambda h, i, b: (qrow(h, i, b), h))]
                + [pl.BlockSpec(tile, krow(d)) for d in range(3)]
                + [pl.BlockSpec(tile, vrow(d)) for d in range(3)]
                + [pl.BlockSpec(tile, lambda h, i, b: (ctx0 + b, nhb + h)),
                   pl.BlockSpec(tile, lambda h, i, b: (ctx0 + b, 2 * nhb + h)),
                   pl.BlockSpec((NA_HB, None, NA_QT, NA_KT), pat)])
    return pl.pallas_call(
        _na_kernel,
        grid=(nhb, ng + 1, B),
        in_specs=in_specs,
        out_specs=pl.BlockSpec(tile, lambda h, i, b: (qrow(h, i, b), h)),
        out_shape=jax.ShapeDtypeStruct((NT, NA_W), BF16),
        compiler_params=_params("parallel", "parallel", "parallel"),
        name="na_attention",
    )(*([qkv] * 9), table)


def _s5_matrices(lam_re, lam_im, log_dt, b_re, b_im, c_re, c_im):
    lam_re, lam_im = lam_re.astype(F32), lam_im.astype(F32)
    b_re, b_im = b_re.astype(F32), b_im.astype(F32)
    c_re, c_im = c_re.astype(F32), c_im.astype(F32)
    dt = jnp.exp(log_dt.astype(F32))[..., None]
    mag = jnp.exp(lam_re * dt)
    a_re = mag * jnp.cos(lam_im * dt)
    a_im = mag * jnp.sin(lam_im * dt)
    den = lam_re * lam_re + lam_im * lam_im
    f_re = ((a_re - 1.0) * lam_re + a_im * lam_im) / den
    f_im = (a_im * lam_re - (a_re - 1.0) * lam_im) / den
    bb_re = f_re[..., None] * b_re - f_im[..., None] * b_im
    bb_im = f_re[..., None] * b_im + f_im[..., None] * b_re

    pw_re, pw_im = jnp.ones_like(a_re)[None], jnp.zeros_like(a_im)[None]
    an_re, an_im = a_re, a_im
    while pw_re.shape[0] < S5_LC:
        pw_re, pw_im = (jnp.concatenate([pw_re, pw_re * an_re - pw_im * an_im]),
                        jnp.concatenate([pw_im, pw_re * an_im + pw_im * an_re]))
        an_re, an_im = an_re * an_re - an_im * an_im, 2.0 * an_re * an_im
    pw_re = jnp.concatenate([pw_re, an_re[None]])
    pw_im = jnp.concatenate([pw_im, an_im[None]])

    ab_re = pw_re[..., None] * bb_re[None] - pw_im[..., None] * bb_im[None]
    ab_im = pw_re[..., None] * bb_im[None] + pw_im[..., None] * bb_re[None]
    def lag_minor(z, d):
        return z[:S5_LC, d].transpose(1, 2, 0, 3).reshape(S5_G, 1, S5_P, S5_CW)
    def kernels(d):
        k = jnp.sum(c_re[d][..., None] * lag_minor(ab_re, d)
                    - c_im[d][..., None] * lag_minor(ab_im, d), axis=2)
        return k.reshape(S5_G, S5_CG, S5_LC, S5_CG).transpose(0, 3, 2, 1)
    zpad = jnp.zeros((S5_G, S5_CG, S5_LC, S5_CG), F32)
    kf = jnp.concatenate([zpad, kernels(0)], axis=2)
    kb = jnp.concatenate([jnp.flip(kernels(1), axis=2), zpad], axis=2)
    tc = jnp.stack([kf[:, :, S5_LC - s:2 * S5_LC - s] + kb[:, :, S5_LC - 1 - s:2 * S5_LC - 1 - s]
                    for s in range(S5_LC)], axis=1).reshape(S5_G, S5_CW, S5_CW)

    def st(arr, d, flip):
        z = arr[:S5_LC, d]
        z = jnp.flip(z, axis=0) if flip else z
        return z.transpose(1, 0, 3, 2).reshape(S5_G, S5_CW, S5_P)
    et = jnp.concatenate([st(ab_re, 0, True), st(ab_re, 1, False),
                          st(ab_im, 0, True), st(ab_im, 1, False)], axis=-1)

    def rd(d, flip):
        pr_ = pw_re[1:, d]
        pi_ = pw_im[1:, d]
        if flip:
            pr_, pi_ = jnp.flip(pr_, axis=0), jnp.flip(pi_, axis=0)
        cr = c_re[d][None] * pr_[:, :, None, :] - c_im[d][None] * pi_[:, :, None, :]
        ci = c_re[d][None] * pi_[:, :, None, :] + c_im[d][None] * pr_[:, :, None, :]
        to = lambda z: z.transpose(1, 3, 0, 2).reshape(S5_G, S5_P, S5_CW)
        return to(cr), to(-ci)
    fr, fi = rd(0, False)
    br, bi = rd(1, True)
    z = jnp.zeros_like(fr)
    ft = jnp.concatenate([fr, z, fi, z, z, br, z, bi], axis=1)

    a16_re = jnp.concatenate([pw_re[S5_LC, 0], pw_re[S5_LC, 1]], axis=-1)[:, None, :]
    a16_im = jnp.concatenate([pw_im[S5_LC, 0], pw_im[S5_LC, 1]], axis=-1)[:, None, :]
    return tc.astype(BF16), et.astype(BF16), ft.astype(BF16), a16_re, a16_im


def _uproj_kernel(wt_ref, h_ref, o_ref):
    acc = lax.dot_general(wt_ref[...], h_ref[...], _NT_DIMS, preferred_element_type=F32)
    o_ref[...] = acc.reshape(S5_G, S5_CG, S5_NT).astype(o_ref.dtype)


def _s5_uproj(h_tl, w_u_t):
    return pl.pallas_call(
        _uproj_kernel,
        grid=(S5_LC, S5_N // S5_NT),
        in_specs=[pl.BlockSpec((S5_W, D), lambda t, n: (0, 0)),
                  pl.BlockSpec((None, S5_NT, D), lambda t, n: (t, n, 0))],
        out_specs=pl.BlockSpec((S5_G, None, S5_CG, S5_NT), lambda t, n: (0, t, 0, n)),
        out_shape=jax.ShapeDtypeStruct((S5_G, S5_LC, S5_CG, S5_N), BF16),
        compiler_params=_params("parallel", "parallel"),
        name="s5_uproj",
    )(w_u_t, h_tl)


S5_PL = S5_NLAT + 4
S5_PC = S5_NCTX + 4
S5_CB = B * S5_PL
S5_ROWS = S5_CB + B * S5_PC


def _s5_chunk_rows(kind, k):
    if kind == "c":
        return pl.ds(S5_CB + k, B, stride=S5_PC)
    return pl.ds(k, B, stride=S5_PL)


def _s5_batch_rows():
    runs = [(b * S5_NLAT, b * S5_PL, S5_NLAT) for b in range(B)]
    runs += [(B * S5_NLAT + b * S5_NCTX, S5_CB + b * S5_PC, S5_NCTX) for b in range(B)]
    return runs


def _s5_kernel(ut_ref, tc_ref, et_ref, ft_ref, ar_ref, ai_ref, d_ref, o_ref,
               he_re_ref, he_im_ref, hpf_re_ref, hpf_im_ref, hpb_re_ref, hpb_im_ref):
    sw = 2 * S5_P
    utf = ut_ref[...].reshape(S5_CW, S5_N).astype(F32)
    un = utf.T.astype(BF16)
    y = jnp.dot(un, tc_ref[...], preferred_element_type=F32)
    he = jnp.dot(un, et_ref[...], preferred_element_type=F32)
    for src, dst, n in _s5_batch_rows():
        he_re_ref[pl.ds(dst, n), :] = he[src:src + n, :sw]
        he_im_ref[pl.ds(dst, n), :] = he[src:src + n, sw:]

    ar = ar_ref[...]
    ai = ai_ref[...]
    is_fwd = lax.broadcasted_iota(jnp.int32, (B, 2 * S5_P), 1) < S5_P
    h_re = jnp.zeros((B, 2 * S5_P), F32)
    h_im = jnp.zeros((B, 2 * S5_P), F32)
    fwd = [("c", k) for k in range(S5_NCTX)] + [("l", k) for k in range(S5_NLAT)]
    bwd = ([("c", k) for k in range(S5_NCTX - 1, -1, -1)]
           + [("l", k) for k in range(S5_NLAT - 1, -1, -1)])
    for cf, cb in zip(fwd, bwd):
        rf = _s5_chunk_rows(*cf)
        rb = _s5_chunk_rows(*cb)
        hpf_re_ref[rf, :] = h_re
        hpf_im_ref[rf, :] = h_im
        hpb_re_ref[rb, :] = h_re
        hpb_im_ref[rb, :] = h_im
        e_re = jnp.where(is_fwd, he_re_ref[rf, :], he_re_ref[rb, :])
        e_im = jnp.where(is_fwd, he_im_ref[rf, :], he_im_ref[rb, :])
        n_re = ar * h_re - ai * h_im + e_re
        n_im = ar * h_im + ai * h_re + e_im
        h_re, h_im = n_re, n_im

    def chunk_order(ref):
        return jnp.concatenate([ref[pl.ds(dst, n), :] for _, dst, n in _s5_batch_rows()], axis=0)
    hp = jnp.concatenate([chunk_order(r) for r in (hpf_re_ref, hpf_im_ref, hpb_re_ref, hpb_im_ref)],
                         axis=1).astype(BF16)
    y = y + jnp.dot(hp, ft_ref[...], preferred_element_type=F32)
    g = y.T + d_ref[...] * utf
    gl = 0.5 * g * (1.0 + lax.erf(g * (0.5 ** 0.5)))
    o_ref[...] = gl.astype(o_ref.dtype).reshape(S5_LC, S5_CG, S5_N)


def _s5_scan(ut, mats, d_col):
    tc, et, ft, a_re, a_im = mats
    mat = pl.BlockSpec((None, S5_CW, S5_CW), lambda g: (g, 0, 0))
    vec = pl.BlockSpec((None, 1, 2 * S5_P), lambda g: (g, 0, 0))
    io = pl.BlockSpec((None, S5_LC, S5_CG, S5_N), lambda g: (g, 0, 0, 0))
    return pl.pallas_call(
        _s5_kernel,
        grid=(S5_G,),
        in_specs=[io, mat, mat,
                  pl.BlockSpec((None, 2 * S5_CW, S5_CW), lambda g: (g, 0, 0)), vec, vec,
                  pl.BlockSpec((None, S5_CW, 1), lambda g: (g, 0, 0))],
        out_specs=io,
        out_shape=jax.ShapeDtypeStruct((S5_G, S5_LC, S5_CG, S5_N), BF16),
        scratch_shapes=[pltpu.VMEM((S5_ROWS, 2 * S5_P), F32)] * 6,
        compiler_params=_params("parallel"),
        name="s5_scan",
    )(ut, tc, et, ft, a_re, a_im, d_col)


def _glu_kernel(gl_ref, w_ref, b_ref, o_ref):
    gl = gl_ref[...].reshape(S5_W, S5_NT)
    z = jnp.dot(w_ref[...], gl, preferred_element_type=F32) + b_ref[...]
    s = gl.astype(F32) * jax.nn.sigmoid(z)
    o_ref[...] = s.T.astype(o_ref.dtype)


def _s5_glu(glt, w_t, b_col):
    out = pl.pallas_call(
        _glu_kernel,
        grid=(S5_LC, S5_N // S5_NT),
        in_specs=[pl.BlockSpec((S5_G, None, S5_CG, S5_NT), lambda t, n: (0, t, 0, n)),
                  pl.BlockSpec((S5_W, S5_W), lambda t, n: (0, 0)),
                  pl.BlockSpec((S5_W, 1), lambda t, n: (0, 0))],
        out_specs=pl.BlockSpec((S5_NT, S5_W), lambda t, n: (n, t)),
        out_shape=jax.ShapeDtypeStruct((S5_N, S5_LC * S5_W), BF16),
        compiler_params=_params("parallel", "parallel"),
        name="s5_glu",
    )(glt, w_t, b_col)
    return out.reshape(NT, S5_W)


def _rope_tables():
    half = GLA_DK // 2
    freqs = ROPE_BASE ** (-np.arange(0, half, 2, dtype=np.float32) / half)
    out = []
    for n in (ROWS, GRID_W):
        ang = np.arange(n, dtype=np.float32)[:, None] * freqs[None, :]
        c, s = np.cos(ang), np.sin(ang)
        out += [np.concatenate([c, c], axis=-1), np.concatenate([-s, s], axis=-1)]
    return tuple(jnp.asarray(t, F32) for t in out)


def _rope(x, rcos, rsin, ccos, csin):
    x0 = x[:, :128]
    x1 = x[:, 128:]
    return jnp.concatenate([x0 * rcos + pltpu.roll(x0, 64, axis=1) * rsin,
                            x1 * ccos + pltpu.roll(x1, 64, axis=1) * csin], axis=-1)


def _cumsum_rows(x, reverse):
    row = lax.broadcasted_iota(jnp.int32, x.shape, 0)
    s = 1
    while s < GLA_C:
        if s >= 8:
            if reverse:
                x = jnp.concatenate([x[:GLA_C - s] + x[s:], x[GLA_C - s:]], axis=0)
            else:
                x = jnp.concatenate([x[:s], x[s:] + x[:GLA_C - s]], axis=0)
        elif reverse:
            x = x + jnp.where(row < GLA_C - s, pltpu.roll(x, GLA_C - s, axis=0), 0.0)
        else:
            x = x + jnp.where(row >= s, pltpu.roll(x, s, axis=0), 0.0)
        s *= 2
    return x


def _log_sigmoid(x):
    return jnp.minimum(x, 0.0) - jnp.log(1.0 + jnp.exp(-jnp.abs(x)))


def _chunk_rows(c):
    if isinstance(c, int):
        return pl.ds(c * GLA_C, GLA_C)
    return pl.ds(pl.multiple_of(c * GLA_C, GLA_C), GLA_C)


def _gla_kernel(ql_ref, kl_ref, vl_ref, gl_ref, qc_ref, kc_ref, vc_ref, al_ref, ac_ref,
                waf_ref, wab_ref, baf_ref, bab_ref, rcos_ref, rsin_ref, ccos_ref, csin_ref,
                ng_ref, o_ref,
                qi_f, ki_f, ke_f, dec_f, st_f, qi_b, ki_b, ke_b, dec_b, st_b, acc_ref):
    n_ctx = L // GLA_C
    n_lat = S // GLA_C
    qscale = GLA_DK ** -0.5
    fwd = (waf_ref, baf_ref, qi_f, ki_f, ke_f, dec_f, False)
    bwd = (wab_ref, bab_ref, qi_b, ki_b, ke_b, dec_b, True)

    ii = lax.broadcasted_iota(jnp.int32, (GLA_C, GLA_C), 0)
    jj = lax.broadcasted_iota(jnp.int32, (GLA_C, GLA_C), 1)

    def prepare(direction, q, k, a, c):
        wa_ref, ba_ref, qi, ki, ke, dec, reverse = direction
        dst = _chunk_rows(c)
        la = _log_sigmoid(jnp.dot(a, wa_ref[...], preferred_element_type=F32)
                          + ba_ref[...]) / GLA_TAU
        bc = _cumsum_rows(la, reverse)
        b_last = bc[0:1, :] if reverse else bc[GLA_C - 1:GLA_C, :]
        qi[dst, :] = (q * jnp.exp(bc)).astype(BF16)
        ki[dst, :] = (k * jnp.exp(-bc)).astype(BF16)
        ke[dst, :] = (k * jnp.exp(b_last - bc)).astype(BF16)
        dec[pl.ds(c, 1), :] = jnp.exp(b_last)

    def prepare_latent(direction, c):
        r = _chunk_rows(c)
        tabs = (rcos_ref[pl.ds(c, 1), :], rsin_ref[pl.ds(c, 1), :], ccos_ref[...], csin_ref[...])
        q = _rope(ql_ref[r, :].astype(F32) * qscale, *tabs)
        k = _rope(kl_ref[r, :].astype(F32), *tabs)
        prepare(direction, q, k, al_ref[r, :], n_ctx + c)

    def prep_ctx(c, carry):
        r = _chunk_rows(c)
        q = qc_ref[r, :].astype(F32) * qscale
        k = kc_ref[r, :].astype(F32)
        prepare(fwd, q, k, ac_ref[r, :], c)
        prepare(bwd, q, k, ac_ref[r, :], c)
        return carry
    lax.fori_loop(0, n_ctx, prep_ctx, 0)
    prepare_latent(fwd, 0)
    prepare_latent(bwd, n_lat - 1)

    def advance(direction, st_ref, c, v, want_out):
        _, _, qi, ki, ke, dec, reverse = direction
        r = _chunk_rows(c)
        st = st_ref[...]
        o = None
        if want_out:
            q_in = qi[r, :]
            att = lax.dot_general(q_in, ki[r, :], _NT_DIMS, preferred_element_type=F32)
            att = jnp.where((ii <= jj) if reverse else (ii >= jj), att, 0.0)
            o = (jnp.dot(att.astype(BF16), v, preferred_element_type=F32)
                 + lax.dot_general(q_in, st.astype(BF16), _NT_DIMS, preferred_element_type=F32))
        st_ref[...] = dec[pl.ds(c, 1), :] * st + lax.dot_general(
            v, ke[r, :], (((0,), (0,)), ((), ())), preferred_element_type=F32)
        return o

    st_f[...] = jnp.zeros_like(st_f)
    st_b[...] = jnp.zeros_like(st_b)

    def ctx_pair(j, carry):
        cb = n_ctx - 1 - j
        advance(fwd, st_f, j, vc_ref[_chunk_rows(j), :], False)
        advance(bwd, st_b, cb, vc_ref[_chunk_rows(cb), :], False)
        return carry
    lax.fori_loop(0, n_ctx, ctx_pair, 0)

    def lat_pair(j, accumulate):
        cb = n_lat - 1 - j
        rf = _chunk_rows(j)
        rb = _chunk_rows(cb)
        o_f = advance(fwd, st_f, n_ctx + j, vl_ref[rf, :], True)
        o_b = advance(bwd, st_b, n_ctx + cb, vl_ref[rb, :], True)
        if accumulate:
            acc_ref[rf, :] += o_f
            acc_ref[rb, :] += o_b
        else:
            acc_ref[rf, :] = o_f
            acc_ref[rb, :] = o_b
        prepare_latent(fwd, jnp.minimum(j + 1, n_lat - 1))
        prepare_latent(bwd, jnp.maximum(cb - 1, 0))

    def lat_first(j, carry):
        lat_pair(j, False)
        return carry
    lax.fori_loop(0, n_lat // 2, lat_first, 0, unroll=2)

    def lat_second(j, carry):
        lat_pair(j, True)
        return carry
    lax.fori_loop(n_lat // 2, n_lat, lat_second, 0, unroll=2)

    tr = 256

    def fin(t, carry):
        r = pl.ds(pl.multiple_of(t * tr, tr), tr)
        o = acc_ref[r, :]
        ms = jnp.mean(o * o, axis=-1, keepdims=True)
        g = gl_ref[r, :].astype(F32)
        o_ref[r, :] = (o * lax.rsqrt(ms + EPS) * ng_ref[...]
                       * (g * jax.nn.sigmoid(g))).astype(o_ref.dtype)
        return carry
    lax.fori_loop(0, S // tr, fin, 0)


def _gla(qkvg, acode, wa, ba, rope, norm_g):
    ctx0 = NLAT // L
    kq = GLA_QK // GLA_DK
    half = GLA_DK // 2
    n_chunks = (L + S) // GLA_C
    per_dir = [pltpu.VMEM((L + S, GLA_DK), BF16)] * 3 + [pltpu.VMEM((n_chunks, GLA_DK), F32),
                                                         pltpu.VMEM((GLA_DV, GLA_DK), F32)]
    in_specs = [
        pl.BlockSpec((S, GLA_DK), lambda b, h: (b, h)),
        pl.BlockSpec((S, GLA_DK), lambda b, h: (b, kq + h)),
        pl.BlockSpec((S, GLA_DV), lambda b, h: (b, kq + h)),
        pl.BlockSpec((S, GLA_DV), lambda b, h: (b, 2 * kq + h)),
        pl.BlockSpec((L, GLA_DK), lambda b, h: (ctx0 + b, h)),
        pl.BlockSpec((L, GLA_DK), lambda b, h: (ctx0 + b, kq + h)),
        pl.BlockSpec((L, GLA_DV), lambda b, h: (ctx0 + b, kq + h)),
        pl.BlockSpec((S, 128), lambda b, h: (b, 0)),
        pl.BlockSpec((L, 128), lambda b, h: (ctx0 + b, 0)),
        pl.BlockSpec((128, GLA_DK), lambda b, h: (0, h)),
        pl.BlockSpec((128, GLA_DK), lambda b, h: (0, kq + h)),
        pl.BlockSpec((1, GLA_DK), lambda b, h: (0, h)),
        pl.BlockSpec((1, GLA_DK), lambda b, h: (0, kq + h)),
        pl.BlockSpec((ROWS, half), lambda b, h: (0, 0)),
        pl.BlockSpec((ROWS, half), lambda b, h: (0, 0)),
        pl.BlockSpec((GRID_W, half), lambda b, h: (0, 0)),
        pl.BlockSpec((GRID_W, half), lambda b, h: (0, 0)),
        pl.BlockSpec((1, GLA_DV), lambda b, h: (0, 0)),
    ]
    return pl.pallas_call(
        _gla_kernel,
        grid=(B, GLA_H),
        in_specs=in_specs,
        out_specs=pl.BlockSpec((S, GLA_DV), lambda b, h: (b, h)),
        out_shape=jax.ShapeDtypeStruct((NLAT, GLA_VW), BF16),
        scratch_shapes=per_dir + per_dir + [pltpu.VMEM((S, GLA_DV), F32)],
        compiler_params=_params("parallel", "parallel"),
        name="gla",
    )(qkvg, qkvg, qkvg, qkvg, qkvg, qkvg, qkvg, acode, acode, wa, wa, ba, ba, *rope,
      norm_g.reshape(1, GLA_DV))


def kernel(x, c, ctx, c_ctx, ada_w, ada_b, norm1_g, norm2_g, mlp_w1, mlp_w2, final_g, ab_w_in, ab_w_out, na_rel_bias, s5_lambda_re, s5_lambda_im, s5_log_dt, s5_b_re, s5_b_im, s5_c_re, s5_c_im, s5_d, s5_glu_w, s5_glu_b, gla_w_in, gla_w_a2, gla_b_a, gla_norm_g, gla_w_out):
    xs = (x.astype(F32).reshape(NLAT, D), ctx.astype(F32).reshape(NCTX, D))
    cvec = jnp.zeros((16, D), F32).at[:B].set(c.astype(F32)).at[B].set(c_ctx.astype(F32))
    mods = _ada_mod(cvec, ada_w, ada_b).reshape(2, 16, 1, 6 * D)
    bf = lambda w: w.astype(BF16)

    mod = mods[0]
    h, h_tl = _normmod(xs, mod, norm1_g[0], 0, 1, NT)
    w_in = ab_w_in[0]
    qkv = _mm(h, ab_w_in, 0, 3 * NA_W, NT)
    att = _na_attention(qkv, _na_tables(na_rel_bias[0]))
    ut = _s5_uproj(h_tl, bf(w_in[:, 3 * NA_W:].T))
    mats = _s5_matrices(s5_lambda_re[0], s5_lambda_im[0], s5_log_dt[0], s5_b_re[0], s5_b_im[0],
                        s5_c_re[0], s5_c_im[0])
    d_col = jnp.tile(s5_d[0].astype(F32).reshape(S5_G, 1, S5_CG), (1, S5_LC, 1)).reshape(S5_G, S5_CW, 1)
    glt = _s5_scan(ut, mats, d_col)
    s5 = _s5_glu(glt, bf(s5_glu_w[0].T), s5_glu_b[0].astype(F32).reshape(S5_W, 1))
    w2 = bf(mlp_w2)
    xs, h = _mm_res([att, s5], bf(ab_w_out), 0, xs, mod, 2, NT, nxt=(mod, norm2_g[0], 3, 4))
    hid = _mm(h, mlp_w1, 0, MLP_H, NT, relu2=True)
    xs, h = _mm_res([hid], w2, 0, xs, mod, 5, NT, nxt=(mods[1], norm1_g[1], 0, 1))

    mod = mods[1]
    w_in = gla_w_in[0]
    qkvg = _mm(h, gla_w_in, 0, GLA_MAIN, NT)
    w_code = jnp.zeros((1, D, 128), F32).at[0, :, :2 * GLA_RANK].set(w_in[:, GLA_MAIN:])
    acode = _mm(h, w_code, 0, 128, NT)
    wa = (jnp.zeros((128, 2 * GLA_QK), F32)
          .at[:GLA_RANK, :GLA_QK].set(gla_w_a2[0, 0])
          .at[GLA_RANK:2 * GLA_RANK, GLA_QK:].set(gla_w_a2[0, 1]))
    ba = gla_b_a[0].astype(F32).reshape(1, 2 * GLA_QK)
    og = _gla(qkvg, acode, bf(wa), ba, _rope_tables(), gla_norm_g[0].astype(F32))
    xl, h = _mm_res([og], bf(gla_w_out), 0, xs, mod, 2, NLAT, nxt=(mod, norm2_g[1], 3, 4))
    hid = _mm(h, mlp_w1, 1, MLP_H, NLAT, relu2=True)
    out = _mm_res([hid], w2, 1, xl, mod, 5, NLAT, final_g=final_g.astype(F32))
    return out.reshape(B, S, D).astype(x.dtype)
```

```python
import functools
import math

import numpy as np
import jax
import jax.numpy as jnp
from jax import lax
from jax.experimental import pallas as pl
from jax.experimental.pallas import tpu as pltpu

F32 = jnp.float32
BF16 = jnp.bfloat16

D = 2048
B = 8
S = 2048
L = 256
GRID_W = 64
ROWS = S // GRID_W
NLAT = B * S
NCTX = B * L
NT = NLAT + NCTX
MLP_H = 4 * D
EPS = 1e-6
NEG_INF = -1e30

NA_H = 8
NA_DH = 128
NA_W = NA_H * NA_DH
NA_SCALE = NA_DH ** -0.5
NA_HB = 4
NA_QROWS = 4
NA_KROWS = 12
NA_QT = NA_QROWS * GRID_W
NA_KT = NA_KROWS * GRID_W

S5_W = D // 2
S5_CG = 16
S5_G = S5_W // S5_CG
S5_P = 64
S5_LC = 16
S5_CW = S5_LC * S5_CG
S5_NLAT = S // S5_LC
S5_NCTX = L // S5_LC
S5_N = B * (S5_NLAT + S5_NCTX)
S5_NT = S5_N

GLA_H = 4
GLA_DK = 256
GLA_DV = 512
GLA_QK = GLA_H * GLA_DK
GLA_VW = GLA_H * GLA_DV
GLA_RANK = 16
GLA_TAU = 16.0
GLA_C = 64
GLA_MAIN = 2 * GLA_QK + 2 * GLA_VW
ROPE_BASE = 10000.0

MM_RES_NSPLIT = 4

VMEM_LIMIT = 56 * 1024 * 1024

_NT_DIMS = (((1,), (1,)), ((), ()))


def _params(*sem):
    return pltpu.CompilerParams(dimension_semantics=sem, vmem_limit_bytes=VMEM_LIMIT)


def _mod_row(i, tm):
    return jnp.minimum((i * tm) // S, B)


def _ada_kernel(c_ref, w_ref, b_ref, o_ref):
    c = c_ref[...]
    s = c * jax.nn.sigmoid(c)
    o_ref[...] = jnp.dot(s.astype(BF16), w_ref[...].astype(BF16),
                         preferred_element_type=F32) + b_ref[...]


def _ada_mod(cvec, ada_w, ada_b):
    depth = ada_w.shape[0]
    tn = 1024
    return pl.pallas_call(
        _ada_kernel,
        grid=(depth, 6 * D // tn),
        in_specs=[pl.BlockSpec((16, D), lambda l, j: (0, 0)),
                  pl.BlockSpec((None, D, tn), lambda l, j: (l, 0, j)),
                  pl.BlockSpec((None, 1, tn), lambda l, j: (l, 0, j))],
        out_specs=pl.BlockSpec((None, 16, tn), lambda l, j: (l, 0, j)),
        out_shape=jax.ShapeDtypeStruct((depth, 16, 6 * D), F32),
        compiler_params=_params("parallel", "parallel"),
        name="ada_mod",
    )(cvec, ada_w, ada_b.reshape(depth, 1, 6 * D))


def _stream_specs(xs, tm, two_axes):
    if not isinstance(xs, tuple):
        imap = (lambda i, k: (i, 0)) if two_axes else (lambda i: (i, 0))
        return 0, [xs], [pl.BlockSpec((tm, D), imap)]
    nl = xs[0].shape[0] // tm
    if two_axes:
        maps = [lambda i, k: (jnp.minimum(i, nl - 1), 0), lambda i, k: (jnp.maximum(i - nl, 0), 0)]
    else:
        maps = [lambda i: (jnp.minimum(i, nl - 1), 0), lambda i: (jnp.maximum(i - nl, 0), 0)]
    return nl, list(xs), [pl.BlockSpec((tm, D), m) for m in maps]


def _stream_tile(x_refs, n_lat_tiles, rows=slice(None), cols=slice(None)):
    if len(x_refs) == 1:
        return x_refs[0][rows, cols]
    return jnp.where(pl.program_id(0) < n_lat_tiles, x_refs[0][rows, cols], x_refs[1][rows, cols])


def _normmod_kernel(*refs, n_x, n_lat_tiles):
    x_refs = refs[:n_x]
    sh_ref, sc_ref, g_ref, o_ref, oc_ref = refs[n_x:]
    x = _stream_tile(x_refs, n_lat_tiles)
    ms = jnp.mean(x * x, axis=-1, keepdims=True)
    h = x * lax.rsqrt(ms + EPS) * g_ref[...]
    h = h * (1.0 + sc_ref[...]) + sh_ref[...]
    o_ref[...] = h.astype(o_ref.dtype)
    h3 = h.reshape(h.shape[0] // S5_LC, S5_LC, D)
    for tl in range(S5_LC):
        oc_ref[tl] = h3[:, tl, :].astype(oc_ref.dtype)


def _normmod(xs, mod, g, shift_idx, scale_idx, rows):
    tm = 512
    nl, x_args, x_specs = _stream_specs(xs, tm, False)
    return pl.pallas_call(
        functools.partial(_normmod_kernel, n_x=len(x_args), n_lat_tiles=nl),
        grid=(rows // tm,),
        in_specs=x_specs + [
            pl.BlockSpec((None, 1, D), lambda i: (_mod_row(i, tm), 0, shift_idx)),
            pl.BlockSpec((None, 1, D), lambda i: (_mod_row(i, tm), 0, scale_idx)),
            pl.BlockSpec((1, D), lambda i: (0, 0))],
        out_specs=[pl.BlockSpec((tm, D), lambda i: (i, 0)),
                   pl.BlockSpec((S5_LC, tm // S5_LC, D), lambda i: (0, i, 0))],
        out_shape=[jax.ShapeDtypeStruct((rows, D), BF16),
                   jax.ShapeDtypeStruct((S5_LC, rows // S5_LC, D), BF16)],
        compiler_params=_params("parallel"),
        name="normmod",
    )(*x_args, mod, mod, g.reshape(1, D))


def _mm_kernel(a_ref, w_ref, o_ref, wb_ref, *, relu2):
    @pl.when(pl.program_id(1) == 0)
    def _():
        wb_ref[...] = w_ref[...].astype(BF16)

    acc = jnp.dot(a_ref[...], wb_ref[...], preferred_element_type=F32)
    if relu2:
        acc = jnp.square(jnp.maximum(acc, 0.0))
    o_ref[...] = acc.astype(o_ref.dtype)


def _mm(a, w, layer, n, rows, *, relu2=False, tm=2048, tn=1024):
    k = a.shape[1]
    tn = min(tn, n)
    return pl.pallas_call(
        functools.partial(_mm_kernel, relu2=relu2),
        grid=(n // tn, rows // tm),
        in_specs=[pl.BlockSpec((tm, k), lambda j, i: (i, 0)),
                  pl.BlockSpec((None, k, tn), lambda j, i: (layer, 0, j))],
        out_specs=pl.BlockSpec((tm, tn), lambda j, i: (i, j)),
        out_shape=jax.ShapeDtypeStruct((rows, n), BF16),
        scratch_shapes=[pltpu.VMEM((k, tn), BF16)],
        compiler_params=_params("parallel", "arbitrary"),
        name="mm_relu2" if relu2 else "mm",
    )(a, w)


def _mm_res_kernel(*refs, n_lhs, n_res, n_lat_tiles, nk, final_norm, next_norm):
    a_refs = refs[:n_lhs]
    w_refs = refs[n_lhs:2 * n_lhs]
    res_refs = refs[2 * n_lhs:2 * n_lhs + n_res]
    gate_ref = refs[2 * n_lhs + n_res]
    pos = 2 * n_lhs + n_res + 1
    n_extra = 1 if final_norm else (3 if next_norm else 0)
    extra = refs[pos:pos + n_extra]
    o_ref = refs[pos + n_extra]
    h_ref = refs[pos + n_extra + 1] if next_norm else None

    def normed(y, coef):
        ms = jnp.mean(y * y, axis=-1, keepdims=True)
        return y * lax.rsqrt(ms + EPS) * coef

    def epilogue(rows):
        if final_norm:
            o_ref[rows, :] = normed(o_ref[rows, :], extra[0][...])
        elif next_norm:
            sh_ref, sc_ref, g_ref = extra
            coef = g_ref[...] * (1.0 + sc_ref[...])
            h_ref[rows, :] = (normed(o_ref[rows, :], coef) + sh_ref[...]).astype(h_ref.dtype)

    wn = D // MM_RES_NSPLIT

    def update(rows, first):
        for cc in range(MM_RES_NSPLIT):
            cols = slice(cc * wn, (cc + 1) * wn)
            part = None
            for a_ref, w_ref in zip(a_refs, w_refs):
                d = jnp.dot(a_ref[rows, :], w_ref[:, cols], preferred_element_type=F32)
                part = d if part is None else part + d
            part = gate_ref[:, cols] * part
            if first:
                o_ref[rows, cols] = _stream_tile(res_refs, n_lat_tiles, rows, cols) + part
            else:
                o_ref[rows, cols] += part

    tm = o_ref.shape[0]
    if nk == 1:
        for rh in range(2):
            rows = slice(rh * (tm // 2), (rh + 1) * (tm // 2))
            update(rows, True)
            epilogue(rows)
        return

    kk = pl.program_id(1)
    every = slice(0, tm)

    @pl.when(kk == 0)
    def _():
        o_ref[...] = _stream_tile(res_refs, n_lat_tiles)

    update(every, False)
    if final_norm or next_norm:
        pl.when(kk == nk - 1)(functools.partial(epilogue, every))


def _mm_res(a_list, w, layer, resid, mod, gate_idx, rows, *, final_g=None, nxt=None,
            tm=512, tk=2048):
    n_lhs = len(a_list)
    kdim = a_list[0].shape[1]
    tk = min(kdim, tk)
    nk = kdim // tk
    nl, res_args, res_specs = _stream_specs(resid, tm, True)

    def mod_spec(idx):
        return pl.BlockSpec((None, 1, D), lambda i, k: (_mod_row(i, tm), 0, idx))

    def w_spec(j):
        return pl.BlockSpec((None, tk, D), lambda i, k: (layer, j * nk + k, 0))

    row_spec = pl.BlockSpec((tm, D), lambda i, k: (i, 0))
    vec_spec = pl.BlockSpec((1, D), lambda i, k: (0, 0))
    in_specs = ([pl.BlockSpec((tm, tk), lambda i, k: (i, k)) for _ in a_list]
                + [w_spec(j) for j in range(n_lhs)] + res_specs + [mod_spec(gate_idx)])
    args = list(a_list) + [w] * n_lhs + res_args + [mod]
    out_specs, out_shape = row_spec, jax.ShapeDtypeStruct((rows, D), F32)
    if final_g is not None:
        in_specs.append(vec_spec)
        args.append(final_g.reshape(1, D))
    elif nxt is not None:
        mod_n, g_n, shift_idx, scale_idx = nxt
        in_specs += [mod_spec(shift_idx), mod_spec(scale_idx), vec_spec]
        args += [mod_n, mod_n, g_n.reshape(1, D)]
        out_specs = [row_spec, row_spec]
        out_shape = [out_shape, jax.ShapeDtypeStruct((rows, D), BF16)]
    return pl.pallas_call(
        functools.partial(_mm_res_kernel, n_lhs=n_lhs, n_res=len(res_args), n_lat_tiles=nl,
                          nk=nk, final_norm=final_g is not None, next_norm=nxt is not None),
        grid=(rows // tm, nk),
        in_specs=in_specs,
        out_specs=out_specs,
        out_shape=out_shape,
        compiler_params=_params("parallel", "arbitrary"),
        name="mm_res",
    )(*args)


def _na_tables(rel_bias):
    hp = lax.Precision.HIGHEST
    cq = np.arange(GRID_W)[:, None]
    ck = np.arange(GRID_W)[None, :]
    ws = np.clip(cq - 8, 0, GRID_W - 16)
    col_ok = (ck >= ws) & (ck < ws + 16)
    col_hot = ((ck - cq + 15)[..., None] == np.arange(31)) & col_ok[..., None]
    blocks = jnp.einsum('hrj,qkj->hrqk', rel_bias.astype(F32), col_hot.astype(np.float32),
                        precision=hp)
    blocks = blocks + np.where(col_ok, 0.0, NEG_INF).astype(np.float32)
    masked = jnp.full((NA_H, GRID_W, GRID_W), NEG_INF, F32)
    pats = []
    for r0, start in ((0, 0), (NA_QROWS, 0), (ROWS - NA_QROWS, ROWS - NA_KROWS)):
        rows = []
        for a in range(NA_QROWS):
            r = r0 + a
            rs = min(max(r - 4, 0), ROWS - 8)
            rows.append(jnp.concatenate(
                [blocks[:, krow - r + 7] if rs <= krow < rs + 8 else masked
                 for krow in range(start, start + NA_KROWS)], axis=-1))
        pats.append(jnp.concatenate(rows, axis=1))
    return jnp.stack(pats, axis=1)


def _na_kernel(q_ref, k0_ref, k1_ref, k2_ref, v0_ref, v1_ref, v2_ref, kc_ref, vc_ref,
               tab_ref, o_ref):
    i = pl.program_id(1)

    def head(ref, hh):
        return ref[:, hh * NA_DH:(hh + 1) * NA_DH]

    def ctx_scores(hh):
        return lax.dot_general(head(q_ref, hh), head(kc_ref, hh), _NT_DIMS,
                               preferred_element_type=F32) * NA_SCALE

    @pl.when(i < ROWS // NA_QROWS)
    def _():
        for hh in range(NA_HB):
            q = head(q_ref, hh)
            s_c = ctx_scores(hh)
            m = jnp.max(s_c, axis=-1, keepdims=True)
            s_w = []
            for d, k_ref in enumerate((k0_ref, k1_ref, k2_ref)):
                s = lax.dot_general(q, head(k_ref, hh), _NT_DIMS,
                                    preferred_element_type=F32) * NA_SCALE
                s = s + tab_ref[hh, :, d * NA_QT:(d + 1) * NA_QT]
                s_w.append(s)
                m = jnp.maximum(m, jnp.max(s, axis=-1, keepdims=True))
            p_c = jnp.exp(s_c - m)
            l = jnp.sum(p_c, axis=-1, keepdims=True)
            o = jnp.dot(p_c.astype(BF16), head(vc_ref, hh), preferred_element_type=F32)
            for s, v_ref in zip(s_w, (v0_ref, v1_ref, v2_ref)):
                p = jnp.exp(s - m)
                l = l + jnp.sum(p, axis=-1, keepdims=True)
                o = o + jnp.dot(p.astype(BF16), head(v_ref, hh), preferred_element_type=F32)
            o_ref[:, hh * NA_DH:(hh + 1) * NA_DH] = (o / l).astype(o_ref.dtype)

    @pl.when(i == ROWS // NA_QROWS)
    def _():
        for hh in range(NA_HB):
            s_c = ctx_scores(hh)
            p_c = jnp.exp(s_c - jnp.max(s_c, axis=-1, keepdims=True))
            l = jnp.sum(p_c, axis=-1, keepdims=True)
            o = jnp.dot(p_c.astype(BF16), head(vc_ref, hh), preferred_element_type=F32)
            o_ref[:, hh * NA_DH:(hh + 1) * NA_DH] = (o / l).astype(o_ref.dtype)


def _na_attention(qkv, table):
    ng = ROWS // NA_QROWS
    blk = S // NA_QT
    ctx0 = NLAT // NA_QT

    def qrow(h, i, b):
        return jnp.where(i < ng, b * blk + i, ctx0 + b)

    nhb = NA_H // NA_HB

    def krow(d):
        return lambda h, i, b: (b * blk + jnp.clip(i - 1, 0, blk - 3) + d, nhb + h)

    def vrow(d):
        return lambda h, i, b: (b * blk + jnp.clip(i - 1, 0, blk - 3) + d, 2 * nhb + h)

    def pat(h, i, b):
        return (h, jnp.where(i == 0, 0, jnp.where(i >= ng - 1, 2, 1)), 0, 0)

    tile = (NA_QT, NA_HB * NA_DH)
    in_specs = ([pl.BlockSpec(tile, lambda h, i, b: (qrow(h, i, b), h))]
                + [pl.BlockSpec(tile, krow(d)) for d in range(3)]
                + [pl.BlockSpec(tile, vrow(d)) for d in range(3)]
                + [pl.BlockSpec(tile, lambda h, i, b: (ctx0 + b, nhb + h)),
                   pl.BlockSpec(tile, lambda h, i, b: (ctx0 + b, 2 * nhb + h)),
                   pl.BlockSpec((NA_HB, None, NA_QT, NA_KT), pat)])
    return pl.pallas_call(
        _na_kernel,
        grid=(nhb, ng + 1, B),
        in_specs=in_specs,
        out_specs=pl.BlockSpec(tile, lambda h, i, b: (qrow(h, i, b), h)),
        out_shape=jax.ShapeDtypeStruct((NT, NA_W), BF16),
        compiler_params=_params("parallel", "parallel", "parallel"),
        name="na_attention",
    )(*([qkv] * 9), table)


def _s5_matrices(lam_re, lam_im, log_dt, b_re, b_im, c_re, c_im):
    lam_re, lam_im = lam_re.astype(F32), lam_im.astype(F32)
    b_re, b_im = b_re.astype(F32), b_im.astype(F32)
    c_re, c_im = c_re.astype(F32), c_im.astype(F32)
    dt = jnp.exp(log_dt.astype(F32))[..., None]
    mag = jnp.exp(lam_re * dt)
    a_re = mag * jnp.cos(lam_im * dt)
    a_im = mag * jnp.sin(lam_im * dt)
    den = lam_re * lam_re + lam_im * lam_im
    f_re = ((a_re - 1.0) * lam_re + a_im * lam_im) / den
    f_im = (a_im * lam_re - (a_re - 1.0) * lam_im) / den
    bb_re = f_re[..., None] * b_re - f_im[..., None] * b_im
    bb_im = f_re[..., None] * b_im + f_im[..., None] * b_re

    pw_re, pw_im = jnp.ones_like(a_re)[None], jnp.zeros_like(a_im)[None]
    an_re, an_im = a_re, a_im
    while pw_re.shape[0] < S5_LC:
        pw_re, pw_im = (jnp.concatenate([pw_re, pw_re * an_re - pw_im * an_im]),
                        jnp.concatenate([pw_im, pw_re * an_im + pw_im * an_re]))
        an_re, an_im = an_re * an_re - an_im * an_im, 2.0 * an_re * an_im
    pw_re = jnp.concatenate([pw_re, an_re[None]])
    pw_im = jnp.concatenate([pw_im, an_im[None]])

    ab_re = pw_re[..., None] * bb_re[None] - pw_im[..., None] * bb_im[None]
    ab_im = pw_re[..., None] * bb_im[None] + pw_im[..., None] * bb_re[None]
    def lag_minor(z, d):
        return z[:S5_LC, d].transpose(1, 2, 0, 3).reshape(S5_G, 1, S5_P, S5_CW)
    def kernels(d):
        k = jnp.sum(c_re[d][..., None] * lag_minor(ab_re, d)
                    - c_im[d][..., None] * lag_minor(ab_im, d), axis=2)
        return k.reshape(S5_G, S5_CG, S5_LC, S5_CG).transpose(0, 3, 2, 1)
    zpad = jnp.zeros((S5_G, S5_CG, S5_LC, S5_CG), F32)
    kf = jnp.concatenate([zpad, kernels(0)], axis=2)
    kb = jnp.concatenate([jnp.flip(kernels(1), axis=2), zpad], axis=2)
    tc = jnp.stack([kf[:, :, S5_LC - s:2 * S5_LC - s] + kb[:, :, S5_LC - 1 - s:2 * S5_LC - 1 - s]
                    for s in range(S5_LC)], axis=1).reshape(S5_G, S5_CW, S5_CW)

    def st(arr, d, flip):
        z = arr[:S5_LC, d]
        z = jnp.flip(z, axis=0) if flip else z
        return z.transpose(1, 0, 3, 2).reshape(S5_G, S5_CW, S5_P)
    et = jnp.concatenate([st(ab_re, 0, True), st(ab_re, 1, False),
                          st(ab_im, 0, True), st(ab_im, 1, False)], axis=-1)

    def rd(d, flip):
        pr_ = pw_re[1:, d]
        pi_ = pw_im[1:, d]
        if flip:
            pr_, pi_ = jnp.flip(pr_, axis=0), jnp.flip(pi_, axis=0)
        cr = c_re[d][None] * pr_[:, :, None, :] - c_im[d][None] * pi_[:, :, None, :]
        ci = c_re[d][None] * pi_[:, :, None, :] + c_im[d][None] * pr_[:, :, None, :]
        to = lambda z: z.transpose(1, 3, 0, 2).reshape(S5_G, S5_P, S5_CW)
        return to(cr), to(-ci)
    fr, fi = rd(0, False)
    br, bi = rd(1, True)
    z = jnp.zeros_like(fr)
    ft = jnp.concatenate([fr, z, fi, z, z, br, z, bi], axis=1)

    a16_re = jnp.concatenate([pw_re[S5_LC, 0], pw_re[S5_LC, 1]], axis=-1)[:, None, :]
    a16_im = jnp.concatenate([pw_im[S5_LC, 0], pw_im[S5_LC, 1]], axis=-1)[:, None, :]
    return tc.astype(BF16), et.astype(BF16), ft.astype(BF16), a16_re, a16_im


def _uproj_kernel(wt_ref, h_ref, o_ref):
    acc = lax.dot_general(wt_ref[...], h_ref[...], _NT_DIMS, preferred_element_type=F32)
    o_ref[...] = acc.reshape(S5_G, S5_CG, S5_NT).astype(o_ref.dtype)


def _s5_uproj(h_tl, w_u_t):
    return pl.pallas_call(
        _uproj_kernel,
        grid=(S5_LC, S5_N // S5_NT),
        in_specs=[pl.BlockSpec((S5_W, D), lambda t, n: (0, 0)),
                  pl.BlockSpec((None, S5_NT, D), lambda t, n: (t, n, 0))],
        out_specs=pl.BlockSpec((S5_G, None, S5_CG, S5_NT), lambda t, n: (0, t, 0, n)),
        out_shape=jax.ShapeDtypeStruct((S5_G, S5_LC, S5_CG, S5_N), BF16),
        compiler_params=_params("parallel", "parallel"),
        name="s5_uproj",
    )(w_u_t, h_tl)


S5_PL = S5_NLAT + 4
S5_PC = S5_NCTX + 4
S5_CB = B * S5_PL
S5_ROWS = S5_CB + B * S5_PC


def _s5_chunk_rows(kind, k):
    if kind == "c":
        return pl.ds(S5_CB + k, B, stride=S5_PC)
    return pl.ds(k, B, stride=S5_PL)


def _s5_batch_rows():
    runs = [(b * S5_NLAT, b * S5_PL, S5_NLAT) for b in range(B)]
    runs += [(B * S5_NLAT + b * S5_NCTX, S5_CB + b * S5_PC, S5_NCTX) for b in range(B)]
    return runs


def _s5_kernel(ut_ref, tc_ref, et_ref, ft_ref, ar_ref, ai_ref, d_ref, o_ref,
               he_re_ref, he_im_ref, hpf_re_ref, hpf_im_ref, hpb_re_ref, hpb_im_ref):
    sw = 2 * S5_P
    utf = ut_ref[...].reshape(S5_CW, S5_N).astype(F32)
    un = utf.T.astype(BF16)
    y = jnp.dot(un, tc_ref[...], preferred_element_type=F32)
    he = jnp.dot(un, et_ref[...], preferred_element_type=F32)
    for src, dst, n in _s5_batch_rows():
        he_re_ref[pl.ds(dst, n), :] = he[src:src + n, :sw]
        he_im_ref[pl.ds(dst, n), :] = he[src:src + n, sw:]

    ar = ar_ref[...]
    ai = ai_ref[...]
    is_fwd = lax.broadcasted_iota(jnp.int32, (B, 2 * S5_P), 1) < S5_P
    h_re = jnp.zeros((B, 2 * S5_P), F32)
    h_im = jnp.zeros((B, 2 * S5_P), F32)
    fwd = [("c", k) for k in range(S5_NCTX)] + [("l", k) for k in range(S5_NLAT)]
    bwd = ([("c", k) for k in range(S5_NCTX - 1, -1, -1)]
           + [("l", k) for k in range(S5_NLAT - 1, -1, -1)])
    for cf, cb in zip(fwd, bwd):
        rf = _s5_chunk_rows(*cf)
        rb = _s5_chunk_rows(*cb)
        hpf_re_ref[rf, :] = h_re
        hpf_im_ref[rf, :] = h_im
        hpb_re_ref[rb, :] = h_re
        hpb_im_ref[rb, :] = h_im
        e_re = jnp.where(is_fwd, he_re_ref[rf, :], he_re_ref[rb, :])
        e_im = jnp.where(is_fwd, he_im_ref[rf, :], he_im_ref[rb, :])
        n_re = ar * h_re - ai * h_im + e_re
        n_im = ar * h_im + ai * h_re + e_im
        h_re, h_im = n_re, n_im

    def chunk_order(ref):
        return jnp.concatenate([ref[pl.ds(dst, n), :] for _, dst, n in _s5_batch_rows()], axis=0)
    hp = jnp.concatenate([chunk_order(r) for r in (hpf_re_ref, hpf_im_ref, hpb_re_ref, hpb_im_ref)],
                         axis=1).astype(BF16)
    y = y + jnp.dot(hp, ft_ref[...], preferred_element_type=F32)
    g = y.T + d_ref[...] * utf
    gl = 0.5 * g * (1.0 + lax.erf(g * (0.5 ** 0.5)))
    o_ref[...] = gl.astype(o_ref.dtype).reshape(S5_LC, S5_CG, S5_N)


def _s5_scan(ut, mats, d_col):
    tc, et, ft, a_re, a_im = mats
    mat = pl.BlockSpec((None, S5_CW, S5_CW), lambda g: (g, 0, 0))
    vec = pl.BlockSpec((None, 1, 2 * S5_P), lambda g: (g, 0, 0))
    io = pl.BlockSpec((None, S5_LC, S5_CG, S5_N), lambda g: (g, 0, 0, 0))
    return pl.pallas_call(
        _s5_kernel,
        grid=(S5_G,),
        in_specs=[io, mat, mat,
                  pl.BlockSpec((None, 2 * S5_CW, S5_CW), lambda g: (g, 0, 0)), vec, vec,
                  pl.BlockSpec((None, S5_CW, 1), lambda g: (g, 0, 0))],
        out_specs=io,
        out_shape=jax.ShapeDtypeStruct((S5_G, S5_LC, S5_CG, S5_N), BF16),
        scratch_shapes=[pltpu.VMEM((S5_ROWS, 2 * S5_P), F32)] * 6,
        compiler_params=_params("parallel"),
        name="s5_scan",
    )(ut, tc, et, ft, a_re, a_im, d_col)


def _glu_kernel(gl_ref, w_ref, b_ref, o_ref):
    gl = gl_ref[...].reshape(S5_W, S5_NT)
    z = jnp.dot(w_ref[...], gl, preferred_element_type=F32) + b_ref[...]
    s = gl.astype(F32) * jax.nn.sigmoid(z)
    o_ref[...] = s.T.astype(o_ref.dtype)


def _s5_glu(glt, w_t, b_col):
    out = pl.pallas_call(
        _glu_kernel,
        grid=(S5_LC, S5_N // S5_NT),
        in_specs=[pl.BlockSpec((S5_G, None, S5_CG, S5_NT), lambda t, n: (0, t, 0, n)),
                  pl.BlockSpec((S5_W, S5_W), lambda t, n: (0, 0)),
                  pl.BlockSpec((S5_W, 1), lambda t, n: (0, 0))],
        out_specs=pl.BlockSpec((S5_NT, S5_W), lambda t, n: (n, t)),
        out_shape=jax.ShapeDtypeStruct((S5_N, S5_LC * S5_W), BF16),
        compiler_params=_params("parallel", "parallel"),
        name="s5_glu",
    )(glt, w_t, b_col)
    return out.reshape(NT, S5_W)


def _rope_tables():
    half = GLA_DK // 2
    freqs = ROPE_BASE ** (-np.arange(0, half, 2, dtype=np.float32) / half)
    out = []
    for n in (ROWS, GRID_W):
        ang = np.arange(n, dtype=np.float32)[:, None] * freqs[None, :]
        c, s = np.cos(ang), np.sin(ang)
        out += [np.concatenate([c, c], axis=-1), np.concatenate([-s, s], axis=-1)]
    return tuple(jnp.asarray(t, F32) for t in out)


def _rope(x, rcos, rsin, ccos, csin):
    x0 = x[:, :128]
    x1 = x[:, 128:]
    return jnp.concatenate([x0 * rcos + pltpu.roll(x0, 64, axis=1) * rsin,
                            x1 * ccos + pltpu.roll(x1, 64, axis=1) * csin], axis=-1)


def _cumsum_rows(x, reverse):
    row = lax.broadcasted_iota(jnp.int32, x.shape, 0)
    s = 1
    while s < GLA_C:
        if s >= 8:
            if reverse:
                x = jnp.concatenate([x[:GLA_C - s] + x[s:], x[GLA_C - s:]], axis=0)
            else:
                x = jnp.concatenate([x[:s], x[s:] + x[:GLA_C - s]], axis=0)
        elif reverse:
            x = x + jnp.where(row < GLA_C - s, pltpu.roll(x, GLA_C - s, axis=0), 0.0)
        else:
            x = x + jnp.where(row >= s, pltpu.roll(x, s, axis=0), 0.0)
        s *= 2
    return x


def _log_sigmoid(x):
    return jnp.minimum(x, 0.0) - jnp.log(1.0 + jnp.exp(-jnp.abs(x)))


def _chunk_rows(c):
    if isinstance(c, int):
        return pl.ds(c * GLA_C, GLA_C)
    return pl.ds(pl.multiple_of(c * GLA_C, GLA_C), GLA_C)


def _gla_kernel(ql_ref, kl_ref, vl_ref, gl_ref, qc_ref, kc_ref, vc_ref, al_ref, ac_ref,
                waf_ref, wab_ref, baf_ref, bab_ref, rcos_ref, rsin_ref, ccos_ref, csin_ref,
                ng_ref, o_ref,
                qi_f, ki_f, ke_f, dec_f, st_f, qi_b, ki_b, ke_b, dec_b, st_b, acc_ref):
    n_ctx = L // GLA_C
    n_lat = S // GLA_C
    qscale = GLA_DK ** -0.5
    fwd = (waf_ref, baf_ref, qi_f, ki_f, ke_f, dec_f, False)
    bwd = (wab_ref, bab_ref, qi_b, ki_b, ke_b, dec_b, True)

    ii = lax.broadcasted_iota(jnp.int32, (GLA_C, GLA_C), 0)
    jj = lax.broadcasted_iota(jnp.int32, (GLA_C, GLA_C), 1)

    def prepare(direction, q, k, a, c):
        wa_ref, ba_ref, qi, ki, ke, dec, reverse = direction
        dst = _chunk_rows(c)
        la = _log_sigmoid(jnp.dot(a, wa_ref[...], preferred_element_type=F32)
                          + ba_ref[...]) / GLA_TAU
        bc = _cumsum_rows(la, reverse)
        b_last = bc[0:1, :] if reverse else bc[GLA_C - 1:GLA_C, :]
        qi[dst, :] = (q * jnp.exp(bc)).astype(BF16)
        ki[dst, :] = (k * jnp.exp(-bc)).astype(BF16)
        ke[dst, :] = (k * jnp.exp(b_last - bc)).astype(BF16)
        dec[pl.ds(c, 1), :] = jnp.exp(b_last)

    def prepare_latent(direction, c):
        r = _chunk_rows(c)
        tabs = (rcos_ref[pl.ds(c, 1), :], rsin_ref[pl.ds(c, 1), :], ccos_ref[...], csin_ref[...])
        q = _rope(ql_ref[r, :].astype(F32) * qscale, *tabs)
        k = _rope(kl_ref[r, :].astype(F32), *tabs)
        prepare(direction, q, k, al_ref[r, :], n_ctx + c)

    def prepare_context(direction, c):
        r = _chunk_rows(c)
        prepare(direction, qc_ref[r, :].astype(F32) * qscale, kc_ref[r, :].astype(F32),
                ac_ref[r, :], c)

    prepare_context(fwd, 0)
    prepare_context(bwd, n_ctx - 1)
    prepare_latent(fwd, 0)
    prepare_latent(bwd, n_lat - 1)

    def advance(direction, st_ref, c, v, want_out):
        _, _, qi, ki, ke, dec, reverse = direction
        r = _chunk_rows(c)
        st = st_ref[...]
        o = None
        if want_out:
            q_in = qi[r, :]
            att = lax.dot_general(q_in, ki[r, :], _NT_DIMS, preferred_element_type=F32)
            att = jnp.where((ii <= jj) if reverse else (ii >= jj), att, 0.0)
            o = (jnp.dot(att.astype(BF16), v, preferred_element_type=F32)
                 + lax.dot_general(q_in, st.astype(BF16), _NT_DIMS, preferred_element_type=F32))
        st_ref[...] = dec[pl.ds(c, 1), :] * st + lax.dot_general(
            v, ke[r, :], (((0,), (0,)), ((), ())), preferred_element_type=F32)
        return o

    st_f[...] = jnp.zeros_like(st_f)
    st_b[...] = jnp.zeros_like(st_b)

    def ctx_pair(j, carry):
        cb = n_ctx - 1 - j
        advance(fwd, st_f, j, vc_ref[_chunk_rows(j), :], False)
        advance(bwd, st_b, cb, vc_ref[_chunk_rows(cb), :], False)
        prepare_context(fwd, jnp.minimum(j + 1, n_ctx - 1))
        prepare_context(bwd, jnp.maximum(cb - 1, 0))
        return carry
    lax.fori_loop(0, n_ctx, ctx_pair, 0, unroll=2)

    def lat_pair(j, accumulate):
        cb = n_lat - 1 - j
        rf = _chunk_rows(j)
        rb = _chunk_rows(cb)
        o_f = advance(fwd, st_f, n_ctx + j, vl_ref[rf, :], True)
        o_b = advance(bwd, st_b, n_ctx + cb, vl_ref[rb, :], True)
        if accumulate:
            acc_ref[rf, :] += o_f
            acc_ref[rb, :] += o_b
        else:
            acc_ref[rf, :] = o_f
            acc_ref[rb, :] = o_b
        prepare_latent(fwd, jnp.minimum(j + 1, n_lat - 1))
        prepare_latent(bwd, jnp.maximum(cb - 1, 0))

    def lat_first(j, carry):
        lat_pair(j, False)
        return carry
    lax.fori_loop(0, n_lat // 2, lat_first, 0, unroll=2)

    def lat_second(j, carry):
        lat_pair(j, True)
        return carry
    lax.fori_loop(n_lat // 2, n_lat, lat_second, 0, unroll=2)

    tr = 256

    def fin(t, carry):
        r = pl.ds(pl.multiple_of(t * tr, tr), tr)
        o = acc_ref[r, :]
        ms = jnp.mean(o * o, axis=-1, keepdims=True)
        g = gl_ref[r, :].astype(F32)
        o_ref[r, :] = (o * lax.rsqrt(ms + EPS) * ng_ref[...]
                       * (g * jax.nn.sigmoid(g))).astype(o_ref.dtype)
        return carry
    lax.fori_loop(0, S // tr, fin, 0)


def _gla(qkvg, acode, wa, ba, rope, norm_g):
    ctx0 = NLAT // L
    kq = GLA_QK // GLA_DK
    half = GLA_DK // 2
    n_chunks = (L + S) // GLA_C
    per_dir = [pltpu.VMEM((L + S, GLA_DK), BF16)] * 3 + [pltpu.VMEM((n_chunks, GLA_DK), F32),
                                                         pltpu.VMEM((GLA_DV, GLA_DK), F32)]
    in_specs = [
        pl.BlockSpec((S, GLA_DK), lambda b, h: (b, h)),
        pl.BlockSpec((S, GLA_DK), lambda b, h: (b, kq + h)),
        pl.BlockSpec((S, GLA_DV), lambda b, h: (b, kq + h)),
        pl.BlockSpec((S, GLA_DV), lambda b, h: (b, 2 * kq + h)),
        pl.BlockSpec((L, GLA_DK), lambda b, h: (ctx0 + b, h)),
        pl.BlockSpec((L, GLA_DK), lambda b, h: (ctx0 + b, kq + h)),
        pl.BlockSpec((L, GLA_DV), lambda b, h: (ctx0 + b, kq + h)),
        pl.BlockSpec((S, 128), lambda b, h: (b, 0)),
        pl.BlockSpec((L, 128), lambda b, h: (ctx0 + b, 0)),
        pl.BlockSpec((128, GLA_DK), lambda b, h: (0, h)),
        pl.BlockSpec((128, GLA_DK), lambda b, h: (0, kq + h)),
        pl.BlockSpec((1, GLA_DK), lambda b, h: (0, h)),
        pl.BlockSpec((1, GLA_DK), lambda b, h: (0, kq + h)),
        pl.BlockSpec((ROWS, half), lambda b, h: (0, 0)),
        pl.BlockSpec((ROWS, half), lambda b, h: (0, 0)),
        pl.BlockSpec((GRID_W, half), lambda b, h: (0, 0)),
        pl.BlockSpec((GRID_W, half), lambda b, h: (0, 0)),
        pl.BlockSpec((1, GLA_DV), lambda b, h: (0, 0)),
    ]
    return pl.pallas_call(
        _gla_kernel,
        grid=(B, GLA_H),
        in_specs=in_specs,
        out_specs=pl.BlockSpec((S, GLA_DV), lambda b, h: (b, h)),
        out_shape=jax.ShapeDtypeStruct((NLAT, GLA_VW), BF16),
        scratch_shapes=per_dir + per_dir + [pltpu.VMEM((S, GLA_DV), F32)],
        compiler_params=_params("parallel", "parallel"),
        name="gla",
    )(qkvg, qkvg, qkvg, qkvg, qkvg, qkvg, qkvg, acode, acode, wa, wa, ba, ba, *rope,
      norm_g.reshape(1, GLA_DV))


def kernel(x, c, ctx, c_ctx, ada_w, ada_b, norm1_g, norm2_g, mlp_w1, mlp_w2, final_g, ab_w_in, ab_w_out, na_rel_bias, s5_lambda_re, s5_lambda_im, s5_log_dt, s5_b_re, s5_b_im, s5_c_re, s5_c_im, s5_d, s5_glu_w, s5_glu_b, gla_w_in, gla_w_a2, gla_b_a, gla_norm_g, gla_w_out):
    xs = (x.astype(F32).reshape(NLAT, D), ctx.astype(F32).reshape(NCTX, D))
    cvec = jnp.zeros((16, D), F32).at[:B].set(c.astype(F32)).at[B].set(c_ctx.astype(F32))
    mods = _ada_mod(cvec, ada_w, ada_b).reshape(2, 16, 1, 6 * D)
    bf = lambda w: w.astype(BF16)

    mod = mods[0]
    h, h_tl = _normmod(xs, mod, norm1_g[0], 0, 1, NT)
    w_in = ab_w_in[0]
    qkv = _mm(h, ab_w_in, 0, 3 * NA_W, NT)
    att = _na_attention(qkv, _na_tables(na_rel_bias[0]))
    ut = _s5_uproj(h_tl, bf(w_in[:, 3 * NA_W:].T))
    mats = _s5_matrices(s5_lambda_re[0], s5_lambda_im[0], s5_log_dt[0], s5_b_re[0], s5_b_im[0],
                        s5_c_re[0], s5_c_im[0])
    d_col = jnp.tile(s5_d[0].astype(F32).reshape(S5_G, 1, S5_CG), (1, S5_LC, 1)).reshape(S5_G, S5_CW, 1)
    glt = _s5_scan(ut, mats, d_col)
    s5 = _s5_glu(glt, bf(s5_glu_w[0].T), s5_glu_b[0].astype(F32).reshape(S5_W, 1))
    w2 = bf(mlp_w2)
    xs, h = _mm_res([att, s5], bf(ab_w_out), 0, xs, mod, 2, NT, nxt=(mod, norm2_g[0], 3, 4))
    hid = _mm(h, mlp_w1, 0, MLP_H, NT, relu2=True)
    xs, h = _mm_res([hid], w2, 0, xs, mod, 5, NT, nxt=(mods[1], norm1_g[1], 0, 1))

    mod = mods[1]
    w_in = gla_w_in[0]
    qkvg = _mm(h, gla_w_in, 0, GLA_MAIN, NT)
    w_code = jnp.zeros((1, D, 128), F32).at[0, :, :2 * GLA_RANK].set(w_in[:, GLA_MAIN:])
    acode = _mm(h, w_code, 0, 128, NT)
    wa = (jnp.zeros((128, 2 * GLA_QK), F32)
          .at[:GLA_RANK, :GLA_QK].set(gla_w_a2[0, 0])
          .at[GLA_RANK:2 * GLA_RANK, GLA_QK:].set(gla_w_a2[0, 1]))
    ba = gla_b_a[0].astype(F32).reshape(1, 2 * GLA_QK)
    og = _gla(qkvg, acode, bf(wa), ba, _rope_tables(), gla_norm_g[0].astype(F32))
    xl, h = _mm_res([og], bf(gla_w_out), 0, xs, mod, 2, NLAT, nxt=(mod, norm2_g[1], 3, 4))
    hid = _mm(h, mlp_w1, 1, MLP_H, NLAT, relu2=True)
    out = _mm_res([hid], w2, 1, xl, mod, 5, NLAT, final_g=final_g.astype(F32))
    return out.reshape(B, S, D).astype(x.dtype)
```

```python
import functools
import math

import numpy as np
import jax
import jax.numpy as jnp
from jax import lax
from jax.experimental import pallas as pl
from jax.experimental.pallas import tpu as pltpu

F32 = jnp.float32
BF16 = jnp.bfloat16

D = 2048
B = 8
S = 2048
L = 256
GRID_W = 64
ROWS = S // GRID_W
NLAT = B * S
NCTX = B * L
NT = NLAT + NCTX
MLP_H = 4 * D
EPS = 1e-6
NEG_INF = -1e30

NA_H = 8
NA_DH = 128
NA_W = NA_H * NA_DH
NA_SCALE = NA_DH ** -0.5
NA_HB = 4
NA_QROWS = 4
NA_KROWS = 12
NA_QT = NA_QROWS * GRID_W
NA_KT = NA_KROWS * GRID_W

S5_W = D // 2
S5_CG = 16
S5_G = S5_W // S5_CG
S5_P = 64
S5_LC = 16
S5_CW = S5_LC * S5_CG
S5_NLAT = S // S5_LC
S5_NCTX = L // S5_LC
S5_N = B * (S5_NLAT + S5_NCTX)
S5_NT = S5_N

GLA_H = 4
GLA_DK = 256
GLA_DV = 512
GLA_QK = GLA_H * GLA_DK
GLA_VW = GLA_H * GLA_DV
GLA_RANK = 16
GLA_TAU = 16.0
GLA_C = 64
GLA_MAIN = 2 * GLA_QK + 2 * GLA_VW
ROPE_BASE = 10000.0

MM_RES_NSPLIT = 4

VMEM_LIMIT = 60 * 1024 * 1024

_NT_DIMS = (((1,), (1,)), ((), ()))


def _params(*sem):
    return pltpu.CompilerParams(dimension_semantics=sem, vmem_limit_bytes=VMEM_LIMIT)


def _mod_row(i, tm):
    return jnp.minimum((i * tm) // S, B)


def _ada_kernel(c_ref, w_ref, b_ref, o_ref):
    c = c_ref[...]
    s = c * jax.nn.sigmoid(c)
    o_ref[...] = jnp.dot(s.astype(BF16), w_ref[...].astype(BF16),
                         preferred_element_type=F32) + b_ref[...]


def _ada_mod(cvec, ada_w, ada_b):
    depth = ada_w.shape[0]
    tn = 1024
    return pl.pallas_call(
        _ada_kernel,
        grid=(depth, 6 * D // tn),
        in_specs=[pl.BlockSpec((16, D), lambda l, j: (0, 0)),
                  pl.BlockSpec((None, D, tn), lambda l, j: (l, 0, j)),
                  pl.BlockSpec((None, 1, tn), lambda l, j: (l, 0, j))],
        out_specs=pl.BlockSpec((None, 16, tn), lambda l, j: (l, 0, j)),
        out_shape=jax.ShapeDtypeStruct((depth, 16, 6 * D), F32),
        compiler_params=_params("parallel", "parallel"),
        name="ada_mod",
    )(cvec, ada_w, ada_b.reshape(depth, 1, 6 * D))


def _stream_specs(xs, tm, two_axes):
    if not isinstance(xs, tuple):
        imap = (lambda i, k: (i, 0)) if two_axes else (lambda i: (i, 0))
        return 0, [xs], [pl.BlockSpec((tm, D), imap)]
    nl = xs[0].shape[0] // tm
    if two_axes:
        maps = [lambda i, k: (jnp.minimum(i, nl - 1), 0), lambda i, k: (jnp.maximum(i - nl, 0), 0)]
    else:
        maps = [lambda i: (jnp.minimum(i, nl - 1), 0), lambda i: (jnp.maximum(i - nl, 0), 0)]
    return nl, list(xs), [pl.BlockSpec((tm, D), m) for m in maps]


def _stream_tile(x_refs, n_lat_tiles, rows=slice(None), cols=slice(None)):
    if len(x_refs) == 1:
        return x_refs[0][rows, cols]
    return jnp.where(pl.program_id(0) < n_lat_tiles, x_refs[0][rows, cols], x_refs[1][rows, cols])


def _normmod_kernel(*refs, n_x, n_lat_tiles):
    x_refs = refs[:n_x]
    sh_ref, sc_ref, g_ref, o_ref, oc_ref = refs[n_x:]
    x = _stream_tile(x_refs, n_lat_tiles)
    ms = jnp.mean(x * x, axis=-1, keepdims=True)
    h = x * lax.rsqrt(ms + EPS) * g_ref[...]
    h = h * (1.0 + sc_ref[...]) + sh_ref[...]
    o_ref[...] = h.astype(o_ref.dtype)
    h3 = h.reshape(h.shape[0] // S5_LC, S5_LC, D)
    for tl in range(S5_LC):
        oc_ref[tl] = h3[:, tl, :].astype(oc_ref.dtype)


def _normmod(xs, mod, g, shift_idx, scale_idx, rows):
    tm = 512
    nl, x_args, x_specs = _stream_specs(xs, tm, False)
    return pl.pallas_call(
        functools.partial(_normmod_kernel, n_x=len(x_args), n_lat_tiles=nl),
        grid=(rows // tm,),
        in_specs=x_specs + [
            pl.BlockSpec((None, 1, D), lambda i: (_mod_row(i, tm), 0, shift_idx)),
            pl.BlockSpec((None, 1, D), lambda i: (_mod_row(i, tm), 0, scale_idx)),
            pl.BlockSpec((1, D), lambda i: (0, 0))],
        out_specs=[pl.BlockSpec((tm, D), lambda i: (i, 0)),
                   pl.BlockSpec((S5_LC, tm // S5_LC, D), lambda i: (0, i, 0))],
        out_shape=[jax.ShapeDtypeStruct((rows, D), BF16),
                   jax.ShapeDtypeStruct((S5_LC, rows // S5_LC, D), BF16)],
        compiler_params=_params("parallel"),
        name="normmod",
    )(*x_args, mod, mod, g.reshape(1, D))


def _mm_kernel(*refs, relu2, cast_side):
    if cast_side:
        a_ref, w_ref, side_ref, o_ref, side_o_ref, wb_ref = refs
        side_o_ref[...] = side_ref[...].astype(BF16)
    else:
        a_ref, w_ref, o_ref, wb_ref = refs

    @pl.when(pl.program_id(1) == 0)
    def _():
        wb_ref[...] = w_ref[...].astype(BF16)

    acc = jnp.dot(a_ref[...], wb_ref[...], preferred_element_type=F32)
    if relu2:
        acc = jnp.square(jnp.maximum(acc, 0.0))
    o_ref[...] = acc.astype(o_ref.dtype)


MM_SIDE_ROWS = 128


def _mm(a, w, layer, n, rows, *, relu2=False, side=None, tm=2048, tn=1024):
    k = a.shape[1]
    tn = min(tn, n)
    n_i = rows // tm
    in_specs = [pl.BlockSpec((tm, k), lambda j, i: (i, 0)),
                pl.BlockSpec((None, k, tn), lambda j, i: (layer, 0, j))]
    out_specs = pl.BlockSpec((tm, tn), lambda j, i: (i, j))
    out_shape = jax.ShapeDtypeStruct((rows, n), BF16)
    args = [a, w]
    if side is not None:
        w2, layer2 = side
        n_blk = w2.shape[1] // MM_SIDE_ROWS
        assert n_blk <= (n // tn) * n_i
        blk = lambda j, i: jnp.minimum(j * n_i + i, n_blk - 1)
        in_specs.append(pl.BlockSpec((None, MM_SIDE_ROWS, D), lambda j, i: (layer2, blk(j, i), 0)))
        out_specs = [out_specs, pl.BlockSpec((None, MM_SIDE_ROWS, D), lambda j, i: (0, blk(j, i), 0))]
        out_shape = [out_shape, jax.ShapeDtypeStruct((1, w2.shape[1], D), BF16)]
        args.append(w2)
    return pl.pallas_call(
        functools.partial(_mm_kernel, relu2=relu2, cast_side=side is not None),
        grid=(n // tn, n_i),
        in_specs=in_specs,
        out_specs=out_specs,
        out_shape=out_shape,
        scratch_shapes=[pltpu.VMEM((k, tn), BF16)],
        compiler_params=_params("parallel", "arbitrary"),
        name="mm_relu2" if relu2 else "mm",
    )(*args)


def _mm_res_kernel(*refs, n_lhs, n_res, n_lat_tiles, nk, final_norm, next_norm):
    a_refs = refs[:n_lhs]
    w_refs = refs[n_lhs:2 * n_lhs]
    res_refs = refs[2 * n_lhs:2 * n_lhs + n_res]
    gate_ref = refs[2 * n_lhs + n_res]
    pos = 2 * n_lhs + n_res + 1
    n_extra = 1 if final_norm else (3 if next_norm else 0)
    extra = refs[pos:pos + n_extra]
    o_ref = refs[pos + n_extra]
    h_ref = refs[pos + n_extra + 1] if next_norm else None

    def normed(y, coef):
        ms = jnp.mean(y * y, axis=-1, keepdims=True)
        return y * lax.rsqrt(ms + EPS) * coef

    def epilogue(rows):
        if final_norm:
            o_ref[rows, :] = normed(o_ref[rows, :], extra[0][...])
        elif next_norm:
            sh_ref, sc_ref, g_ref = extra
            coef = g_ref[...] * (1.0 + sc_ref[...])
            h_ref[rows, :] = (normed(o_ref[rows, :], coef) + sh_ref[...]).astype(h_ref.dtype)

    wn = D // MM_RES_NSPLIT

    def update(rows, first):
        for cc in range(MM_RES_NSPLIT):
            cols = slice(cc * wn, (cc + 1) * wn)
            part = None
            for a_ref, w_ref in zip(a_refs, w_refs):
                d = jnp.dot(a_ref[rows, :], w_ref[:, cols], preferred_element_type=F32)
                part = d if part is None else part + d
            part = gate_ref[:, cols] * part
            if first:
                o_ref[rows, cols] = _stream_tile(res_refs, n_lat_tiles, rows, cols) + part
            else:
                o_ref[rows, cols] += part

    tm = o_ref.shape[0]
    if nk == 1:
        for rh in range(2):
            rows = slice(rh * (tm // 2), (rh + 1) * (tm // 2))
            update(rows, True)
            epilogue(rows)
        return

    kk = pl.program_id(1)
    every = slice(0, tm)

    @pl.when(kk == 0)
    def _():
        o_ref[...] = _stream_tile(res_refs, n_lat_tiles)

    update(every, False)
    if final_norm or next_norm:
        pl.when(kk == nk - 1)(functools.partial(epilogue, every))


def _mm_res(a_list, w, layer, resid, mod, gate_idx, rows, *, final_g=None, nxt=None,
            tm=512, tk=2048):
    n_lhs = len(a_list)
    kdim = a_list[0].shape[1]
    tk = min(kdim, tk)
    nk = kdim // tk
    nl, res_args, res_specs = _stream_specs(resid, tm, True)

    def mod_spec(idx):
        return pl.BlockSpec((None, 1, D), lambda i, k: (_mod_row(i, tm), 0, idx))

    def w_spec(j):
        return pl.BlockSpec((None, tk, D), lambda i, k: (layer, j * nk + k, 0))

    row_spec = pl.BlockSpec((tm, D), lambda i, k: (i, 0))
    vec_spec = pl.BlockSpec((1, D), lambda i, k: (0, 0))
    in_specs = ([pl.BlockSpec((tm, tk), lambda i, k: (i, k)) for _ in a_list]
                + [w_spec(j) for j in range(n_lhs)] + res_specs + [mod_spec(gate_idx)])
    args = list(a_list) + [w] * n_lhs + res_args + [mod]
    out_specs, out_shape = row_spec, jax.ShapeDtypeStruct((rows, D), F32)
    if final_g is not None:
        in_specs.append(vec_spec)
        args.append(final_g.reshape(1, D))
    elif nxt is not None:
        mod_n, g_n, shift_idx, scale_idx = nxt
        in_specs += [mod_spec(shift_idx), mod_spec(scale_idx), vec_spec]
        args += [mod_n, mod_n, g_n.reshape(1, D)]
        out_specs = [row_spec, row_spec]
        out_shape = [out_shape, jax.ShapeDtypeStruct((rows, D), BF16)]
    return pl.pallas_call(
        functools.partial(_mm_res_kernel, n_lhs=n_lhs, n_res=len(res_args), n_lat_tiles=nl,
                          nk=nk, final_norm=final_g is not None, next_norm=nxt is not None),
        grid=(rows // tm, nk),
        in_specs=in_specs,
        out_specs=out_specs,
        out_shape=out_shape,
        compiler_params=_params("parallel", "arbitrary"),
        name="mm_res",
    )(*args)


def _na_tables(rel_bias):
    hp = lax.Precision.HIGHEST
    cq = np.arange(GRID_W)[:, None]
    ck = np.arange(GRID_W)[None, :]
    ws = np.clip(cq - 8, 0, GRID_W - 16)
    col_ok = (ck >= ws) & (ck < ws + 16)
    col_hot = ((ck - cq + 15)[..., None] == np.arange(31)) & col_ok[..., None]
    blocks = jnp.einsum('hrj,qkj->hrqk', rel_bias.astype(F32), col_hot.astype(np.float32),
                        precision=hp)
    blocks = blocks + np.where(col_ok, 0.0, NEG_INF).astype(np.float32)
    masked = jnp.full((NA_H, GRID_W, GRID_W), NEG_INF, F32)
    pats = []
    for r0, start in ((0, 0), (NA_QROWS, 0), (ROWS - NA_QROWS, ROWS - NA_KROWS)):
        rows = []
        for a in range(NA_QROWS):
            r = r0 + a
            rs = min(max(r - 4, 0), ROWS - 8)
            rows.append(jnp.concatenate(
                [blocks[:, krow - r + 7] if rs <= krow < rs + 8 else masked
                 for krow in range(start, start + NA_KROWS)], axis=-1))
        pats.append(jnp.concatenate(rows, axis=1))
    return jnp.stack(pats, axis=1)


def _na_kernel(q_ref, k0_ref, k1_ref, k2_ref, v0_ref, v1_ref, v2_ref, kc_ref, vc_ref,
               tab_ref, o_ref):
    i = pl.program_id(1)

    def head(ref, hh):
        return ref[:, hh * NA_DH:(hh + 1) * NA_DH]

    def ctx_scores(hh):
        return lax.dot_general(head(q_ref, hh), head(kc_ref, hh), _NT_DIMS,
                               preferred_element_type=F32) * NA_SCALE

    @pl.when(i < ROWS // NA_QROWS)
    def _():
        for hh in range(NA_HB):
            q = head(q_ref, hh)
            s_c = ctx_scores(hh)
            m = jnp.max(s_c, axis=-1, keepdims=True)
            s_w = []
            for d, k_ref in enumerate((k0_ref, k1_ref, k2_ref)):
                s = lax.dot_general(q, head(k_ref, hh), _NT_DIMS,
                                    preferred_element_type=F32) * NA_SCALE
                s = s + tab_ref[hh, :, d * NA_QT:(d + 1) * NA_QT]
                s_w.append(s)
                m = jnp.maximum(m, jnp.max(s, axis=-1, keepdims=True))
            p_c = jnp.exp(s_c - m)
            l = jnp.sum(p_c, axis=-1, keepdims=True)
            o = jnp.dot(p_c.astype(BF16), head(vc_ref, hh), preferred_element_type=F32)
            for s, v_ref in zip(s_w, (v0_ref, v1_ref, v2_ref)):
                p = jnp.exp(s - m)
                l = l + jnp.sum(p, axis=-1, keepdims=True)
                o = o + jnp.dot(p.astype(BF16), head(v_ref, hh), preferred_element_type=F32)
            o_ref[:, hh * NA_DH:(hh + 1) * NA_DH] = (o / l).astype(o_ref.dtype)

    @pl.when(i == ROWS // NA_QROWS)
    def _():
        for hh in range(NA_HB):
            s_c = ctx_scores(hh)
            p_c = jnp.exp(s_c - jnp.max(s_c, axis=-1, keepdims=True))
            l = jnp.sum(p_c, axis=-1, keepdims=True)
            o = jnp.dot(p_c.astype(BF16), head(vc_ref, hh), preferred_element_type=F32)
            o_ref[:, hh * NA_DH:(hh + 1) * NA_DH] = (o / l).astype(o_ref.dtype)


def _na_attention(qkv, table):
    ng = ROWS // NA_QROWS
    blk = S // NA_QT
    ctx0 = NLAT // NA_QT

    def qrow(h, i, b):
        return jnp.where(i < ng, b * blk + i, ctx0 + b)

    nhb = NA_H // NA_HB

    def krow(d):
        return lambda h, i, b: (b * blk + jnp.clip(i - 1, 0, blk - 3) + d, nhb + h)

    def vrow(d):
        return lambda h, i, b: (b * blk + jnp.clip(i - 1, 0, blk - 3) + d, 2 * nhb + h)

    def pat(h, i, b):
        return (h, jnp.where(i == 0, 0, jnp.where(i >= ng - 1, 2, 1)), 0, 0)

    tile = (NA_QT, NA_HB * NA_DH)
    in_specs = ([pl.BlockSpec(tile, lambda h, i, b: (qrow(h, i, b), h))]
                + [pl.BlockSpec(tile, krow(d)) for d in range(3)]
                + [pl.BlockSpec(tile, vrow(d)) for d in range(3)]
                + [pl.BlockSpec(tile, lambda h, i, b: (ctx0 + b, nhb + h)),
                   pl.BlockSpec(tile, lambda h, i, b: (ctx0 + b, 2 * nhb + h)),
                   pl.BlockSpec((NA_HB, None, NA_QT, NA_KT), pat)])
    return pl.pallas_call(
        _na_kernel,
        grid=(nhb, ng + 1, B),
        in_specs=in_specs,
        out_specs=pl.BlockSpec(tile, lambda h, i, b: (qrow(h, i, b), h)),
        out_shape=jax.ShapeDtypeStruct((NT, NA_W), BF16),
        compiler_params=_params("parallel", "parallel", "parallel"),
        name="na_attention",
    )(*([qkv] * 9), table)


def _s5_matrices(lam_re, lam_im, log_dt, b_re, b_im, c_re, c_im):
    lam_re, lam_im = lam_re.astype(F32), lam_im.astype(F32)
    b_re, b_im = b_re.astype(F32), b_im.astype(F32)
    c_re, c_im = c_re.astype(F32), c_im.astype(F32)
    dt = jnp.exp(log_dt.astype(F32))[..., None]
    mag = jnp.exp(lam_re * dt)
    a_re = mag * jnp.cos(lam_im * dt)
    a_im = mag * jnp.sin(lam_im * dt)
    den = lam_re * lam_re + lam_im * lam_im
    f_re = ((a_re - 1.0) * lam_re + a_im * lam_im) / den
    f_im = (a_im * lam_re - (a_re - 1.0) * lam_im) / den
    bb_re = f_re[..., None] * b_re - f_im[..., None] * b_im
    bb_im = f_re[..., None] * b_im + f_im[..., None] * b_re

    pw_re, pw_im = jnp.ones_like(a_re)[None], jnp.zeros_like(a_im)[None]
    an_re, an_im = a_re, a_im
    while pw_re.shape[0] < S5_LC:
        pw_re, pw_im = (jnp.concatenate([pw_re, pw_re * an_re - pw_im * an_im]),
                        jnp.concatenate([pw_im, pw_re * an_im + pw_im * an_re]))
        an_re, an_im = an_re * an_re - an_im * an_im, 2.0 * an_re * an_im
    pw_re = jnp.concatenate([pw_re, an_re[None]])
    pw_im = jnp.concatenate([pw_im, an_im[None]])

    ab_re = pw_re[..., None] * bb_re[None] - pw_im[..., None] * bb_im[None]
    ab_im = pw_re[..., None] * bb_im[None] + pw_im[..., None] * bb_re[None]
    def lag_minor(z, d):
        return z[:S5_LC, d].transpose(1, 2, 0, 3).reshape(S5_G, 1, S5_P, S5_CW)
    def kernels(d):
        k = jnp.sum(c_re[d][..., None] * lag_minor(ab_re, d)
                    - c_im[d][..., None] * lag_minor(ab_im, d), axis=2)
        return k.reshape(S5_G, S5_CG, S5_LC, S5_CG).transpose(0, 3, 2, 1)
    zpad = jnp.zeros((S5_G, S5_CG, S5_LC, S5_CG), F32)
    kf = jnp.concatenate([zpad, kernels(0)], axis=2)
    kb = jnp.concatenate([jnp.flip(kernels(1), axis=2), zpad], axis=2)
    tc = jnp.stack([kf[:, :, S5_LC - s:2 * S5_LC - s] + kb[:, :, S5_LC - 1 - s:2 * S5_LC - 1 - s]
                    for s in range(S5_LC)], axis=1).reshape(S5_G, S5_CW, S5_CW)

    def st(arr, d, flip):
        z = arr[:S5_LC, d]
        z = jnp.flip(z, axis=0) if flip else z
        return z.transpose(1, 0, 3, 2).reshape(S5_G, S5_CW, S5_P)
    et = jnp.concatenate([st(ab_re, 0, True), st(ab_re, 1, False),
                          st(ab_im, 0, True), st(ab_im, 1, False)], axis=-1)

    def rd(d, flip):
        pr_ = pw_re[1:, d]
        pi_ = pw_im[1:, d]
        if flip:
            pr_, pi_ = jnp.flip(pr_, axis=0), jnp.flip(pi_, axis=0)
        cr = c_re[d][None] * pr_[:, :, None, :] - c_im[d][None] * pi_[:, :, None, :]
        ci = c_re[d][None] * pi_[:, :, None, :] + c_im[d][None] * pr_[:, :, None, :]
        to = lambda z: z.transpose(1, 3, 0, 2).reshape(S5_G, S5_P, S5_CW)
        return to(cr), to(-ci)
    fr, fi = rd(0, False)
    br, bi = rd(1, True)
    z = jnp.zeros_like(fr)
    ft = jnp.concatenate([fr, z, fi, z, z, br, z, bi], axis=1)

    a16_re = jnp.concatenate([pw_re[S5_LC, 0], pw_re[S5_LC, 1]], axis=-1)[:, None, :]
    a16_im = jnp.concatenate([pw_im[S5_LC, 0], pw_im[S5_LC, 1]], axis=-1)[:, None, :]
    return tc.astype(BF16), et.astype(BF16), ft.astype(BF16), a16_re, a16_im


def _uproj_kernel(w_ref, h_ref, o_ref, wt_ref):
    @pl.when((pl.program_id(0) == 0) & (pl.program_id(1) == 0))
    def _():
        wt_ref[...] = w_ref[...].T.astype(BF16)

    acc = lax.dot_general(wt_ref[...], h_ref[...], _NT_DIMS, preferred_element_type=F32)
    o_ref[...] = acc.reshape(S5_G, S5_CG, S5_NT).astype(o_ref.dtype)


def _s5_uproj(h_tl, w_in, col_block):
    return pl.pallas_call(
        _uproj_kernel,
        grid=(S5_LC, S5_N // S5_NT),
        in_specs=[pl.BlockSpec((None, D, S5_W), lambda t, n: (0, 0, col_block)),
                  pl.BlockSpec((None, S5_NT, D), lambda t, n: (t, n, 0))],
        out_specs=pl.BlockSpec((S5_G, None, S5_CG, S5_NT), lambda t, n: (0, t, 0, n)),
        out_shape=jax.ShapeDtypeStruct((S5_G, S5_LC, S5_CG, S5_N), BF16),
        scratch_shapes=[pltpu.VMEM((S5_W, D), BF16)],
        compiler_params=_params("arbitrary", "arbitrary"),
        name="s5_uproj",
    )(w_in, h_tl)


S5_PL = S5_NLAT + 4
S5_PC = S5_NCTX + 4
S5_CB = B * S5_PL
S5_ROWS = S5_CB + B * S5_PC


def _s5_chunk_rows(kind, k):
    if kind == "c":
        return pl.ds(S5_CB + k, B, stride=S5_PC)
    return pl.ds(k, B, stride=S5_PL)


def _s5_batch_rows():
    runs = [(b * S5_NLAT, b * S5_PL, S5_NLAT) for b in range(B)]
    runs += [(B * S5_NLAT + b * S5_NCTX, S5_CB + b * S5_PC, S5_NCTX) for b in range(B)]
    return runs


def _s5_kernel(ut_ref, tc_ref, et_ref, ft_ref, ar_ref, ai_ref, d_ref, o_ref,
               he_re_ref, he_im_ref, hpf_re_ref, hpf_im_ref, hpb_re_ref, hpb_im_ref):
    sw = 2 * S5_P
    utf = ut_ref[...].reshape(S5_CW, S5_N).astype(F32)
    un = utf.T.astype(BF16)
    y = jnp.dot(un, tc_ref[...], preferred_element_type=F32)
    he = jnp.dot(un, et_ref[...], preferred_element_type=F32)
    for src, dst, n in _s5_batch_rows():
        he_re_ref[pl.ds(dst, n), :] = he[src:src + n, :sw]
        he_im_ref[pl.ds(dst, n), :] = he[src:src + n, sw:]

    ar = ar_ref[...]
    ai = ai_ref[...]
    is_fwd = lax.broadcasted_iota(jnp.int32, (B, 2 * S5_P), 1) < S5_P
    h_re = jnp.zeros((B, 2 * S5_P), F32)
    h_im = jnp.zeros((B, 2 * S5_P), F32)
    fwd = [("c", k) for k in range(S5_NCTX)] + [("l", k) for k in range(S5_NLAT)]
    bwd = ([("c", k) for k in range(S5_NCTX - 1, -1, -1)]
           + [("l", k) for k in range(S5_NLAT - 1, -1, -1)])
    for cf, cb in zip(fwd, bwd):
        rf = _s5_chunk_rows(*cf)
        rb = _s5_chunk_rows(*cb)
        hpf_re_ref[rf, :] = h_re
        hpf_im_ref[rf, :] = h_im
        hpb_re_ref[rb, :] = h_re
        hpb_im_ref[rb, :] = h_im
        e_re = jnp.where(is_fwd, he_re_ref[rf, :], he_re_ref[rb, :])
        e_im = jnp.where(is_fwd, he_im_ref[rf, :], he_im_ref[rb, :])
        n_re = ar * h_re - ai * h_im + e_re
        n_im = ar * h_im + ai * h_re + e_im
        h_re, h_im = n_re, n_im

    def chunk_order(ref):
        return jnp.concatenate([ref[pl.ds(dst, n), :] for _, dst, n in _s5_batch_rows()], axis=0)
    hp = jnp.concatenate([chunk_order(r) for r in (hpf_re_ref, hpf_im_ref, hpb_re_ref, hpb_im_ref)],
                         axis=1).astype(BF16)
    y = y + jnp.dot(hp, ft_ref[...], preferred_element_type=F32)
    g = y.T + d_ref[...] * utf
    gl = 0.5 * g * (1.0 + lax.erf(g * (0.5 ** 0.5)))
    o_ref[...] = gl.astype(o_ref.dtype).reshape(S5_LC, S5_CG, S5_N)


def _s5_scan(ut, mats, d_col):
    tc, et, ft, a_re, a_im = mats
    mat = pl.BlockSpec((None, S5_CW, S5_CW), lambda g: (g, 0, 0))
    vec = pl.BlockSpec((None, 1, 2 * S5_P), lambda g: (g, 0, 0))
    io = pl.BlockSpec((None, S5_LC, S5_CG, S5_N), lambda g: (g, 0, 0, 0))
    return pl.pallas_call(
        _s5_kernel,
        grid=(S5_G,),
        in_specs=[io, mat, mat,
                  pl.BlockSpec((None, 2 * S5_CW, S5_CW), lambda g: (g, 0, 0)), vec, vec,
                  pl.BlockSpec((None, S5_CW, 1), lambda g: (g, 0, 0))],
        out_specs=io,
        out_shape=jax.ShapeDtypeStruct((S5_G, S5_LC, S5_CG, S5_N), BF16),
        scratch_shapes=[pltpu.VMEM((S5_ROWS, 2 * S5_P), F32)] * 6,
        compiler_params=_params("parallel"),
        name="s5_scan",
    )(ut, tc, et, ft, a_re, a_im, d_col)


def _glu_kernel(gl_ref, w_ref, b_ref, o_ref):
    gl = gl_ref[...].reshape(S5_W, S5_NT)
    z = jnp.dot(w_ref[...], gl, preferred_element_type=F32) + b_ref[...]
    s = gl.astype(F32) * jax.nn.sigmoid(z)
    o_ref[...] = s.T.astype(o_ref.dtype)


def _s5_glu(glt, w_t, b_col):
    out = pl.pallas_call(
        _glu_kernel,
        grid=(S5_LC, S5_N // S5_NT),
        in_specs=[pl.BlockSpec((S5_G, None, S5_CG, S5_NT), lambda t, n: (0, t, 0, n)),
                  pl.BlockSpec((S5_W, S5_W), lambda t, n: (0, 0)),
                  pl.BlockSpec((S5_W, 1), lambda t, n: (0, 0))],
        out_specs=pl.BlockSpec((S5_NT, S5_W), lambda t, n: (n, t)),
        out_shape=jax.ShapeDtypeStruct((S5_N, S5_LC * S5_W), BF16),
        compiler_params=_params("parallel", "parallel"),
        name="s5_glu",
    )(glt, w_t, b_col)
    return out.reshape(NT, S5_W)


def _rope_tables():
    half = GLA_DK // 2
    freqs = ROPE_BASE ** (-np.arange(0, half, 2, dtype=np.float32) / half)
    out = []
    for n in (ROWS, GRID_W):
        ang = np.arange(n, dtype=np.float32)[:, None] * freqs[None, :]
        c, s = np.cos(ang), np.sin(ang)
        out += [np.concatenate([c, c], axis=-1), np.concatenate([-s, s], axis=-1)]
    return tuple(jnp.asarray(t, F32) for t in out)


def _rope(x, rcos, rsin, ccos, csin):
    x0 = x[:, :128]
    x1 = x[:, 128:]
    return jnp.concatenate([x0 * rcos + pltpu.roll(x0, 64, axis=1) * rsin,
                            x1 * ccos + pltpu.roll(x1, 64, axis=1) * csin], axis=-1)


def _cumsum_rows(x, reverse):
    row = lax.broadcasted_iota(jnp.int32, x.shape, 0)
    s = 1
    while s < GLA_C:
        if s >= 8:
            if reverse:
                x = jnp.concatenate([x[:GLA_C - s] + x[s:], x[GLA_C - s:]], axis=0)
            else:
                x = jnp.concatenate([x[:s], x[s:] + x[:GLA_C - s]], axis=0)
        elif reverse:
            x = x + jnp.where(row < GLA_C - s, pltpu.roll(x, GLA_C - s, axis=0), 0.0)
        else:
            x = x + jnp.where(row >= s, pltpu.roll(x, s, axis=0), 0.0)
        s *= 2
    return x


def _log_sigmoid(x):
    return jnp.minimum(x, 0.0) - jnp.log(1.0 + jnp.exp(-jnp.abs(x)))


def _chunk_rows(c):
    if isinstance(c, int):
        return pl.ds(c * GLA_C, GLA_C)
    return pl.ds(pl.multiple_of(c * GLA_C, GLA_C), GLA_C)


def _gla_kernel(ql_ref, kl_ref, vl_ref, gl_ref, qc_ref, kc_ref, vc_ref, al_ref, ac_ref,
                waf_ref, wab_ref, baf_ref, bab_ref, rcos_ref, rsin_ref, ccos_ref, csin_ref,
                ng_ref, o_ref,
                qi_f, ki_f, ke_f, dec_f, st_f, qi_b, ki_b, ke_b, dec_b, st_b, acc_ref):
    n_ctx = L // GLA_C
    n_lat = S // GLA_C
    qscale = GLA_DK ** -0.5
    fwd = (waf_ref, baf_ref, qi_f, ki_f, ke_f, dec_f, False)
    bwd = (wab_ref, bab_ref, qi_b, ki_b, ke_b, dec_b, True)

    ii = lax.broadcasted_iota(jnp.int32, (GLA_C, GLA_C), 0)
    jj = lax.broadcasted_iota(jnp.int32, (GLA_C, GLA_C), 1)

    def prepare(direction, q, k, a, c):
        wa_ref, ba_ref, qi, ki, ke, dec, reverse = direction
        dst = _chunk_rows(c)
        la = _log_sigmoid(jnp.dot(a, wa_ref[...], preferred_element_type=F32)
                          + ba_ref[...]) / GLA_TAU
        bc = _cumsum_rows(la, reverse)
        b_last = bc[0:1, :] if reverse else bc[GLA_C - 1:GLA_C, :]
        qi[dst, :] = (q * jnp.exp(bc)).astype(BF16)
        ki[dst, :] = (k * jnp.exp(-bc)).astype(BF16)
        ke[dst, :] = (k * jnp.exp(b_last - bc)).astype(BF16)
        dec[pl.ds(c, 1), :] = jnp.exp(b_last)

    def prepare_latent(direction, c):
        r = _chunk_rows(c)
        tabs = (rcos_ref[pl.ds(c, 1), :], rsin_ref[pl.ds(c, 1), :], ccos_ref[...], csin_ref[...])
        q = _rope(ql_ref[r, :].astype(F32) * qscale, *tabs)
        k = _rope(kl_ref[r, :].astype(F32), *tabs)
        prepare(direction, q, k, al_ref[r, :], n_ctx + c)

    def prepare_context(direction, c):
        r = _chunk_rows(c)
        prepare(direction, qc_ref[r, :].astype(F32) * qscale, kc_ref[r, :].astype(F32),
                ac_ref[r, :], c)

    prepare_context(fwd, 0)
    prepare_context(bwd, n_ctx - 1)
    prepare_latent(fwd, 0)
    prepare_latent(bwd, n_lat - 1)

    def advance(direction, st_ref, c, v, want_out):
        _, _, qi, ki, ke, dec, reverse = direction
        r = _chunk_rows(c)
        st = st_ref[...]
        o = None
        if want_out:
            q_in = qi[r, :]
            att = lax.dot_general(q_in, ki[r, :], _NT_DIMS, preferred_element_type=F32)
            att = jnp.where((ii <= jj) if reverse else (ii >= jj), att, 0.0)
            o = (jnp.dot(att.astype(BF16), v, preferred_element_type=F32)
                 + lax.dot_general(q_in, st.astype(BF16), _NT_DIMS, preferred_element_type=F32))
        st_ref[...] = dec[pl.ds(c, 1), :] * st + lax.dot_general(
            v, ke[r, :], (((0,), (0,)), ((), ())), preferred_element_type=F32)
        return o

    st_f[...] = jnp.zeros_like(st_f)
    st_b[...] = jnp.zeros_like(st_b)

    def ctx_pair(j, carry):
        cb = n_ctx - 1 - j
        advance(fwd, st_f, j, vc_ref[_chunk_rows(j), :], False)
        advance(bwd, st_b, cb, vc_ref[_chunk_rows(cb), :], False)
        prepare_context(fwd, jnp.minimum(j + 1, n_ctx - 1))
        prepare_context(bwd, jnp.maximum(cb - 1, 0))
        return carry
    lax.fori_loop(0, n_ctx, ctx_pair, 0, unroll=2)

    def lat_pair(j, accumulate):
        cb = n_lat - 1 - j
        rf = _chunk_rows(j)
        rb = _chunk_rows(cb)
        o_f = advance(fwd, st_f, n_ctx + j, vl_ref[rf, :], True)
        o_b = advance(bwd, st_b, n_ctx + cb, vl_ref[rb, :], True)
        if accumulate:
            acc_ref[rf, :] += o_f
            acc_ref[rb, :] += o_b
        else:
            acc_ref[rf, :] = o_f
            acc_ref[rb, :] = o_b
        prepare_latent(fwd, jnp.minimum(j + 1, n_lat - 1))
        prepare_latent(bwd, jnp.maximum(cb - 1, 0))

    def lat_first(j, carry):
        lat_pair(j, False)
        return carry
    lax.fori_loop(0, n_lat // 2, lat_first, 0, unroll=2)

    def lat_second(j, carry):
        lat_pair(j, True)
        return carry
    lax.fori_loop(n_lat // 2, n_lat, lat_second, 0, unroll=2)

    tr = 256

    def fin(t, carry):
        r = pl.ds(pl.multiple_of(t * tr, tr), tr)
        o = acc_ref[r, :]
        ms = jnp.mean(o * o, axis=-1, keepdims=True)
        g = gl_ref[r, :].astype(F32)
        o_ref[r, :] = (o * lax.rsqrt(ms + EPS) * ng_ref[...]
                       * (g * jax.nn.sigmoid(g))).astype(o_ref.dtype)
        return carry
    lax.fori_loop(0, S // tr, fin, 0)


def _gla(qkvg, acode, wa, ba, rope, norm_g):
    ctx0 = NLAT // L
    kq = GLA_QK // GLA_DK
    half = GLA_DK // 2
    n_chunks = (L + S) // GLA_C
    per_dir = [pltpu.VMEM((L + S, GLA_DK), BF16)] * 3 + [pltpu.VMEM((n_chunks, GLA_DK), F32),
                                                         pltpu.VMEM((GLA_DV, GLA_DK), F32)]
    in_specs = [
        pl.BlockSpec((S, GLA_DK), lambda b, h: (b, h)),
        pl.BlockSpec((S, GLA_DK), lambda b, h: (b, kq + h)),
        pl.BlockSpec((S, GLA_DV), lambda b, h: (b, kq + h)),
        pl.BlockSpec((S, GLA_DV), lambda b, h: (b, 2 * kq + h)),
        pl.BlockSpec((L, GLA_DK), lambda b, h: (ctx0 + b, h)),
        pl.BlockSpec((L, GLA_DK), lambda b, h: (ctx0 + b, kq + h)),
        pl.BlockSpec((L, GLA_DV), lambda b, h: (ctx0 + b, kq + h)),
        pl.BlockSpec((S, 128), lambda b, h: (b, 0)),
        pl.BlockSpec((L, 128), lambda b, h: (ctx0 + b, 0)),
        pl.BlockSpec((128, GLA_DK), lambda b, h: (0, h)),
        pl.BlockSpec((128, GLA_DK), lambda b, h: (0, kq + h)),
        pl.BlockSpec((1, GLA_DK), lambda b, h: (0, h)),
        pl.BlockSpec((1, GLA_DK), lambda b, h: (0, kq + h)),
        pl.BlockSpec((ROWS, half), lambda b, h: (0, 0)),
        pl.BlockSpec((ROWS, half), lambda b, h: (0, 0)),
        pl.BlockSpec((GRID_W, half), lambda b, h: (0, 0)),
        pl.BlockSpec((GRID_W, half), lambda b, h: (0, 0)),
        pl.BlockSpec((1, GLA_DV), lambda b, h: (0, 0)),
    ]
    return pl.pallas_call(
        _gla_kernel,
        grid=(B, GLA_H),
        in_specs=in_specs,
        out_specs=pl.BlockSpec((S, GLA_DV), lambda b, h: (b, h)),
        out_shape=jax.ShapeDtypeStruct((NLAT, GLA_VW), BF16),
        scratch_shapes=per_dir + per_dir + [pltpu.VMEM((S, GLA_DV), F32)],
        compiler_params=_params("parallel", "parallel"),
        name="gla",
    )(qkvg, qkvg, qkvg, qkvg, qkvg, qkvg, qkvg, acode, acode, wa, wa, ba, ba, *rope,
      norm_g.reshape(1, GLA_DV))


def kernel(x, c, ctx, c_ctx, ada_w, ada_b, norm1_g, norm2_g, mlp_w1, mlp_w2, final_g, ab_w_in, ab_w_out, na_rel_bias, s5_lambda_re, s5_lambda_im, s5_log_dt, s5_b_re, s5_b_im, s5_c_re, s5_c_im, s5_d, s5_glu_w, s5_glu_b, gla_w_in, gla_w_a2, gla_b_a, gla_norm_g, gla_w_out):
    xs = (x.astype(F32).reshape(NLAT, D), ctx.astype(F32).reshape(NCTX, D))
    cvec = jnp.zeros((16, D), F32).at[:B].set(c.astype(F32)).at[B].set(c_ctx.astype(F32))
    mods = _ada_mod(cvec, ada_w, ada_b).reshape(2, 16, 1, 6 * D)
    bf = lambda w: w.astype(BF16)

    mod = mods[0]
    h, h_tl = _normmod(xs, mod, norm1_g[0], 0, 1, NT)
    qkv, w_ab_out = _mm(h, ab_w_in, 0, 3 * NA_W, NT, side=(ab_w_out, 0))
    att = _na_attention(qkv, _na_tables(na_rel_bias[0]))
    ut = _s5_uproj(h_tl, ab_w_in, 3 * NA_W // S5_W)
    mats = _s5_matrices(s5_lambda_re[0], s5_lambda_im[0], s5_log_dt[0], s5_b_re[0], s5_b_im[0],
                        s5_c_re[0], s5_c_im[0])
    d_col = jnp.tile(s5_d[0].astype(F32).reshape(S5_G, 1, S5_CG), (1, S5_LC, 1)).reshape(S5_G, S5_CW, 1)
    glt = _s5_scan(ut, mats, d_col)
    s5 = _s5_glu(glt, bf(s5_glu_w[0].T), s5_glu_b[0].astype(F32).reshape(S5_W, 1))
    xs, h = _mm_res([att, s5], w_ab_out, 0, xs, mod, 2, NT, nxt=(mod, norm2_g[0], 3, 4))
    hid, w2 = _mm(h, mlp_w1, 0, MLP_H, NT, relu2=True, side=(mlp_w2, 0))
    xs, h = _mm_res([hid], w2, 0, xs, mod, 5, NT, nxt=(mods[1], norm1_g[1], 0, 1))

    mod = mods[1]
    w_in = gla_w_in[0]
    qkvg, w_gla_out = _mm(h, gla_w_in, 0, GLA_MAIN, NT, side=(gla_w_out, 0))
    w_code = jnp.zeros((1, D, 128), F32).at[0, :, :2 * GLA_RANK].set(w_in[:, GLA_MAIN:])
    acode = _mm(h, w_code, 0, 128, NT)
    wa = (jnp.zeros((128, 2 * GLA_QK), F32)
          .at[:GLA_RANK, :GLA_QK].set(gla_w_a2[0, 0])
          .at[GLA_RANK:2 * GLA_RANK, GLA_QK:].set(gla_w_a2[0, 1]))
    ba = gla_b_a[0].astype(F32).reshape(1, 2 * GLA_QK)
    og = _gla(qkvg, acode, bf(wa), ba, _rope_tables(), gla_norm_g[0].astype(F32))
    xl, h = _mm_res([og], w_gla_out, 0, xs, mod, 2, NLAT, nxt=(mod, norm2_g[1], 3, 4))
    hid, w2 = _mm(h, mlp_w1, 1, MLP_H, NLAT, relu2=True, side=(mlp_w2, 1))
    out = _mm_res([hid], w2, 0, xl, mod, 5, NLAT, final_g=final_g.astype(F32))
    return out.reshape(B, S, D).astype(x.dtype)
```

```python
import functools
import math

import numpy as np
import jax
import jax.numpy as jnp
from jax import lax
from jax.experimental import pallas as pl
from jax.experimental.pallas import tpu as pltpu

F32 = jnp.float32
BF16 = jnp.bfloat16

D = 2048
B = 8
S = 2048
L = 256
GRID_W = 64
ROWS = S // GRID_W
NLAT = B * S
NCTX = B * L
NT = NLAT + NCTX
MLP_H = 4 * D
EPS = 1e-6
NEG_INF = -1e30

NA_H = 8
NA_DH = 128
NA_W = NA_H * NA_DH
NA_SCALE = NA_DH ** -0.5
NA_HB = 4
NA_QROWS = 4
NA_KROWS = 12
NA_QT = NA_QROWS * GRID_W
NA_KT = NA_KROWS * GRID_W

S5_W = D // 2
S5_CG = 16
S5_G = S5_W // S5_CG
S5_P = 64
S5_LC = 16
S5_CW = S5_LC * S5_CG
S5_NLAT = S // S5_LC
S5_NCTX = L // S5_LC
S5_N = B * (S5_NLAT + S5_NCTX)
S5_NT = S5_N

GLA_H = 4
GLA_DK = 256
GLA_DV = 512
GLA_QK = GLA_H * GLA_DK
GLA_VW = GLA_H * GLA_DV
GLA_RANK = 16
GLA_TAU = 16.0
GLA_C = 64
GLA_MAIN = 2 * GLA_QK + 2 * GLA_VW
ROPE_BASE = 10000.0

MM_RES_NSPLIT = 4

VMEM_LIMIT = 60 * 1024 * 1024

_NT_DIMS = (((1,), (1,)), ((), ()))


def _params(*sem):
    return pltpu.CompilerParams(dimension_semantics=sem, vmem_limit_bytes=VMEM_LIMIT)


def _mod_row(i, tm):
    return jnp.minimum((i * tm) // S, B)


def _ada_kernel(c_ref, w_ref, b_ref, o_ref):
    c = c_ref[...]
    s = c * jax.nn.sigmoid(c)
    o_ref[...] = jnp.dot(s.astype(BF16), w_ref[...].astype(BF16),
                         preferred_element_type=F32) + b_ref[...]


def _ada_mod(cvec, ada_w, ada_b):
    depth = ada_w.shape[0]
    tn = 1024
    return pl.pallas_call(
        _ada_kernel,
        grid=(depth, 6 * D // tn),
        in_specs=[pl.BlockSpec((16, D), lambda l, j: (0, 0)),
                  pl.BlockSpec((None, D, tn), lambda l, j: (l, 0, j)),
                  pl.BlockSpec((None, 1, tn), lambda l, j: (l, 0, j))],
        out_specs=pl.BlockSpec((None, 16, tn), lambda l, j: (l, 0, j)),
        out_shape=jax.ShapeDtypeStruct((depth, 16, 6 * D), F32),
        compiler_params=_params("parallel", "parallel"),
        name="ada_mod",
    )(cvec, ada_w, ada_b.reshape(depth, 1, 6 * D))


def _stream_specs(xs, tm, two_axes):
    if not isinstance(xs, tuple):
        imap = (lambda i, k: (i, 0)) if two_axes else (lambda i: (i, 0))
        return 0, [xs], [pl.BlockSpec((tm, D), imap)]
    nl = xs[0].shape[0] // tm
    if two_axes:
        maps = [lambda i, k: (jnp.minimum(i, nl - 1), 0), lambda i, k: (jnp.maximum(i - nl, 0), 0)]
    else:
        maps = [lambda i: (jnp.minimum(i, nl - 1), 0), lambda i: (jnp.maximum(i - nl, 0), 0)]
    return nl, list(xs), [pl.BlockSpec((tm, D), m) for m in maps]


def _stream_tile(x_refs, n_lat_tiles, rows=slice(None), cols=slice(None)):
    if len(x_refs) == 1:
        return x_refs[0][rows, cols]
    return jnp.where(pl.program_id(0) < n_lat_tiles, x_refs[0][rows, cols], x_refs[1][rows, cols])


def _normmod_kernel(*refs, n_x, n_lat_tiles):
    x_refs = refs[:n_x]
    sh_ref, sc_ref, g_ref, perm_ref, o_ref, oc_ref = refs[n_x:]
    x = _stream_tile(x_refs, n_lat_tiles)
    ms = jnp.mean(x * x, axis=-1, keepdims=True)
    h = x * lax.rsqrt(ms + EPS) * g_ref[...]
    h = (h * (1.0 + sc_ref[...]) + sh_ref[...]).astype(o_ref.dtype)
    o_ref[...] = h
    hp = jnp.dot(perm_ref[...], h, preferred_element_type=F32)
    oc_ref[...] = hp.reshape(oc_ref.shape).astype(oc_ref.dtype)


def _chunk_row_perm(tm):
    r = np.arange(tm)
    perm = np.zeros((tm, tm), np.float32)
    perm[(r % S5_LC) * (tm // S5_LC) + r // S5_LC, r] = 1.0
    return perm


def _normmod(xs, mod, g, shift_idx, scale_idx, rows):
    tm = 512
    nl, x_args, x_specs = _stream_specs(xs, tm, False)
    perm = _chunk_row_perm(tm)
    return pl.pallas_call(
        functools.partial(_normmod_kernel, n_x=len(x_args), n_lat_tiles=nl),
        grid=(rows // tm,),
        in_specs=x_specs + [
            pl.BlockSpec((None, 1, D), lambda i: (_mod_row(i, tm), 0, shift_idx)),
            pl.BlockSpec((None, 1, D), lambda i: (_mod_row(i, tm), 0, scale_idx)),
            pl.BlockSpec((1, D), lambda i: (0, 0)),
            pl.BlockSpec((tm, tm), lambda i: (0, 0))],
        out_specs=[pl.BlockSpec((tm, D), lambda i: (i, 0)),
                   pl.BlockSpec((S5_LC, tm // S5_LC, D), lambda i: (0, i, 0))],
        out_shape=[jax.ShapeDtypeStruct((rows, D), BF16),
                   jax.ShapeDtypeStruct((S5_LC, rows // S5_LC, D), BF16)],
        compiler_params=_params("parallel"),
        name="normmod",
    )(*x_args, mod, mod, g.reshape(1, D), jnp.asarray(perm, BF16))


def _mm_kernel(*refs, relu2, cast_side):
    if cast_side:
        a_ref, w_ref, side_ref, o_ref, side_o_ref, wb_ref = refs
        side_o_ref[...] = side_ref[...].astype(BF16)
    else:
        a_ref, w_ref, o_ref, wb_ref = refs

    @pl.when(pl.program_id(1) == 0)
    def _():
        wb_ref[...] = w_ref[...].astype(BF16)

    acc = jnp.dot(a_ref[...], wb_ref[...], preferred_element_type=F32)
    if relu2:
        acc = jnp.square(jnp.maximum(acc, 0.0))
    o_ref[...] = acc.astype(o_ref.dtype)


MM_SIDE_ROWS = 128


def _mm(a, w, layer, n, rows, *, relu2=False, side=None, tm=2048, tn=1024):
    k = a.shape[1]
    tn = min(tn, n)
    n_i = rows // tm
    in_specs = [pl.BlockSpec((tm, k), lambda j, i: (i, 0)),
                pl.BlockSpec((None, k, tn), lambda j, i: (layer, 0, j))]
    out_specs = pl.BlockSpec((tm, tn), lambda j, i: (i, j))
    out_shape = jax.ShapeDtypeStruct((rows, n), BF16)
    args = [a, w]
    if side is not None:
        w2, layer2 = side
        n_blk = w2.shape[1] // MM_SIDE_ROWS
        assert n_blk <= (n // tn) * n_i
        blk = lambda j, i: jnp.minimum(j * n_i + i, n_blk - 1)
        in_specs.append(pl.BlockSpec((None, MM_SIDE_ROWS, D), lambda j, i: (layer2, blk(j, i), 0)))
        out_specs = [out_specs, pl.BlockSpec((None, MM_SIDE_ROWS, D), lambda j, i: (0, blk(j, i), 0))]
        out_shape = [out_shape, jax.ShapeDtypeStruct((1, w2.shape[1], D), BF16)]
        args.append(w2)
    return pl.pallas_call(
        functools.partial(_mm_kernel, relu2=relu2, cast_side=side is not None),
        grid=(n // tn, n_i),
        in_specs=in_specs,
        out_specs=out_specs,
        out_shape=out_shape,
        scratch_shapes=[pltpu.VMEM((k, tn), BF16)],
        compiler_params=_params("parallel", "arbitrary"),
        name="mm_relu2" if relu2 else "mm",
    )(*args)


def _mm_res_kernel(*refs, n_lhs, n_res, n_lat_tiles, nk, final_norm, next_norm, chunked):
    a_refs = refs[:n_lhs]
    w_refs = refs[n_lhs:2 * n_lhs]
    res_refs = refs[2 * n_lhs:2 * n_lhs + n_res]
    gate_ref = refs[2 * n_lhs + n_res]
    pos = 2 * n_lhs + n_res + 1
    n_extra = 1 if final_norm else (3 if next_norm else 0)
    extra = refs[pos:pos + n_extra]
    pos += n_extra
    if any(chunked):
        perm_ref = refs[pos]
        pos += 1
    o_ref = refs[pos]
    h_ref = refs[pos + 1] if next_norm else None

    a_vals = []
    for a_ref, is_chunked in zip(a_refs, chunked):
        if is_chunked:
            kw = a_ref.shape[1] // S5_LC
            stacked = jnp.concatenate([a_ref[:, tl * kw:(tl + 1) * kw] for tl in range(S5_LC)],
                                      axis=0)
            a_vals.append(jnp.dot(perm_ref[...], stacked,
                                  preferred_element_type=F32).astype(stacked.dtype))
        else:
            a_vals.append(a_ref)

    def normed(y, coef):
        ms = jnp.mean(y * y, axis=-1, keepdims=True)
        return y * lax.rsqrt(ms + EPS) * coef

    def epilogue(rows):
        if final_norm:
            o_ref[rows, :] = normed(o_ref[rows, :], extra[0][...])
        elif next_norm:
            sh_ref, sc_ref, g_ref = extra
            coef = g_ref[...] * (1.0 + sc_ref[...])
            h_ref[rows, :] = (normed(o_ref[rows, :], coef) + sh_ref[...]).astype(h_ref.dtype)

    wn = D // MM_RES_NSPLIT

    def update(rows, first):
        for cc in range(MM_RES_NSPLIT):
            cols = slice(cc * wn, (cc + 1) * wn)
            part = None
            for a_val, w_ref in zip(a_vals, w_refs):
                d = jnp.dot(a_val[rows, :], w_ref[:, cols], preferred_element_type=F32)
                part = d if part is None else part + d
            part = gate_ref[:, cols] * part
            if first:
                o_ref[rows, cols] = _stream_tile(res_refs, n_lat_tiles, rows, cols) + part
            else:
                o_ref[rows, cols] += part

    tm = o_ref.shape[0]
    if nk == 1:
        for rh in range(2):
            rows = slice(rh * (tm // 2), (rh + 1) * (tm // 2))
            update(rows, True)
            epilogue(rows)
        return

    kk = pl.program_id(1)
    every = slice(0, tm)

    @pl.when(kk == 0)
    def _():
        o_ref[...] = _stream_tile(res_refs, n_lat_tiles)

    update(every, False)
    if final_norm or next_norm:
        pl.when(kk == nk - 1)(functools.partial(epilogue, every))


def _mm_res(a_list, w, layer, resid, mod, gate_idx, rows, *, final_g=None, nxt=None,
            tm=512, tk=2048):
    n_lhs = len(a_list)
    kdim = a_list[0].shape[1]
    chunked = tuple(a.shape[0] * S5_LC == rows and a.shape[1] == S5_LC * kdim for a in a_list)
    tk = min(kdim, tk)
    nk = kdim // tk
    assert nk == 1 or not any(chunked)
    nl, res_args, res_specs = _stream_specs(resid, tm, True)

    def mod_spec(idx):
        return pl.BlockSpec((None, 1, D), lambda i, k: (_mod_row(i, tm), 0, idx))

    def w_spec(j):
        return pl.BlockSpec((None, tk, D), lambda i, k: (layer, j * nk + k, 0))

    row_spec = pl.BlockSpec((tm, D), lambda i, k: (i, 0))
    vec_spec = pl.BlockSpec((1, D), lambda i, k: (0, 0))
    def a_spec(is_chunked):
        if is_chunked:
            return pl.BlockSpec((tm // S5_LC, S5_LC * kdim), lambda i, k: (i, 0))
        return pl.BlockSpec((tm, tk), lambda i, k: (i, k))

    in_specs = ([a_spec(c) for c in chunked]
                + [w_spec(j) for j in range(n_lhs)] + res_specs + [mod_spec(gate_idx)])
    args = list(a_list) + [w] * n_lhs + res_args + [mod]
    out_specs, out_shape = row_spec, jax.ShapeDtypeStruct((rows, D), F32)
    if final_g is not None:
        in_specs.append(vec_spec)
        args.append(final_g.reshape(1, D))
    elif nxt is not None:
        mod_n, g_n, shift_idx, scale_idx = nxt
        in_specs += [mod_spec(shift_idx), mod_spec(scale_idx), vec_spec]
        args += [mod_n, mod_n, g_n.reshape(1, D)]
        out_specs = [row_spec, row_spec]
        out_shape = [out_shape, jax.ShapeDtypeStruct((rows, D), BF16)]
    if any(chunked):
        in_specs.append(pl.BlockSpec((tm, tm), lambda i, k: (0, 0)))
        args.append(jnp.asarray(_chunk_row_perm(tm).T, BF16))
    return pl.pallas_call(
        functools.partial(_mm_res_kernel, n_lhs=n_lhs, n_res=len(res_args), n_lat_tiles=nl,
                          nk=nk, final_norm=final_g is not None, next_norm=nxt is not None,
                          chunked=chunked),
        grid=(rows // tm, nk),
        in_specs=in_specs,
        out_specs=out_specs,
        out_shape=out_shape,
        compiler_params=_params("parallel", "arbitrary"),
        name="mm_res",
    )(*args)


def _na_tables(rel_bias):
    hp = lax.Precision.HIGHEST
    cq = np.arange(GRID_W)[:, None]
    ck = np.arange(GRID_W)[None, :]
    ws = np.clip(cq - 8, 0, GRID_W - 16)
    col_ok = (ck >= ws) & (ck < ws + 16)
    col_hot = ((ck - cq + 15)[..., None] == np.arange(31)) & col_ok[..., None]
    blocks = jnp.einsum('hrj,qkj->hrqk', rel_bias.astype(F32), col_hot.astype(np.float32),
                        precision=hp)
    blocks = blocks + np.where(col_ok, 0.0, NEG_INF).astype(np.float32)
    masked = jnp.full((NA_H, GRID_W, GRID_W), NEG_INF, F32)
    pats = []
    for r0, start in ((0, 0), (NA_QROWS, 0), (ROWS - NA_QROWS, ROWS - NA_KROWS)):
        rows = []
        for a in range(NA_QROWS):
            r = r0 + a
            rs = min(max(r - 4, 0), ROWS - 8)
            rows.append(jnp.concatenate(
                [blocks[:, krow - r + 7] if rs <= krow < rs + 8 else masked
                 for krow in range(start, start + NA_KROWS)], axis=-1))
        pats.append(jnp.concatenate(rows, axis=1))
    return jnp.stack(pats, axis=1)


def _na_kernel(q_ref, k0_ref, k1_ref, k2_ref, v0_ref, v1_ref, v2_ref, kc_ref, vc_ref,
               tab_ref, o_ref):
    i = pl.program_id(1)

    def head(ref, hh):
        return ref[:, hh * NA_DH:(hh + 1) * NA_DH]

    def ctx_scores(hh):
        return lax.dot_general(head(q_ref, hh), head(kc_ref, hh), _NT_DIMS,
                               preferred_element_type=F32) * NA_SCALE

    @pl.when(i < ROWS // NA_QROWS)
    def _():
        for hh in range(NA_HB):
            q = head(q_ref, hh)
            s_c = ctx_scores(hh)
            m = jnp.max(s_c, axis=-1, keepdims=True)
            s_w = []
            for d, k_ref in enumerate((k0_ref, k1_ref, k2_ref)):
                s = lax.dot_general(q, head(k_ref, hh), _NT_DIMS,
                                    preferred_element_type=F32) * NA_SCALE
                s = s + tab_ref[hh, :, d * NA_QT:(d + 1) * NA_QT]
                s_w.append(s)
                m = jnp.maximum(m, jnp.max(s, axis=-1, keepdims=True))
            p_c = jnp.exp(s_c - m)
            l = jnp.sum(p_c, axis=-1, keepdims=True)
            o = jnp.dot(p_c.astype(BF16), head(vc_ref, hh), preferred_element_type=F32)
            for s, v_ref in zip(s_w, (v0_ref, v1_ref, v2_ref)):
                p = jnp.exp(s - m)
                l = l + jnp.sum(p, axis=-1, keepdims=True)
                o = o + jnp.dot(p.astype(BF16), head(v_ref, hh), preferred_element_type=F32)
            o_ref[:, hh * NA_DH:(hh + 1) * NA_DH] = (o / l).astype(o_ref.dtype)

    @pl.when(i == ROWS // NA_QROWS)
    def _():
        for hh in range(NA_HB):
            s_c = ctx_scores(hh)
            p_c = jnp.exp(s_c - jnp.max(s_c, axis=-1, keepdims=True))
            l = jnp.sum(p_c, axis=-1, keepdims=True)
            o = jnp.dot(p_c.astype(BF16), head(vc_ref, hh), preferred_element_type=F32)
            o_ref[:, hh * NA_DH:(hh + 1) * NA_DH] = (o / l).astype(o_ref.dtype)


def _na_attention(qkv, table):
    ng = ROWS // NA_QROWS
    blk = S // NA_QT
    ctx0 = NLAT // NA_QT

    def qrow(h, i, b):
        return jnp.where(i < ng, b * blk + i, ctx0 + b)

    nhb = NA_H // NA_HB

    def krow(d):
        return lambda h, i, b: (b * blk + jnp.clip(i - 1, 0, blk - 3) + d, nhb + h)

    def vrow(d):
        return lambda h, i, b: (b * blk + jnp.clip(i - 1, 0, blk - 3) + d, 2 * nhb + h)

    def pat(h, i, b):
        return (h, jnp.where(i == 0, 0, jnp.where(i >= ng - 1, 2, 1)), 0, 0)

    tile = (NA_QT, NA_HB * NA_DH)
    in_specs = ([pl.BlockSpec(tile, lambda h, i, b: (qrow(h, i, b), h))]
                + [pl.BlockSpec(tile, krow(d)) for d in range(3)]
                + [pl.BlockSpec(tile, vrow(d)) for d in range(3)]
                + [pl.BlockSpec(tile, lambda h, i, b: (ctx0 + b, nhb + h)),
                   pl.BlockSpec(tile, lambda h, i, b: (ctx0 + b, 2 * nhb + h)),
                   pl.BlockSpec((NA_HB, None, NA_QT, NA_KT), pat)])
    return pl.pallas_call(
        _na_kernel,
        grid=(nhb, ng + 1, B),
        in_specs=in_specs,
        out_specs=pl.BlockSpec(tile, lambda h, i, b: (qrow(h, i, b), h)),
        out_shape=jax.ShapeDtypeStruct((NT, NA_W), BF16),
        compiler_params=_params("parallel", "parallel", "parallel"),
        name="na_attention",
    )(*([qkv] * 9), table)


def _s5_matrices(lam_re, lam_im, log_dt, b_re, b_im, c_re, c_im):
    lam_re, lam_im = lam_re.astype(F32), lam_im.astype(F32)
    b_re, b_im = b_re.astype(F32), b_im.astype(F32)
    c_re, c_im = c_re.astype(F32), c_im.astype(F32)
    dt = jnp.exp(log_dt.astype(F32))[..., None]
    mag = jnp.exp(lam_re * dt)
    a_re = mag * jnp.cos(lam_im * dt)
    a_im = mag * jnp.sin(lam_im * dt)
    den = lam_re * lam_re + lam_im * lam_im
    f_re = ((a_re - 1.0) * lam_re + a_im * lam_im) / den
    f_im = (a_im * lam_re - (a_re - 1.0) * lam_im) / den
    bb_re = f_re[..., None] * b_re - f_im[..., None] * b_im
    bb_im = f_re[..., None] * b_im + f_im[..., None] * b_re

    pw_re, pw_im = jnp.ones_like(a_re)[None], jnp.zeros_like(a_im)[None]
    an_re, an_im = a_re, a_im
    while pw_re.shape[0] < S5_LC:
        pw_re, pw_im = (jnp.concatenate([pw_re, pw_re * an_re - pw_im * an_im]),
                        jnp.concatenate([pw_im, pw_re * an_im + pw_im * an_re]))
        an_re, an_im = an_re * an_re - an_im * an_im, 2.0 * an_re * an_im
    pw_re = jnp.concatenate([pw_re, an_re[None]])
    pw_im = jnp.concatenate([pw_im, an_im[None]])

    ab_re = pw_re[..., None] * bb_re[None] - pw_im[..., None] * bb_im[None]
    ab_im = pw_re[..., None] * bb_im[None] + pw_im[..., None] * bb_re[None]
    def lag_minor(z, d):
        return z[:S5_LC, d].transpose(1, 2, 0, 3).reshape(S5_G, 1, S5_P, S5_CW)
    def kernels(d):
        k = jnp.sum(c_re[d][..., None] * lag_minor(ab_re, d)
                    - c_im[d][..., None] * lag_minor(ab_im, d), axis=2)
        return k.reshape(S5_G, S5_CG, S5_LC, S5_CG).transpose(0, 3, 2, 1)
    zpad = jnp.zeros((S5_G, S5_CG, S5_LC, S5_CG), F32)
    kf = jnp.concatenate([zpad, kernels(0)], axis=2)
    kb = jnp.concatenate([jnp.flip(kernels(1), axis=2), zpad], axis=2)
    tc = jnp.stack([kf[:, :, S5_LC - s:2 * S5_LC - s] + kb[:, :, S5_LC - 1 - s:2 * S5_LC - 1 - s]
                    for s in range(S5_LC)], axis=1).reshape(S5_G, S5_CW, S5_CW)

    def st(arr, d, flip):
        z = arr[:S5_LC, d]
        z = jnp.flip(z, axis=0) if flip else z
        return z.transpose(1, 0, 3, 2).reshape(S5_G, S5_CW, S5_P)
    et = jnp.concatenate([st(ab_re, 0, True), st(ab_re, 1, False),
                          st(ab_im, 0, True), st(ab_im, 1, False)], axis=-1)

    def rd(d, flip):
        pr_ = pw_re[1:, d]
        pi_ = pw_im[1:, d]
        if flip:
            pr_, pi_ = jnp.flip(pr_, axis=0), jnp.flip(pi_, axis=0)
        cr = c_re[d][None] * pr_[:, :, None, :] - c_im[d][None] * pi_[:, :, None, :]
        ci = c_re[d][None] * pi_[:, :, None, :] + c_im[d][None] * pr_[:, :, None, :]
        to = lambda z: z.transpose(1, 3, 0, 2).reshape(S5_G, S5_P, S5_CW)
        return to(cr), to(-ci)
    fr, fi = rd(0, False)
    br, bi = rd(1, True)
    z = jnp.zeros_like(fr)
    ft = jnp.concatenate([fr, z, fi, z, z, br, z, bi], axis=1)

    a16_re = jnp.concatenate([pw_re[S5_LC, 0], pw_re[S5_LC, 1]], axis=-1)[:, None, :]
    a16_im = jnp.concatenate([pw_im[S5_LC, 0], pw_im[S5_LC, 1]], axis=-1)[:, None, :]
    return tc.astype(BF16), et.astype(BF16), ft.astype(BF16), a16_re, a16_im


def _uproj_kernel(w_ref, h_ref, o_ref, wt_ref):
    @pl.when((pl.program_id(0) == 0) & (pl.program_id(1) == 0))
    def _():
        wt_ref[...] = w_ref[...].T.astype(BF16)

    acc = lax.dot_general(wt_ref[...], h_ref[...], _NT_DIMS, preferred_element_type=F32)
    o_ref[...] = acc.reshape(S5_G, S5_CG, S5_NT).astype(o_ref.dtype)


def _s5_uproj(h_tl, w_in, col_block):
    return pl.pallas_call(
        _uproj_kernel,
        grid=(S5_LC, S5_N // S5_NT),
        in_specs=[pl.BlockSpec((None, D, S5_W), lambda t, n: (0, 0, col_block)),
                  pl.BlockSpec((None, S5_NT, D), lambda t, n: (t, n, 0))],
        out_specs=pl.BlockSpec((S5_G, None, S5_CG, S5_NT), lambda t, n: (0, t, 0, n)),
        out_shape=jax.ShapeDtypeStruct((S5_G, S5_LC, S5_CG, S5_N), BF16),
        scratch_shapes=[pltpu.VMEM((S5_W, D), BF16)],
        compiler_params=_params("arbitrary", "arbitrary"),
        name="s5_uproj",
    )(w_in, h_tl)


S5_PL = S5_NLAT + 4
S5_PC = S5_NCTX + 4
S5_CB = B * S5_PL
S5_ROWS = S5_CB + B * S5_PC


def _s5_chunk_rows(kind, k):
    if kind == "c":
        return pl.ds(S5_CB + k, B, stride=S5_PC)
    return pl.ds(k, B, stride=S5_PL)


def _s5_batch_rows():
    runs = [(b * S5_NLAT, b * S5_PL, S5_NLAT) for b in range(B)]
    runs += [(B * S5_NLAT + b * S5_NCTX, S5_CB + b * S5_PC, S5_NCTX) for b in range(B)]
    return runs


def _s5_kernel(ut_ref, tc_ref, et_ref, ft_ref, ar_ref, ai_ref, d_ref, o_ref,
               he_re_ref, he_im_ref, hpf_re_ref, hpf_im_ref, hpb_re_ref, hpb_im_ref):
    sw = 2 * S5_P
    utf = ut_ref[...].reshape(S5_CW, S5_N).astype(F32)
    un = utf.T.astype(BF16)
    y = jnp.dot(un, tc_ref[...], preferred_element_type=F32)
    he = jnp.dot(un, et_ref[...], preferred_element_type=F32)
    for src, dst, n in _s5_batch_rows():
        he_re_ref[pl.ds(dst, n), :] = he[src:src + n, :sw]
        he_im_ref[pl.ds(dst, n), :] = he[src:src + n, sw:]

    ar = ar_ref[...]
    ai = ai_ref[...]
    is_fwd = lax.broadcasted_iota(jnp.int32, (B, 2 * S5_P), 1) < S5_P
    h_re = jnp.zeros((B, 2 * S5_P), F32)
    h_im = jnp.zeros((B, 2 * S5_P), F32)
    fwd = [("c", k) for k in range(S5_NCTX)] + [("l", k) for k in range(S5_NLAT)]
    bwd = ([("c", k) for k in range(S5_NCTX - 1, -1, -1)]
           + [("l", k) for k in range(S5_NLAT - 1, -1, -1)])
    for cf, cb in zip(fwd, bwd):
        rf = _s5_chunk_rows(*cf)
        rb = _s5_chunk_rows(*cb)
        hpf_re_ref[rf, :] = h_re
        hpf_im_ref[rf, :] = h_im
        hpb_re_ref[rb, :] = h_re
        hpb_im_ref[rb, :] = h_im
        e_re = jnp.where(is_fwd, he_re_ref[rf, :], he_re_ref[rb, :])
        e_im = jnp.where(is_fwd, he_im_ref[rf, :], he_im_ref[rb, :])
        n_re = ar * h_re - ai * h_im + e_re
        n_im = ar * h_im + ai * h_re + e_im
        h_re, h_im = n_re, n_im

    def chunk_order(ref):
        return jnp.concatenate([ref[pl.ds(dst, n), :] for _, dst, n in _s5_batch_rows()], axis=0)
    hp = jnp.concatenate([chunk_order(r) for r in (hpf_re_ref, hpf_im_ref, hpb_re_ref, hpb_im_ref)],
                         axis=1).astype(BF16)
    y = y + jnp.dot(hp, ft_ref[...], preferred_element_type=F32)
    g = y.T + d_ref[...] * utf
    gl = 0.5 * g * (1.0 + lax.erf(g * (0.5 ** 0.5)))
    o_ref[...] = gl.astype(o_ref.dtype).reshape(S5_LC, S5_CG, S5_N)


def _s5_scan(ut, mats, d_col):
    tc, et, ft, a_re, a_im = mats
    mat = pl.BlockSpec((None, S5_CW, S5_CW), lambda g: (g, 0, 0))
    vec = pl.BlockSpec((None, 1, 2 * S5_P), lambda g: (g, 0, 0))
    io = pl.BlockSpec((None, S5_LC, S5_CG, S5_N), lambda g: (g, 0, 0, 0))
    return pl.pallas_call(
        _s5_kernel,
        grid=(S5_G,),
        in_specs=[io, mat, mat,
                  pl.BlockSpec((None, 2 * S5_CW, S5_CW), lambda g: (g, 0, 0)), vec, vec,
                  pl.BlockSpec((None, S5_CW, 1), lambda g: (g, 0, 0))],
        out_specs=io,
        out_shape=jax.ShapeDtypeStruct((S5_G, S5_LC, S5_CG, S5_N), BF16),
        scratch_shapes=[pltpu.VMEM((S5_ROWS, 2 * S5_P), F32)] * 6,
        compiler_params=_params("parallel"),
        name="s5_scan",
    )(ut, tc, et, ft, a_re, a_im, d_col)


def _glu_kernel(gl_ref, w_ref, b_ref, o_ref):
    gl = gl_ref[...].reshape(S5_W, S5_NT)
    z = jnp.dot(w_ref[...], gl, preferred_element_type=F32) + b_ref[...]
    s = gl.astype(F32) * jax.nn.sigmoid(z)
    o_ref[...] = s.T.astype(o_ref.dtype)


def _s5_glu(glt, w_t, b_col):
    return pl.pallas_call(
        _glu_kernel,
        grid=(S5_LC, S5_N // S5_NT),
        in_specs=[pl.BlockSpec((S5_G, None, S5_CG, S5_NT), lambda t, n: (0, t, 0, n)),
                  pl.BlockSpec((S5_W, S5_W), lambda t, n: (0, 0)),
                  pl.BlockSpec((S5_W, 1), lambda t, n: (0, 0))],
        out_specs=pl.BlockSpec((S5_NT, S5_W), lambda t, n: (n, t)),
        out_shape=jax.ShapeDtypeStruct((S5_N, S5_LC * S5_W), BF16),
        compiler_params=_params("parallel", "parallel"),
        name="s5_glu",
    )(glt, w_t, b_col)


def _rope_tables():
    half = GLA_DK // 2
    freqs = ROPE_BASE ** (-np.arange(0, half, 2, dtype=np.float32) / half)
    out = []
    for n in (ROWS, GRID_W):
        ang = np.arange(n, dtype=np.float32)[:, None] * freqs[None, :]
        c, s = np.cos(ang), np.sin(ang)
        out += [np.concatenate([c, c], axis=-1), np.concatenate([-s, s], axis=-1)]
    return tuple(jnp.asarray(t, F32) for t in out)


def _rope(x, rcos, rsin, ccos, csin):
    x0 = x[:, :128]
    x1 = x[:, 128:]
    return jnp.concatenate([x0 * rcos + pltpu.roll(x0, 64, axis=1) * rsin,
                            x1 * ccos + pltpu.roll(x1, 64, axis=1) * csin], axis=-1)


def _cumsum_rows(x, reverse):
    row = lax.broadcasted_iota(jnp.int32, x.shape, 0)
    s = 1
    while s < GLA_C:
        if s >= 8:
            if reverse:
                x = jnp.concatenate([x[:GLA_C - s] + x[s:], x[GLA_C - s:]], axis=0)
            else:
                x = jnp.concatenate([x[:s], x[s:] + x[:GLA_C - s]], axis=0)
        elif reverse:
            x = x + jnp.where(row < GLA_C - s, pltpu.roll(x, GLA_C - s, axis=0), 0.0)
        else:
            x = x + jnp.where(row >= s, pltpu.roll(x, s, axis=0), 0.0)
        s *= 2
    return x


def _log_sigmoid(x):
    return jnp.minimum(x, 0.0) - jnp.log(1.0 + jnp.exp(-jnp.abs(x)))


def _chunk_rows(c):
    if isinstance(c, int):
        return pl.ds(c * GLA_C, GLA_C)
    return pl.ds(pl.multiple_of(c * GLA_C, GLA_C), GLA_C)


def _gla_kernel(ql_ref, kl_ref, vl_ref, gl_ref, qc_ref, kc_ref, vc_ref, al_ref, ac_ref,
                waf_ref, wab_ref, baf_ref, bab_ref, rcos_ref, rsin_ref, ccos_ref, csin_ref,
                ng_ref, o_ref,
                qi_f, ki_f, ke_f, dec_f, st_f, qi_b, ki_b, ke_b, dec_b, st_b, acc_ref):
    n_ctx = L // GLA_C
    n_lat = S // GLA_C
    qscale = GLA_DK ** -0.5
    fwd = (waf_ref, baf_ref, qi_f, ki_f, ke_f, dec_f, False)
    bwd = (wab_ref, bab_ref, qi_b, ki_b, ke_b, dec_b, True)

    ii = lax.broadcasted_iota(jnp.int32, (GLA_C, GLA_C), 0)
    jj = lax.broadcasted_iota(jnp.int32, (GLA_C, GLA_C), 1)

    def prepare(direction, q, k, a, c):
        wa_ref, ba_ref, qi, ki, ke, dec, reverse = direction
        dst = _chunk_rows(c)
        la = _log_sigmoid(jnp.dot(a, wa_ref[...], preferred_element_type=F32)
                          + ba_ref[...]) / GLA_TAU
        bc = _cumsum_rows(la, reverse)
        b_last = bc[0:1, :] if reverse else bc[GLA_C - 1:GLA_C, :]
        qi[dst, :] = (q * jnp.exp(bc)).astype(BF16)
        ki[dst, :] = (k * jnp.exp(-bc)).astype(BF16)
        ke[dst, :] = (k * jnp.exp(b_last - bc)).astype(BF16)
        dec[pl.ds(c, 1), :] = jnp.exp(b_last)

    def prepare_latent(direction, c):
        r = _chunk_rows(c)
        tabs = (rcos_ref[pl.ds(c, 1), :], rsin_ref[pl.ds(c, 1), :], ccos_ref[...], csin_ref[...])
        q = _rope(ql_ref[r, :].astype(F32) * qscale, *tabs)
        k = _rope(kl_ref[r, :].astype(F32), *tabs)
        prepare(direction, q, k, al_ref[r, :], n_ctx + c)

    def prepare_context(direction, c):
        r = _chunk_rows(c)
        prepare(direction, qc_ref[r, :].astype(F32) * qscale, kc_ref[r, :].astype(F32),
                ac_ref[r, :], c)

    prepare_context(fwd, 0)
    prepare_context(bwd, n_ctx - 1)
    prepare_latent(fwd, 0)
    prepare_latent(bwd, n_lat - 1)

    def advance(direction, st_ref, c, v, want_out):
        _, _, qi, ki, ke, dec, reverse = direction
        r = _chunk_rows(c)
        st = st_ref[...]
        o = None
        if want_out:
            q_in = qi[r, :]
            att = lax.dot_general(q_in, ki[r, :], _NT_DIMS, preferred_element_type=F32)
            att = jnp.where((ii <= jj) if reverse else (ii >= jj), att, 0.0)
            o = (jnp.dot(att.astype(BF16), v, preferred_element_type=F32)
                 + lax.dot_general(q_in, st.astype(BF16), _NT_DIMS, preferred_element_type=F32))
        st_ref[...] = dec[pl.ds(c, 1), :] * st + lax.dot_general(
            v, ke[r, :], (((0,), (0,)), ((), ())), preferred_element_type=F32)
        return o

    st_f[...] = jnp.zeros_like(st_f)
    st_b[...] = jnp.zeros_like(st_b)

    def ctx_pair(j, carry):
        cb = n_ctx - 1 - j
        advance(fwd, st_f, j, vc_ref[_chunk_rows(j), :], False)
        advance(bwd, st_b, cb, vc_ref[_chunk_rows(cb), :], False)
        prepare_context(fwd, jnp.minimum(j + 1, n_ctx - 1))
        prepare_context(bwd, jnp.maximum(cb - 1, 0))
        return carry
    lax.fori_loop(0, n_ctx, ctx_pair, 0, unroll=2)

    def lat_pair(j, accumulate):
        cb = n_lat - 1 - j
        rf = _chunk_rows(j)
        rb = _chunk_rows(cb)
        o_f = advance(fwd, st_f, n_ctx + j, vl_ref[rf, :], True)
        o_b = advance(bwd, st_b, n_ctx + cb, vl_ref[rb, :], True)
        if accumulate:
            acc_ref[rf, :] += o_f
            acc_ref[rb, :] += o_b
        else:
            acc_ref[rf, :] = o_f
            acc_ref[rb, :] = o_b
        prepare_latent(fwd, jnp.minimum(j + 1, n_lat - 1))
        prepare_latent(bwd, jnp.maximum(cb - 1, 0))

    def lat_first(j, carry):
        lat_pair(j, False)
        return carry
    lax.fori_loop(0, n_lat // 2, lat_first, 0, unroll=2)

    def lat_second(j, carry):
        lat_pair(j, True)
        return carry
    lax.fori_loop(n_lat // 2, n_lat, lat_second, 0, unroll=2)

    tr = 256

    def fin(t, carry):
        r = pl.ds(pl.multiple_of(t * tr, tr), tr)
        o = acc_ref[r, :]
        ms = jnp.mean(o * o, axis=-1, keepdims=True)
        g = gl_ref[r, :].astype(F32)
        o_ref[r, :] = (o * lax.rsqrt(ms + EPS) * ng_ref[...]
                       * (g * jax.nn.sigmoid(g))).astype(o_ref.dtype)
        return carry
    lax.fori_loop(0, S // tr, fin, 0)


def _gla(qkvg, acode, wa, ba, rope, norm_g):
    ctx0 = NLAT // L
    kq = GLA_QK // GLA_DK
    half = GLA_DK // 2
    n_chunks = (L + S) // GLA_C
    per_dir = [pltpu.VMEM((L + S, GLA_DK), BF16)] * 3 + [pltpu.VMEM((n_chunks, GLA_DK), F32),
                                                         pltpu.VMEM((GLA_DV, GLA_DK), F32)]
    in_specs = [
        pl.BlockSpec((S, GLA_DK), lambda b, h: (b, h)),
        pl.BlockSpec((S, GLA_DK), lambda b, h: (b, kq + h)),
        pl.BlockSpec((S, GLA_DV), lambda b, h: (b, kq + h)),
        pl.BlockSpec((S, GLA_DV), lambda b, h: (b, 2 * kq + h)),
        pl.BlockSpec((L, GLA_DK), lambda b, h: (ctx0 + b, h)),
        pl.BlockSpec((L, GLA_DK), lambda b, h: (ctx0 + b, kq + h)),
        pl.BlockSpec((L, GLA_DV), lambda b, h: (ctx0 + b, kq + h)),
        pl.BlockSpec((S, 128), lambda b, h: (b, 0)),
        pl.BlockSpec((L, 128), lambda b, h: (ctx0 + b, 0)),
        pl.BlockSpec((128, GLA_DK), lambda b, h: (0, h)),
        pl.BlockSpec((128, GLA_DK), lambda b, h: (0, kq + h)),
        pl.BlockSpec((1, GLA_DK), lambda b, h: (0, h)),
        pl.BlockSpec((1, GLA_DK), lambda b, h: (0, kq + h)),
        pl.BlockSpec((ROWS, half), lambda b, h: (0, 0)),
        pl.BlockSpec((ROWS, half), lambda b, h: (0, 0)),
        pl.BlockSpec((GRID_W, half), lambda b, h: (0, 0)),
        pl.BlockSpec((GRID_W, half), lambda b, h: (0, 0)),
        pl.BlockSpec((1, GLA_DV), lambda b, h: (0, 0)),
    ]
    return pl.pallas_call(
        _gla_kernel,
        grid=(B, GLA_H),
        in_specs=in_specs,
        out_specs=pl.BlockSpec((S, GLA_DV), lambda b, h: (b, h)),
        out_shape=jax.ShapeDtypeStruct((NLAT, GLA_VW), BF16),
        scratch_shapes=per_dir + per_dir + [pltpu.VMEM((S, GLA_DV), F32)],
        compiler_params=_params("parallel", "parallel"),
        name="gla",
    )(qkvg, qkvg, qkvg, qkvg, qkvg, qkvg, qkvg, acode, acode, wa, wa, ba, ba, *rope,
      norm_g.reshape(1, GLA_DV))


def kernel(x, c, ctx, c_ctx, ada_w, ada_b, norm1_g, norm2_g, mlp_w1, mlp_w2, final_g, ab_w_in, ab_w_out, na_rel_bias, s5_lambda_re, s5_lambda_im, s5_log_dt, s5_b_re, s5_b_im, s5_c_re, s5_c_im, s5_d, s5_glu_w, s5_glu_b, gla_w_in, gla_w_a2, gla_b_a, gla_norm_g, gla_w_out):
    xs = (x.astype(F32).reshape(NLAT, D), ctx.astype(F32).reshape(NCTX, D))
    cvec = jnp.zeros((16, D), F32).at[:B].set(c.astype(F32)).at[B].set(c_ctx.astype(F32))
    mods = _ada_mod(cvec, ada_w, ada_b).reshape(2, 16, 1, 6 * D)
    bf = lambda w: w.astype(BF16)

    mod = mods[0]
    h, h_tl = _normmod(xs, mod, norm1_g[0], 0, 1, NT)
    qkv, w_ab_out = _mm(h, ab_w_in, 0, 3 * NA_W, NT, side=(ab_w_out, 0))
    att = _na_attention(qkv, _na_tables(na_rel_bias[0]))
    ut = _s5_uproj(h_tl, ab_w_in, 3 * NA_W // S5_W)
    mats = _s5_matrices(s5_lambda_re[0], s5_lambda_im[0], s5_log_dt[0], s5_b_re[0], s5_b_im[0],
                        s5_c_re[0], s5_c_im[0])
    d_col = jnp.tile(s5_d[0].astype(F32).reshape(S5_G, 1, S5_CG), (1, S5_LC, 1)).reshape(S5_G, S5_CW, 1)
    glt = _s5_scan(ut, mats, d_col)
    s5 = _s5_glu(glt, bf(s5_glu_w[0].T), s5_glu_b[0].astype(F32).reshape(S5_W, 1))
    xs, h = _mm_res([att, s5], w_ab_out, 0, xs, mod, 2, NT, nxt=(mod, norm2_g[0], 3, 4))
    hid, w2 = _mm(h, mlp_w1, 0, MLP_H, NT, relu2=True, side=(mlp_w2, 0))
    xs, h = _mm_res([hid], w2, 0, xs, mod, 5, NT, nxt=(mods[1], norm1_g[1], 0, 1))

    mod = mods[1]
    w_in = gla_w_in[0]
    qkvg, w_gla_out = _mm(h, gla_w_in, 0, GLA_MAIN, NT, side=(gla_w_out, 0))
    w_code = jnp.zeros((1, D, 128), F32).at[0, :, :2 * GLA_RANK].set(w_in[:, GLA_MAIN:])
    acode = _mm(h, w_code, 0, 128, NT)
    wa = (jnp.zeros((128, 2 * GLA_QK), F32)
          .at[:GLA_RANK, :GLA_QK].set(gla_w_a2[0, 0])
          .at[GLA_RANK:2 * GLA_RANK, GLA_QK:].set(gla_w_a2[0, 1]))
    ba = gla_b_a[0].astype(F32).reshape(1, 2 * GLA_QK)
    og = _gla(qkvg, acode, bf(wa), ba, _rope_tables(), gla_norm_g[0].astype(F32))
    xl, h = _mm_res([og], w_gla_out, 0, xs, mod, 2, NLAT, nxt=(mod, norm2_g[1], 3, 4))
    hid, w2 = _mm(h, mlp_w1, 1, MLP_H, NLAT, relu2=True, side=(mlp_w2, 1))
    out = _mm_res([hid], w2, 0, xl, mod, 5, NLAT, final_g=final_g.astype(F32))
    return out.reshape(B, S, D).astype(x.dtype)
```

```python
import functools
import math

import numpy as np
import jax
import jax.numpy as jnp
from jax import lax
from jax.experimental import pallas as pl
from jax.experimental.pallas import tpu as pltpu

F32 = jnp.float32
BF16 = jnp.bfloat16

D = 2048
B = 8
S = 2048
L = 256
GRID_W = 64
ROWS = S // GRID_W
NLAT = B * S
NCTX = B * L
NT = NLAT + NCTX
MLP_H = 4 * D
EPS = 1e-6
NEG_INF = -1e30

NA_H = 8
NA_DH = 128
NA_W = NA_H * NA_DH
NA_SCALE = NA_DH ** -0.5
NA_HB = 4
NA_QROWS = 4
NA_KROWS = 12
NA_QT = NA_QROWS * GRID_W
NA_KT = NA_KROWS * GRID_W

S5_W = D // 2
S5_CG = 16
S5_G = S5_W // S5_CG
S5_P = 64
S5_LC = 16
S5_CW = S5_LC * S5_CG
S5_NLAT = S // S5_LC
S5_NCTX = L // S5_LC
S5_N = B * (S5_NLAT + S5_NCTX)
S5_NT = S5_N

GLA_H = 4
GLA_DK = 256
GLA_DV = 512
GLA_QK = GLA_H * GLA_DK
GLA_VW = GLA_H * GLA_DV
GLA_RANK = 16
GLA_TAU = 16.0
GLA_C = 64
GLA_MAIN = 2 * GLA_QK + 2 * GLA_VW
ROPE_BASE = 10000.0

MM_RES_NSPLIT = 4

VMEM_LIMIT = 60 * 1024 * 1024

_NT_DIMS = (((1,), (1,)), ((), ()))


def _params(*sem):
    return pltpu.CompilerParams(dimension_semantics=sem, vmem_limit_bytes=VMEM_LIMIT)


def _mod_row(i, tm):
    return jnp.minimum((i * tm) // S, B)


def _ada_kernel(c_ref, w_ref, b_ref, o_ref):
    c = c_ref[...]
    s = c * jax.nn.sigmoid(c)
    o_ref[...] = jnp.dot(s.astype(BF16), w_ref[...].astype(BF16),
                         preferred_element_type=F32) + b_ref[...]


def _ada_mod(cvec, ada_w, ada_b):
    depth = ada_w.shape[0]
    tn = 1024
    return pl.pallas_call(
        _ada_kernel,
        grid=(depth, 6 * D // tn),
        in_specs=[pl.BlockSpec((16, D), lambda l, j: (0, 0)),
                  pl.BlockSpec((None, D, tn), lambda l, j: (l, 0, j)),
                  pl.BlockSpec((None, 1, tn), lambda l, j: (l, 0, j))],
        out_specs=pl.BlockSpec((None, 16, tn), lambda l, j: (l, 0, j)),
        out_shape=jax.ShapeDtypeStruct((depth, 16, 6 * D), F32),
        compiler_params=_params("parallel", "parallel"),
        name="ada_mod",
    )(cvec, ada_w, ada_b.reshape(depth, 1, 6 * D))


def _stream_specs(xs, tm, two_axes):
    if not isinstance(xs, tuple):
        imap = (lambda i, k: (i, 0)) if two_axes else (lambda i: (i, 0))
        return 0, [xs], [pl.BlockSpec((tm, D), imap)]
    nl = xs[0].shape[0] // tm
    if two_axes:
        maps = [lambda i, k: (jnp.minimum(i, nl - 1), 0), lambda i, k: (jnp.maximum(i - nl, 0), 0)]
    else:
        maps = [lambda i: (jnp.minimum(i, nl - 1), 0), lambda i: (jnp.maximum(i - nl, 0), 0)]
    return nl, list(xs), [pl.BlockSpec((tm, D), m) for m in maps]


def _stream_tile(x_refs, n_lat_tiles, rows=slice(None), cols=slice(None)):
    if len(x_refs) == 1:
        return x_refs[0][rows, cols]
    return jnp.where(pl.program_id(0) < n_lat_tiles, x_refs[0][rows, cols], x_refs[1][rows, cols])


def _normmod_kernel(*refs, n_x, n_lat_tiles):
    x_refs = refs[:n_x]
    sh_ref, sc_ref, g_ref, perm_ref, o_ref, oc_ref = refs[n_x:]
    x = _stream_tile(x_refs, n_lat_tiles)
    ms = jnp.mean(x * x, axis=-1, keepdims=True)
    h = x * lax.rsqrt(ms + EPS) * g_ref[...]
    h = (h * (1.0 + sc_ref[...]) + sh_ref[...]).astype(o_ref.dtype)
    o_ref[...] = h
    hp = jnp.dot(perm_ref[...], h, preferred_element_type=F32)
    oc_ref[...] = hp.reshape(oc_ref.shape).astype(oc_ref.dtype)


def _chunk_row_perm(tm):
    r = np.arange(tm)
    perm = np.zeros((tm, tm), np.float32)
    perm[(r % S5_LC) * (tm // S5_LC) + r // S5_LC, r] = 1.0
    return perm


def _normmod(xs, mod, g, shift_idx, scale_idx, rows):
    tm = 512
    nl, x_args, x_specs = _stream_specs(xs, tm, False)
    perm = _chunk_row_perm(tm)
    return pl.pallas_call(
        functools.partial(_normmod_kernel, n_x=len(x_args), n_lat_tiles=nl),
        grid=(rows // tm,),
        in_specs=x_specs + [
            pl.BlockSpec((None, 1, D), lambda i: (_mod_row(i, tm), 0, shift_idx)),
            pl.BlockSpec((None, 1, D), lambda i: (_mod_row(i, tm), 0, scale_idx)),
            pl.BlockSpec((1, D), lambda i: (0, 0)),
            pl.BlockSpec((tm, tm), lambda i: (0, 0))],
        out_specs=[pl.BlockSpec((tm, D), lambda i: (i, 0)),
                   pl.BlockSpec((S5_LC, tm // S5_LC, D), lambda i: (0, i, 0))],
        out_shape=[jax.ShapeDtypeStruct((rows, D), BF16),
                   jax.ShapeDtypeStruct((S5_LC, rows // S5_LC, D), BF16)],
        compiler_params=_params("parallel"),
        name="normmod",
    )(*x_args, mod, mod, g.reshape(1, D), jnp.asarray(perm, BF16))


def _mm_kernel(*refs, relu2, cast_side):
    if cast_side:
        a_ref, w_ref, side_ref, o_ref, side_o_ref, wb_ref = refs
        side_o_ref[...] = side_ref[...].astype(BF16)
    else:
        a_ref, w_ref, o_ref, wb_ref = refs

    @pl.when(pl.program_id(1) == 0)
    def _():
        wb_ref[...] = w_ref[...].astype(BF16)

    acc = jnp.dot(a_ref[...], wb_ref[...], preferred_element_type=F32)
    if relu2:
        acc = jnp.square(jnp.maximum(acc, 0.0))
    o_ref[...] = acc.astype(o_ref.dtype)


MM_SIDE_ROWS = 128


def _mm(a, w, layer, n, rows, *, relu2=False, side=None, tm=2048, tn=1024):
    k = a.shape[1]
    tn = min(tn, n)
    n_i = rows // tm
    in_specs = [pl.BlockSpec((tm, k), lambda j, i: (i, 0)),
                pl.BlockSpec((None, k, tn), lambda j, i: (layer, 0, j))]
    out_specs = pl.BlockSpec((tm, tn), lambda j, i: (i, j))
    out_shape = jax.ShapeDtypeStruct((rows, n), BF16)
    args = [a, w]
    if side is not None:
        w2, layer2 = side
        n_blk = w2.shape[1] // MM_SIDE_ROWS
        assert n_blk <= (n // tn) * n_i
        blk = lambda j, i: jnp.minimum(j * n_i + i, n_blk - 1)
        in_specs.append(pl.BlockSpec((None, MM_SIDE_ROWS, D), lambda j, i: (layer2, blk(j, i), 0)))
        out_specs = [out_specs, pl.BlockSpec((None, MM_SIDE_ROWS, D), lambda j, i: (0, blk(j, i), 0))]
        out_shape = [out_shape, jax.ShapeDtypeStruct((1, w2.shape[1], D), BF16)]
        args.append(w2)
    return pl.pallas_call(
        functools.partial(_mm_kernel, relu2=relu2, cast_side=side is not None),
        grid=(n // tn, n_i),
        in_specs=in_specs,
        out_specs=out_specs,
        out_shape=out_shape,
        scratch_shapes=[pltpu.VMEM((k, tn), BF16)],
        compiler_params=_params("parallel", "arbitrary"),
        name="mm_relu2" if relu2 else "mm",
    )(*args)


def _mm_res_kernel(*refs, n_lhs, n_res, n_lat_tiles, nk, final_norm, next_norm, chunked):
    a_refs = refs[:n_lhs]
    w_refs = refs[n_lhs:2 * n_lhs]
    res_refs = refs[2 * n_lhs:2 * n_lhs + n_res]
    gate_ref = refs[2 * n_lhs + n_res]
    pos = 2 * n_lhs + n_res + 1
    n_extra = 1 if final_norm else (3 if next_norm else 0)
    extra = refs[pos:pos + n_extra]
    pos += n_extra
    if any(chunked):
        perm_ref = refs[pos]
        pos += 1
    o_ref = refs[pos]
    h_ref = refs[pos + 1] if next_norm else None

    a_vals = []
    for a_ref, is_chunked in zip(a_refs, chunked):
        if is_chunked:
            kw = a_ref.shape[1] // S5_LC
            stacked = jnp.concatenate([a_ref[:, tl * kw:(tl + 1) * kw] for tl in range(S5_LC)],
                                      axis=0)
            a_vals.append(jnp.dot(perm_ref[...], stacked,
                                  preferred_element_type=F32).astype(stacked.dtype))
        else:
            a_vals.append(a_ref)

    def normed(y, coef):
        ms = jnp.mean(y * y, axis=-1, keepdims=True)
        return y * lax.rsqrt(ms + EPS) * coef

    def epilogue(rows):
        if final_norm:
            o_ref[rows, :] = normed(o_ref[rows, :], extra[0][...])
        elif next_norm:
            sh_ref, sc_ref, g_ref = extra
            coef = g_ref[...] * (1.0 + sc_ref[...])
            h_ref[rows, :] = (normed(o_ref[rows, :], coef) + sh_ref[...]).astype(h_ref.dtype)

    wn = D // MM_RES_NSPLIT

    def update(rows, first):
        for cc in range(MM_RES_NSPLIT):
            cols = slice(cc * wn, (cc + 1) * wn)
            part = None
            for a_val, w_ref in zip(a_vals, w_refs):
                d = jnp.dot(a_val[rows, :], w_ref[:, cols], preferred_element_type=F32)
                part = d if part is None else part + d
            part = gate_ref[:, cols] * part
            if first:
                o_ref[rows, cols] = _stream_tile(res_refs, n_lat_tiles, rows, cols) + part
            else:
                o_ref[rows, cols] += part

    tm = o_ref.shape[0]
    if nk == 1:
        for rh in range(2):
            rows = slice(rh * (tm // 2), (rh + 1) * (tm // 2))
            update(rows, True)
            epilogue(rows)
        return

    kk = pl.program_id(1)
    every = slice(0, tm)

    @pl.when(kk == 0)
    def _():
        o_ref[...] = _stream_tile(res_refs, n_lat_tiles)

    update(every, False)
    if final_norm or next_norm:
        pl.when(kk == nk - 1)(functools.partial(epilogue, every))


def _mm_res(a_list, w, layer, resid, mod, gate_idx, rows, *, final_g=None, nxt=None,
            tm=512, tk=2048):
    n_lhs = len(a_list)
    kdim = a_list[0].shape[1]
    chunked = tuple(a.shape[0] * S5_LC == rows and a.shape[1] == S5_LC * kdim for a in a_list)
    tk = min(kdim, tk)
    nk = kdim // tk
    assert nk == 1 or not any(chunked)
    nl, res_args, res_specs = _stream_specs(resid, tm, True)

    def mod_spec(idx):
        return pl.BlockSpec((None, 1, D), lambda i, k: (_mod_row(i, tm), 0, idx))

    def w_spec(j):
        return pl.BlockSpec((None, tk, D), lambda i, k: (layer, j * nk + k, 0))

    row_spec = pl.BlockSpec((tm, D), lambda i, k: (i, 0))
    vec_spec = pl.BlockSpec((1, D), lambda i, k: (0, 0))
    def a_spec(is_chunked):
        if is_chunked:
            return pl.BlockSpec((tm // S5_LC, S5_LC * kdim), lambda i, k: (i, 0))
        return pl.BlockSpec((tm, tk), lambda i, k: (i, k))

    in_specs = ([a_spec(c) for c in chunked]
                + [w_spec(j) for j in range(n_lhs)] + res_specs + [mod_spec(gate_idx)])
    args = list(a_list) + [w] * n_lhs + res_args + [mod]
    out_specs, out_shape = row_spec, jax.ShapeDtypeStruct((rows, D), F32)
    if final_g is not None:
        in_specs.append(vec_spec)
        args.append(final_g.reshape(1, D))
    elif nxt is not None:
        mod_n, g_n, shift_idx, scale_idx = nxt
        in_specs += [mod_spec(shift_idx), mod_spec(scale_idx), vec_spec]
        args += [mod_n, mod_n, g_n.reshape(1, D)]
        out_specs = [row_spec, row_spec]
        out_shape = [out_shape, jax.ShapeDtypeStruct((rows, D), BF16)]
    if any(chunked):
        in_specs.append(pl.BlockSpec((tm, tm), lambda i, k: (0, 0)))
        args.append(jnp.asarray(_chunk_row_perm(tm).T, BF16))
    return pl.pallas_call(
        functools.partial(_mm_res_kernel, n_lhs=n_lhs, n_res=len(res_args), n_lat_tiles=nl,
                          nk=nk, final_norm=final_g is not None, next_norm=nxt is not None,
                          chunked=chunked),
        grid=(rows // tm, nk),
        in_specs=in_specs,
        out_specs=out_specs,
        out_shape=out_shape,
        compiler_params=_params("parallel", "arbitrary"),
        name="mm_res",
    )(*args)


def _na_tables(rel_bias):
    hp = lax.Precision.HIGHEST
    cq = np.arange(GRID_W)[:, None]
    ck = np.arange(GRID_W)[None, :]
    ws = np.clip(cq - 8, 0, GRID_W - 16)
    col_ok = (ck >= ws) & (ck < ws + 16)
    col_hot = ((ck - cq + 15)[..., None] == np.arange(31)) & col_ok[..., None]
    blocks = jnp.einsum('hrj,qkj->hrqk', rel_bias.astype(F32), col_hot.astype(np.float32),
                        precision=hp)
    blocks = blocks + np.where(col_ok, 0.0, NEG_INF).astype(np.float32)
    masked = jnp.full((NA_H, GRID_W, GRID_W), NEG_INF, F32)
    pats = []
    for r0, start in ((0, 0), (NA_QROWS, 0), (ROWS - NA_QROWS, ROWS - NA_KROWS)):
        rows = []
        for a in range(NA_QROWS):
            r = r0 + a
            rs = min(max(r - 4, 0), ROWS - 8)
            rows.append(jnp.concatenate(
                [blocks[:, krow - r + 7] if rs <= krow < rs + 8 else masked
                 for krow in range(start, start + NA_KROWS)], axis=-1))
        pats.append(jnp.concatenate(rows, axis=1))
    return jnp.stack(pats, axis=1)


def _na_kernel(q_ref, k0_ref, k1_ref, k2_ref, v0_ref, v1_ref, v2_ref, kc_ref, vc_ref,
               tab_ref, o_ref):
    i = pl.program_id(1)

    def head(ref, hh):
        return ref[:, hh * NA_DH:(hh + 1) * NA_DH]

    def ctx_scores(hh):
        return lax.dot_general(head(q_ref, hh), head(kc_ref, hh), _NT_DIMS,
                               preferred_element_type=F32) * NA_SCALE

    @pl.when(i < ROWS // NA_QROWS)
    def _():
        for hh in range(NA_HB):
            q = head(q_ref, hh)
            s_c = ctx_scores(hh)
            m = jnp.max(s_c, axis=-1, keepdims=True)
            s_w = []
            for d, k_ref in enumerate((k0_ref, k1_ref, k2_ref)):
                s = lax.dot_general(q, head(k_ref, hh), _NT_DIMS,
                                    preferred_element_type=F32) * NA_SCALE
                s = s + tab_ref[hh, :, d * NA_QT:(d + 1) * NA_QT]
                s_w.append(s)
                m = jnp.maximum(m, jnp.max(s, axis=-1, keepdims=True))
            p_c = jnp.exp(s_c - m)
            l = jnp.sum(p_c, axis=-1, keepdims=True)
            o = jnp.dot(p_c.astype(BF16), head(vc_ref, hh), preferred_element_type=F32)
            for s, v_ref in zip(s_w, (v0_ref, v1_ref, v2_ref)):
                p = jnp.exp(s - m)
                l = l + jnp.sum(p, axis=-1, keepdims=True)
                o = o + jnp.dot(p.astype(BF16), head(v_ref, hh), preferred_element_type=F32)
            o_ref[:, hh * NA_DH:(hh + 1) * NA_DH] = (o / l).astype(o_ref.dtype)

    @pl.when(i == ROWS // NA_QROWS)
    def _():
        for hh in range(NA_HB):
            s_c = ctx_scores(hh)
            p_c = jnp.exp(s_c - jnp.max(s_c, axis=-1, keepdims=True))
            l = jnp.sum(p_c, axis=-1, keepdims=True)
            o = jnp.dot(p_c.astype(BF16), head(vc_ref, hh), preferred_element_type=F32)
            o_ref[:, hh * NA_DH:(hh + 1) * NA_DH] = (o / l).astype(o_ref.dtype)


def _na_attention(qkv, table):
    ng = ROWS // NA_QROWS
    blk = S // NA_QT
    ctx0 = NLAT // NA_QT

    def qrow(h, i, b):
        return jnp.where(i < ng, b * blk + i, ctx0 + b)

    nhb = NA_H // NA_HB

    def krow(d):
        return lambda h, i, b: (b * blk + jnp.clip(i - 1, 0, blk - 3) + d, nhb + h)

    def vrow(d):
        return lambda h, i, b: (b * blk + jnp.clip(i - 1, 0, blk - 3) + d, 2 * nhb + h)

    def pat(h, i, b):
        return (h, jnp.where(i == 0, 0, jnp.where(i >= ng - 1, 2, 1)), 0, 0)

    tile = (NA_QT, NA_HB * NA_DH)
    in_specs = ([pl.BlockSpec(tile, lambda h, i, b: (qrow(h, i, b), h))]
                + [pl.BlockSpec(tile, krow(d)) for d in range(3)]
                + [pl.BlockSpec(tile, vrow(d)) for d in range(3)]
                + [pl.BlockSpec(tile, lambda h, i, b: (ctx0 + b, nhb + h)),
                   pl.BlockSpec(tile, lambda h, i, b: (ctx0 + b, 2 * nhb + h)),
                   pl.BlockSpec((NA_HB, None, NA_QT, NA_KT), pat)])
    return pl.pallas_call(
        _na_kernel,
        grid=(nhb, ng + 1, B),
        in_specs=in_specs,
        out_specs=pl.BlockSpec(tile, lambda h, i, b: (qrow(h, i, b), h)),
        out_shape=jax.ShapeDtypeStruct((NT, NA_W), BF16),
        compiler_params=_params("parallel", "parallel", "parallel"),
        name="na_attention",
    )(*([qkv] * 9), table)


def _s5_matrices(lam_re, lam_im, log_dt, b_re, b_im, c_re, c_im):
    lam_re, lam_im = lam_re.astype(F32), lam_im.astype(F32)
    b_re, b_im = b_re.astype(F32), b_im.astype(F32)
    c_re, c_im = c_re.astype(F32), c_im.astype(F32)
    dt = jnp.exp(log_dt.astype(F32))[..., None]
    mag = jnp.exp(lam_re * dt)
    a_re = mag * jnp.cos(lam_im * dt)
    a_im = mag * jnp.sin(lam_im * dt)
    den = lam_re * lam_re + lam_im * lam_im
    f_re = ((a_re - 1.0) * lam_re + a_im * lam_im) / den
    f_im = (a_im * lam_re - (a_re - 1.0) * lam_im) / den
    bb_re = f_re[..., None] * b_re - f_im[..., None] * b_im
    bb_im = f_re[..., None] * b_im + f_im[..., None] * b_re

    pw_re, pw_im = jnp.ones_like(a_re)[None], jnp.zeros_like(a_im)[None]
    an_re, an_im = a_re, a_im
    while pw_re.shape[0] < S5_LC:
        pw_re, pw_im = (jnp.concatenate([pw_re, pw_re * an_re - pw_im * an_im]),
                        jnp.concatenate([pw_im, pw_re * an_im + pw_im * an_re]))
        an_re, an_im = an_re * an_re - an_im * an_im, 2.0 * an_re * an_im
    pw_re = jnp.concatenate([pw_re, an_re[None]])
    pw_im = jnp.concatenate([pw_im, an_im[None]])

    ab_re = pw_re[..., None] * bb_re[None] - pw_im[..., None] * bb_im[None]
    ab_im = pw_re[..., None] * bb_im[None] + pw_im[..., None] * bb_re[None]
    def lag_minor(z):
        return z[:S5_LC].transpose(1, 2, 3, 0, 4).reshape(2, S5_G, 1, S5_P, S5_CW)
    kern = jnp.sum(c_re[..., None] * lag_minor(ab_re)
                   - c_im[..., None] * lag_minor(ab_im), axis=3)
    kern = kern.reshape(2, S5_G, S5_CG, S5_LC, S5_CG).transpose(0, 1, 4, 3, 2)
    zpad = jnp.zeros((S5_G, S5_CG, S5_LC, S5_CG), F32)
    kf = jnp.concatenate([zpad, kern[0]], axis=2)
    kb = jnp.concatenate([jnp.flip(kern[1], axis=2), zpad], axis=2)
    tc = jnp.stack([kf[:, :, S5_LC - s:2 * S5_LC - s] + kb[:, :, S5_LC - 1 - s:2 * S5_LC - 1 - s]
                    for s in range(S5_LC)], axis=1).reshape(S5_G, S5_CW, S5_CW)

    def st(arr, d, flip):
        z = arr[:S5_LC, d]
        z = jnp.flip(z, axis=0) if flip else z
        return z.transpose(1, 0, 3, 2).reshape(S5_G, S5_CW, S5_P)
    et = jnp.concatenate([st(ab_re, 0, True), st(ab_re, 1, False),
                          st(ab_im, 0, True), st(ab_im, 1, False)], axis=-1)

    def rd(d, flip):
        pr_ = pw_re[1:, d]
        pi_ = pw_im[1:, d]
        if flip:
            pr_, pi_ = jnp.flip(pr_, axis=0), jnp.flip(pi_, axis=0)
        cr = c_re[d][None] * pr_[:, :, None, :] - c_im[d][None] * pi_[:, :, None, :]
        ci = c_re[d][None] * pi_[:, :, None, :] + c_im[d][None] * pr_[:, :, None, :]
        to = lambda z: z.transpose(1, 3, 0, 2).reshape(S5_G, S5_P, S5_CW)
        return to(cr), to(-ci)
    fr, fi = rd(0, False)
    br, bi = rd(1, True)
    z = jnp.zeros_like(fr)
    ft = jnp.concatenate([fr, z, fi, z, z, br, z, bi], axis=1)

    a16_re = jnp.concatenate([pw_re[S5_LC, 0], pw_re[S5_LC, 1]], axis=-1)[:, None, :]
    a16_im = jnp.concatenate([pw_im[S5_LC, 0], pw_im[S5_LC, 1]], axis=-1)[:, None, :]
    return tc.astype(BF16), et.astype(BF16), ft.astype(BF16), a16_re, a16_im


def _uproj_kernel(w_ref, h_ref, o_ref, wt_ref):
    @pl.when((pl.program_id(0) == 0) & (pl.program_id(1) == 0))
    def _():
        wt_ref[...] = w_ref[...].T.astype(BF16)

    acc = lax.dot_general(wt_ref[...], h_ref[...], _NT_DIMS, preferred_element_type=F32)
    o_ref[...] = acc.reshape(S5_G, S5_CG, S5_NT).astype(o_ref.dtype)


def _s5_uproj(h_tl, w_in, col_block):
    return pl.pallas_call(
        _uproj_kernel,
        grid=(S5_LC, S5_N // S5_NT),
        in_specs=[pl.BlockSpec((None, D, S5_W), lambda t, n: (0, 0, col_block)),
                  pl.BlockSpec((None, S5_NT, D), lambda t, n: (t, n, 0))],
        out_specs=pl.BlockSpec((S5_G, None, S5_CG, S5_NT), lambda t, n: (0, t, 0, n)),
        out_shape=jax.ShapeDtypeStruct((S5_G, S5_LC, S5_CG, S5_N), BF16),
        scratch_shapes=[pltpu.VMEM((S5_W, D), BF16)],
        compiler_params=_params("arbitrary", "arbitrary"),
        name="s5_uproj",
    )(w_in, h_tl)


S5_PL = S5_NLAT + 4
S5_PC = S5_NCTX + 4
S5_CB = B * S5_PL
S5_ROWS = S5_CB + B * S5_PC


def _s5_chunk_rows(kind, k):
    if kind == "c":
        return pl.ds(S5_CB + k, B, stride=S5_PC)
    return pl.ds(k, B, stride=S5_PL)


def _s5_batch_rows():
    runs = [(b * S5_NLAT, b * S5_PL, S5_NLAT) for b in range(B)]
    runs += [(B * S5_NLAT + b * S5_NCTX, S5_CB + b * S5_PC, S5_NCTX) for b in range(B)]
    return runs


def _s5_kernel(ut_ref, tc_ref, et_ref, ft_ref, ar_ref, ai_ref, d_ref, o_ref,
               he_re_ref, he_im_ref, hpf_re_ref, hpf_im_ref, hpb_re_ref, hpb_im_ref):
    sw = 2 * S5_P
    utf = ut_ref[...].reshape(S5_CW, S5_N).astype(F32)
    un = utf.T.astype(BF16)
    y = jnp.dot(un, tc_ref[...], preferred_element_type=F32)
    he = jnp.dot(un, et_ref[...], preferred_element_type=F32)
    for src, dst, n in _s5_batch_rows():
        he_re_ref[pl.ds(dst, n), :] = he[src:src + n, :sw]
        he_im_ref[pl.ds(dst, n), :] = he[src:src + n, sw:]

    ar = ar_ref[...]
    ai = ai_ref[...]
    is_fwd = lax.broadcasted_iota(jnp.int32, (B, 2 * S5_P), 1) < S5_P
    h_re = jnp.zeros((B, 2 * S5_P), F32)
    h_im = jnp.zeros((B, 2 * S5_P), F32)
    fwd = [("c", k) for k in range(S5_NCTX)] + [("l", k) for k in range(S5_NLAT)]
    bwd = ([("c", k) for k in range(S5_NCTX - 1, -1, -1)]
           + [("l", k) for k in range(S5_NLAT - 1, -1, -1)])
    for cf, cb in zip(fwd, bwd):
        rf = _s5_chunk_rows(*cf)
        rb = _s5_chunk_rows(*cb)
        hpf_re_ref[rf, :] = h_re
        hpf_im_ref[rf, :] = h_im
        hpb_re_ref[rb, :] = h_re
        hpb_im_ref[rb, :] = h_im
        e_re = jnp.where(is_fwd, he_re_ref[rf, :], he_re_ref[rb, :])
        e_im = jnp.where(is_fwd, he_im_ref[rf, :], he_im_ref[rb, :])
        n_re = ar * h_re - ai * h_im + e_re
        n_im = ar * h_im + ai * h_re + e_im
        h_re, h_im = n_re, n_im

    def chunk_order(ref):
        return jnp.concatenate([ref[pl.ds(dst, n), :] for _, dst, n in _s5_batch_rows()], axis=0)
    hp = jnp.concatenate([chunk_order(r) for r in (hpf_re_ref, hpf_im_ref, hpb_re_ref, hpb_im_ref)],
                         axis=1).astype(BF16)
    y = y + jnp.dot(hp, ft_ref[...], preferred_element_type=F32)
    g = y.T + d_ref[...] * utf
    gl = 0.5 * g * (1.0 + lax.erf(g * (0.5 ** 0.5)))
    o_ref[...] = gl.astype(o_ref.dtype).reshape(S5_LC, S5_CG, S5_N)


def _s5_scan(ut, mats, d_col):
    tc, et, ft, a_re, a_im = mats
    mat = pl.BlockSpec((None, S5_CW, S5_CW), lambda g: (g, 0, 0))
    vec = pl.BlockSpec((None, 1, 2 * S5_P), lambda g: (g, 0, 0))
    io = pl.BlockSpec((None, S5_LC, S5_CG, S5_N), lambda g: (g, 0, 0, 0))
    return pl.pallas_call(
        _s5_kernel,
        grid=(S5_G,),
        in_specs=[io, mat, mat,
                  pl.BlockSpec((None, 2 * S5_CW, S5_CW), lambda g: (g, 0, 0)), vec, vec,
                  pl.BlockSpec((None, S5_CW, 1), lambda g: (g, 0, 0))],
        out_specs=io,
        out_shape=jax.ShapeDtypeStruct((S5_G, S5_LC, S5_CG, S5_N), BF16),
        scratch_shapes=[pltpu.VMEM((S5_ROWS, 2 * S5_P), F32)] * 6,
        compiler_params=_params("parallel"),
        name="s5_scan",
    )(ut, tc, et, ft, a_re, a_im, d_col)


def _glu_kernel(gl_ref, w_ref, b_ref, o_ref):
    gl = gl_ref[...].reshape(S5_W, S5_NT)
    z = jnp.dot(w_ref[...], gl, preferred_element_type=F32) + b_ref[...]
    s = gl.astype(F32) * jax.nn.sigmoid(z)
    o_ref[...] = s.T.astype(o_ref.dtype)


def _s5_glu(glt, w_t, b_col):
    return pl.pallas_call(
        _glu_kernel,
        grid=(S5_LC, S5_N // S5_NT),
        in_specs=[pl.BlockSpec((S5_G, None, S5_CG, S5_NT), lambda t, n: (0, t, 0, n)),
                  pl.BlockSpec((S5_W, S5_W), lambda t, n: (0, 0)),
                  pl.BlockSpec((S5_W, 1), lambda t, n: (0, 0))],
        out_specs=pl.BlockSpec((S5_NT, S5_W), lambda t, n: (n, t)),
        out_shape=jax.ShapeDtypeStruct((S5_N, S5_LC * S5_W), BF16),
        compiler_params=_params("parallel", "parallel"),
        name="s5_glu",
    )(glt, w_t, b_col)


def _rope_tables():
    half = GLA_DK // 2
    freqs = ROPE_BASE ** (-np.arange(0, half, 2, dtype=np.float32) / half)
    out = []
    for n in (ROWS, GRID_W):
        ang = np.arange(n, dtype=np.float32)[:, None] * freqs[None, :]
        c, s = np.cos(ang), np.sin(ang)
        out += [np.concatenate([c, c], axis=-1), np.concatenate([-s, s], axis=-1)]
    return tuple(jnp.asarray(t, F32) for t in out)


def _rope(x, rcos, rsin, ccos, csin):
    x0 = x[:, :128]
    x1 = x[:, 128:]
    return jnp.concatenate([x0 * rcos + pltpu.roll(x0, 64, axis=1) * rsin,
                            x1 * ccos + pltpu.roll(x1, 64, axis=1) * csin], axis=-1)


def _cumsum_rows(x, reverse):
    row = lax.broadcasted_iota(jnp.int32, x.shape, 0)
    s = 1
    while s < GLA_C:
        if s >= 8:
            if reverse:
                x = jnp.concatenate([x[:GLA_C - s] + x[s:], x[GLA_C - s:]], axis=0)
            else:
                x = jnp.concatenate([x[:s], x[s:] + x[:GLA_C - s]], axis=0)
        elif reverse:
            x = x + jnp.where(row < GLA_C - s, pltpu.roll(x, GLA_C - s, axis=0), 0.0)
        else:
            x = x + jnp.where(row >= s, pltpu.roll(x, s, axis=0), 0.0)
        s *= 2
    return x


def _log_sigmoid(x):
    return jnp.minimum(x, 0.0) - jnp.log(1.0 + jnp.exp(-jnp.abs(x)))


def _chunk_rows(c):
    if isinstance(c, int):
        return pl.ds(c * GLA_C, GLA_C)
    return pl.ds(pl.multiple_of(c * GLA_C, GLA_C), GLA_C)


def _gla_kernel(ql_ref, kl_ref, vl_ref, gl_ref, qc_ref, kc_ref, vc_ref, al_ref, ac_ref,
                waf_ref, wab_ref, baf_ref, bab_ref, rcos_ref, rsin_ref, ccos_ref, csin_ref,
                ng_ref, o_ref,
                qi_f, ki_f, ke_f, dec_f, st_f, qi_b, ki_b, ke_b, dec_b, st_b, acc_ref):
    n_ctx = L // GLA_C
    n_lat = S // GLA_C
    qscale = GLA_DK ** -0.5
    fwd = (waf_ref, baf_ref, qi_f, ki_f, ke_f, dec_f, False)
    bwd = (wab_ref, bab_ref, qi_b, ki_b, ke_b, dec_b, True)

    ii = lax.broadcasted_iota(jnp.int32, (GLA_C, GLA_C), 0)
    jj = lax.broadcasted_iota(jnp.int32, (GLA_C, GLA_C), 1)

    def prepare(direction, q, k, a, c):
        wa_ref, ba_ref, qi, ki, ke, dec, reverse = direction
        dst = _chunk_rows(c)
        la = _log_sigmoid(jnp.dot(a, wa_ref[...], preferred_element_type=F32)
                          + ba_ref[...]) / GLA_TAU
        bc = _cumsum_rows(la, reverse)
        b_last = bc[0:1, :] if reverse else bc[GLA_C - 1:GLA_C, :]
        qi[dst, :] = (q * jnp.exp(bc)).astype(BF16)
        ki[dst, :] = (k * jnp.exp(-bc)).astype(BF16)
        ke[dst, :] = (k * jnp.exp(b_last - bc)).astype(BF16)
        dec[pl.ds(c, 1), :] = jnp.exp(b_last)

    def prepare_latent(direction, c):
        r = _chunk_rows(c)
        tabs = (rcos_ref[pl.ds(c, 1), :], rsin_ref[pl.ds(c, 1), :], ccos_ref[...], csin_ref[...])
        q = _rope(ql_ref[r, :].astype(F32) * qscale, *tabs)
        k = _rope(kl_ref[r, :].astype(F32), *tabs)
        prepare(direction, q, k, al_ref[r, :], n_ctx + c)

    def prepare_context(direction, c):
        r = _chunk_rows(c)
        prepare(direction, qc_ref[r, :].astype(F32) * qscale, kc_ref[r, :].astype(F32),
                ac_ref[r, :], c)

    prepare_context(fwd, 0)
    prepare_context(bwd, n_ctx - 1)
    prepare_latent(fwd, 0)
    prepare_latent(bwd, n_lat - 1)

    def advance(direction, st_ref, c, v, want_out):
        _, _, qi, ki, ke, dec, reverse = direction
        r = _chunk_rows(c)
        st = st_ref[...]
        o = None
        if want_out:
            q_in = qi[r, :]
            att = lax.dot_general(q_in, ki[r, :], _NT_DIMS, preferred_element_type=F32)
            att = jnp.where((ii <= jj) if reverse else (ii >= jj), att, 0.0)
            o = (jnp.dot(att.astype(BF16), v, preferred_element_type=F32)
                 + lax.dot_general(q_in, st.astype(BF16), _NT_DIMS, preferred_element_type=F32))
        st_ref[...] = dec[pl.ds(c, 1), :] * st + lax.dot_general(
            v, ke[r, :], (((0,), (0,)), ((), ())), preferred_element_type=F32)
        return o

    st_f[...] = jnp.zeros_like(st_f)
    st_b[...] = jnp.zeros_like(st_b)

    def ctx_pair(j, carry):
        cb = n_ctx - 1 - j
        advance(fwd, st_f, j, vc_ref[_chunk_rows(j), :], False)
        advance(bwd, st_b, cb, vc_ref[_chunk_rows(cb), :], False)
        prepare_context(fwd, jnp.minimum(j + 1, n_ctx - 1))
        prepare_context(bwd, jnp.maximum(cb - 1, 0))
        return carry
    lax.fori_loop(0, n_ctx, ctx_pair, 0, unroll=2)

    def lat_pair(j, accumulate):
        cb = n_lat - 1 - j
        rf = _chunk_rows(j)
        rb = _chunk_rows(cb)
        o_f = advance(fwd, st_f, n_ctx + j, vl_ref[rf, :], True)
        o_b = advance(bwd, st_b, n_ctx + cb, vl_ref[rb, :], True)
        if accumulate:
            acc_ref[rf, :] += o_f
            acc_ref[rb, :] += o_b
        else:
            acc_ref[rf, :] = o_f
            acc_ref[rb, :] = o_b
        prepare_latent(fwd, jnp.minimum(j + 1, n_lat - 1))
        prepare_latent(bwd, jnp.maximum(cb - 1, 0))

    def lat_first(j, carry):
        lat_pair(j, False)
        return carry
    lax.fori_loop(0, n_lat // 2, lat_first, 0, unroll=2)

    def lat_second(j, carry):
        lat_pair(j, True)
        return carry
    lax.fori_loop(n_lat // 2, n_lat, lat_second, 0, unroll=2)

    tr = 256

    def fin(t, carry):
        r = pl.ds(pl.multiple_of(t * tr, tr), tr)
        o = acc_ref[r, :]
        ms = jnp.mean(o * o, axis=-1, keepdims=True)
        g = gl_ref[r, :].astype(F32)
        o_ref[r, :] = (o * lax.rsqrt(ms + EPS) * ng_ref[...]
                       * (g * jax.nn.sigmoid(g))).astype(o_ref.dtype)
        return carry
    lax.fori_loop(0, S // tr, fin, 0)


def _gla(qkvg, acode, wa, ba, rope, norm_g):
    ctx0 = NLAT // L
    kq = GLA_QK // GLA_DK
    half = GLA_DK // 2
    n_chunks = (L + S) // GLA_C
    per_dir = [pltpu.VMEM((L + S, GLA_DK), BF16)] * 3 + [pltpu.VMEM((n_chunks, GLA_DK), F32),
                                                         pltpu.VMEM((GLA_DV, GLA_DK), F32)]
    in_specs = [
        pl.BlockSpec((S, GLA_DK), lambda b, h: (b, h)),
        pl.BlockSpec((S, GLA_DK), lambda b, h: (b, kq + h)),
        pl.BlockSpec((S, GLA_DV), lambda b, h: (b, kq + h)),
        pl.BlockSpec((S, GLA_DV), lambda b, h: (b, 2 * kq + h)),
        pl.BlockSpec((L, GLA_DK), lambda b, h: (ctx0 + b, h)),
        pl.BlockSpec((L, GLA_DK), lambda b, h: (ctx0 + b, kq + h)),
        pl.BlockSpec((L, GLA_DV), lambda b, h: (ctx0 + b, kq + h)),
        pl.BlockSpec((S, 128), lambda b, h: (b, 0)),
        pl.BlockSpec((L, 128), lambda b, h: (ctx0 + b, 0)),
        pl.BlockSpec((128, GLA_DK), lambda b, h: (0, h)),
        pl.BlockSpec((128, GLA_DK), lambda b, h: (0, kq + h)),
        pl.BlockSpec((1, GLA_DK), lambda b, h: (0, h)),
        pl.BlockSpec((1, GLA_DK), lambda b, h: (0, kq + h)),
        pl.BlockSpec((ROWS, half), lambda b, h: (0, 0)),
        pl.BlockSpec((ROWS, half), lambda b, h: (0, 0)),
        pl.BlockSpec((GRID_W, half), lambda b, h: (0, 0)),
        pl.BlockSpec((GRID_W, half), lambda b, h: (0, 0)),
        pl.BlockSpec((1, GLA_DV), lambda b, h: (0, 0)),
    ]
    return pl.pallas_call(
        _gla_kernel,
        grid=(B, GLA_H),
        in_specs=in_specs,
        out_specs=pl.BlockSpec((S, GLA_DV), lambda b, h: (b, h)),
        out_shape=jax.ShapeDtypeStruct((NLAT, GLA_VW), BF16),
        scratch_shapes=per_dir + per_dir + [pltpu.VMEM((S, GLA_DV), F32)],
        compiler_params=_params("parallel", "parallel"),
        name="gla",
    )(qkvg, qkvg, qkvg, qkvg, qkvg, qkvg, qkvg, acode, acode, wa, wa, ba, ba, *rope,
      norm_g.reshape(1, GLA_DV))


def kernel(x, c, ctx, c_ctx, ada_w, ada_b, norm1_g, norm2_g, mlp_w1, mlp_w2, final_g, ab_w_in, ab_w_out, na_rel_bias, s5_lambda_re, s5_lambda_im, s5_log_dt, s5_b_re, s5_b_im, s5_c_re, s5_c_im, s5_d, s5_glu_w, s5_glu_b, gla_w_in, gla_w_a2, gla_b_a, gla_norm_g, gla_w_out):
    xs = (x.astype(F32).reshape(NLAT, D), ctx.astype(F32).reshape(NCTX, D))
    cvec = jnp.zeros((16, D), F32).at[:B].set(c.astype(F32)).at[B].set(c_ctx.astype(F32))
    mods = _ada_mod(cvec, ada_w, ada_b).reshape(2, 16, 1, 6 * D)
    bf = lambda w: w.astype(BF16)

    mod = mods[0]
    h, h_tl = _normmod(xs, mod, norm1_g[0], 0, 1, NT)
    qkv, w_ab_out = _mm(h, ab_w_in, 0, 3 * NA_W, NT, side=(ab_w_out, 0))
    att = _na_attention(qkv, _na_tables(na_rel_bias[0]))
    ut = _s5_uproj(h_tl, ab_w_in, 3 * NA_W // S5_W)
    mats = _s5_matrices(s5_lambda_re[0], s5_lambda_im[0], s5_log_dt[0], s5_b_re[0], s5_b_im[0],
                        s5_c_re[0], s5_c_im[0])
    d_col = jnp.tile(s5_d[0].astype(F32).reshape(S5_G, 1, S5_CG), (1, S5_LC, 1)).reshape(S5_G, S5_CW, 1)
    glt = _s5_scan(ut, mats, d_col)
    s5 = _s5_glu(glt, bf(s5_glu_w[0].T), s5_glu_b[0].astype(F32).reshape(S5_W, 1))
    xs, h = _mm_res([att, s5], w_ab_out, 0, xs, mod, 2, NT, nxt=(mod, norm2_g[0], 3, 4))
    hid, w2 = _mm(h, mlp_w1, 0, MLP_H, NT, relu2=True, side=(mlp_w2, 0))
    xs, h = _mm_res([hid], w2, 0, xs, mod, 5, NT, nxt=(mods[1], norm1_g[1], 0, 1))

    mod = mods[1]
    w_in = gla_w_in[0]
    qkvg, w_gla_out = _mm(h, gla_w_in, 0, GLA_MAIN, NT, side=(gla_w_out, 0))
    w_code = jnp.zeros((1, D, 128), F32).at[0, :, :2 * GLA_RANK].set(w_in[:, GLA_MAIN:])
    acode = _mm(h, w_code, 0, 128, NT)
    wa = (jnp.zeros((128, 2 * GLA_QK), F32)
          .at[:GLA_RANK, :GLA_QK].set(gla_w_a2[0, 0])
          .at[GLA_RANK:2 * GLA_RANK, GLA_QK:].set(gla_w_a2[0, 1]))
    ba = gla_b_a[0].astype(F32).reshape(1, 2 * GLA_QK)
    og = _gla(qkvg, acode, bf(wa), ba, _rope_tables(), gla_norm_g[0].astype(F32))
    xl, h = _mm_res([og], w_gla_out, 0, xs, mod, 2, NLAT, nxt=(mod, norm2_g[1], 3, 4))
    hid, w2 = _mm(h, mlp_w1, 1, MLP_H, NLAT, relu2=True, side=(mlp_w2, 1))
    out = _mm_res([hid], w2, 0, xl, mod, 5, NLAT, final_g=final_g.astype(F32))
    return out.reshape(B, S, D).astype(x.dtype)
```
